```python
import math
import jax
import jax.numpy as jnp
from jax import lax
import numpy as np

D_MODEL = 1024
BATCH = 8
SEQ = 2048
DEPTH = 2

CTX_LEN = 256
GRID_W = 64
N_MOD = 9
D_FF = 2816
RMS_EPS = 1e-6
ROPE_BASE = 10000.0

GLA_HEADS = 4
GLA_DK = 32
GLA_DV = 64
GLA_GATE_RANK = 16
GLA_TAU = 16.0
GLA_CHUNK = 64
SWA_HEADS = 8
SWA_KV_HEADS = 2
SWA_HD = 64
SWA_WINDOW = 128
SWA_BLOCK = 128
DIFF_HEADS = 4
DIFF_QK = 32
DIFF_V = 64
DIFF_QBLOCK = 128

GLA_QK_W = GLA_HEADS * GLA_DK
GLA_WIDTH = GLA_HEADS * GLA_DV
SWA_WIDTH = SWA_HEADS * SWA_HD
SWA_KV_W = SWA_KV_HEADS * SWA_HD
DIFF_QK_W = DIFF_HEADS * 2 * DIFF_QK
DIFF_WIDTH = DIFF_HEADS * DIFF_V
MIX_WIDTH = GLA_WIDTH + SWA_WIDTH + DIFF_WIDTH
IN_SIZES = (GLA_QK_W, GLA_QK_W, GLA_WIDTH, GLA_GATE_RANK, GLA_GATE_RANK, GLA_WIDTH,
            SWA_WIDTH, SWA_KV_W, SWA_KV_W,
            DIFF_QK_W, DIFF_QK_W, DIFF_WIDTH)
IN_COLS = 2 * GLA_QK_W + 2 * GLA_WIDTH + 2 * GLA_GATE_RANK + SWA_WIDTH + 2 * SWA_KV_W + 2 * DIFF_QK_W + DIFF_WIDTH

kernel_name = 'hybrid_gla_swa_diffattn_macaron_dit'


def rms_norm(x, g):
    xf = x.astype(jnp.float32)
    y = xf * lax.rsqrt(jnp.mean(xf * xf, axis=-1, keepdims=True) + RMS_EPS)
    return (y * g.astype(jnp.float32)).astype(x.dtype)


def modulate(h, shift, scale):
    return h * (1 + scale) + shift


def swiglu(h, w_in, w_out):
    gate, up = jnp.split(h @ w_in, 2, axis=-1)
    return (jax.nn.silu(gate) * up) @ w_out


def to_heads(t, n_heads):
    b, s, _ = t.shape
    return t.reshape(b, s, n_heads, -1).transpose(0, 2, 1, 3)


def from_heads(t):
    b, h, s, d = t.shape
    return t.transpose(0, 2, 1, 3).reshape(b, s, h * d)


def head_rms(o, g, dtype):
    of = o.astype(jnp.float32)
    of = of * lax.rsqrt(jnp.mean(of * of, axis=-1, keepdims=True) + RMS_EPS)
    return (from_heads(of) * g.astype(jnp.float32)).astype(dtype)


def split_cols(p, sizes):
    parts, off = [], 0
    for sz in sizes:
        parts.append(p[..., off:off + sz])
        off += sz
    return parts


def axial_rope_tables(rows, head_dim):
    half = head_dim // 2
    row = jnp.repeat(jnp.arange(rows, dtype=jnp.float32), GRID_W)
    col = jnp.tile(jnp.arange(GRID_W, dtype=jnp.float32), rows)
    inv_freq = 1.0 / (ROPE_BASE ** (jnp.arange(0, half, 2, dtype=jnp.float32) / half))
    def axis_angles(pos):
        a = pos[:, None] * inv_freq[None, :]
        return jnp.concatenate([a, a], axis=-1)
    ang = jnp.concatenate([axis_angles(row), axis_angles(col)], axis=-1)
    return jnp.cos(ang), jnp.sin(ang)


def apply_axial_rope(x, cos, sin):
    half = x.shape[-1] // 2
    quarter = half // 2
    def rot(t):
        return jnp.concatenate([-t[..., quarter:], t[..., :quarter]], axis=-1)
    xr = jnp.concatenate([rot(x[..., :half]), rot(x[..., half:])], axis=-1)
    return (x * cos + xr * sin).astype(x.dtype)


def gla_log_gate(z_low, w_up, b_up):
    z = z_low.astype(jnp.float32) @ w_up.astype(jnp.float32) + b_up.astype(jnp.float32)
    return to_heads(jax.nn.log_sigmoid(z) / GLA_TAU, GLA_HEADS)


def gla_scan(q, k, v, log_g, s0):
    b, h, t, _ = k.shape
    n = t // GLA_CHUNK
    with_out = q is not None
    def chunks(a):
        return a.reshape(b, h, n, GLA_CHUNK, a.shape[-1]).transpose(2, 0, 1, 3, 4)
    causal = jnp.tril(jnp.ones((GLA_CHUNK, GLA_CHUNK), dtype=bool))[:, :, None]
    def step(state, inp):
        kc, vc, gc = inp[0], inp[1], inp[2]
        cum = jnp.cumsum(gc, axis=2)
        last = cum[:, :, -1:, :]
        new_state = (jnp.exp(last[:, :, 0, :, None]) * state
                     + jnp.einsum('bhjd,bhjv->bhdv', kc * jnp.exp(last - cum), vc))
        if not with_out:
            return new_state, None
        qc = inp[3]
        o_inter = jnp.einsum('bhid,bhdv->bhiv', qc * jnp.exp(cum), state)
        decay = jnp.exp(jnp.where(causal, cum[:, :, :, None, :] - cum[:, :, None, :, :], -jnp.inf))
        att = jnp.einsum('bhid,bhjd,bhijd->bhij', qc, kc, decay)
        return new_state, o_inter + jnp.einsum('bhij,bhjv->bhiv', att, vc)
    xs = (chunks(k), chunks(v), chunks(log_g)) + ((chunks(q),) if with_out else ())
    state, o = lax.scan(step, s0, xs)
    if not with_out:
        return None, state
    return o.transpose(1, 2, 0, 3, 4).reshape(b, h, t, GLA_DV), state


def gla_mixer(xp, cp, w_gate, b_gate, g_norm, ctx_out):
    dtype = xp[2].dtype
    def prep(parts):
        q, k, v, gf, gb, og = parts
        q = to_heads(q, GLA_HEADS).astype(jnp.float32) * (GLA_DK ** -0.5)
        k = to_heads(k, GLA_HEADS).astype(jnp.float32)
        v = to_heads(v, GLA_HEADS).astype(jnp.float32)
        return q, k, v, gla_log_gate(gf, w_gate[0], b_gate[0]), gla_log_gate(gb, w_gate[1], b_gate[1]), og
    def flip(a):
        return jnp.flip(a, axis=2)
    def finish(o, og):
        return (head_rms(o, g_norm, jnp.float32) * jax.nn.silu(og.astype(jnp.float32))).astype(dtype)
    qc, kc, vc, lfc, lbc, ogc = prep(cp)
    s0 = jnp.zeros((kc.shape[0], GLA_HEADS, GLA_DK, GLA_DV), jnp.float32)
    oc_f, st_f = gla_scan(qc if ctx_out else None, kc, vc, lfc, s0)
    oc_b, st_b = gla_scan(flip(qc) if ctx_out else None, flip(kc), flip(vc), flip(lbc), s0)
    qx, kx, vx, lfx, lbx, ogx = prep(xp)
    ox_f, _ = gla_scan(qx, kx, vx, lfx, st_f)
    ox_b, _ = gla_scan(flip(qx), flip(kx), flip(vx), flip(lbx), st_b)
    y_x = finish(ox_f + flip(ox_b), ogx)
    y_c = finish(oc_f + flip(oc_b), ogc) if ctx_out else None
    return y_x, y_c


def swa_mixer(xp, cp, sink, rope, ctx_out):
    cos, sin = rope
    xq, xk, xv = xp
    cq, ck, cv = cp
    dtype = xv.dtype
    b, s, _ = xq.shape
    n_ctx = ck.shape[1]
    grp = SWA_HEADS // SWA_KV_HEADS
    nb = s // SWA_BLOCK
    band_w = 3 * SWA_BLOCK
    scale = SWA_HD ** -0.5
    sink_f = sink.astype(jnp.float32)
    q = apply_axial_rope(to_heads(xq, SWA_HEADS), cos, sin).reshape(b, SWA_KV_HEADS, grp, nb, SWA_BLOCK, SWA_HD)
    k = apply_axial_rope(to_heads(xk, SWA_KV_HEADS), cos, sin)
    v = to_heads(xv, SWA_KV_HEADS)
    kc = to_heads(ck, SWA_KV_HEADS)
    vc = to_heads(cv, SWA_KV_HEADS)
    def band(a):
        ap = jnp.pad(a, ((0, 0), (0, 0), (SWA_BLOCK, SWA_BLOCK), (0, 0))).reshape(b, SWA_KV_HEADS, nb + 2, SWA_BLOCK, SWA_HD)
        return jnp.concatenate([ap[:, :, :nb], ap[:, :, 1:nb + 1], ap[:, :, 2:]], axis=3)
    kb, vb = band(k), band(v)
    blk = jnp.arange(nb)[:, None] * SWA_BLOCK
    qpos = blk + jnp.arange(SWA_BLOCK)[None, :]
    kpos = blk - SWA_BLOCK + jnp.arange(band_w)[None, :]
    mask = ((jnp.abs(qpos[:, :, None] - kpos[:, None, :]) <= SWA_WINDOW)
            & (kpos[:, None, :] >= 0) & (kpos[:, None, :] < s))
    s_band = jnp.where(mask, jnp.einsum('bkgnqd,bknjd->bkgnqj', q, kb).astype(jnp.float32) * scale, -jnp.inf)
    s_ctx = jnp.einsum('bkgnqd,bkld->bkgnql', q, kc).astype(jnp.float32) * scale
    s_sink = jnp.broadcast_to(sink_f.reshape(1, SWA_KV_HEADS, grp, 1, 1, 1), s_ctx.shape[:-1] + (1,))
    p = jax.nn.softmax(jnp.concatenate([s_band, s_ctx, s_sink], axis=-1), axis=-1)
    o = (jnp.einsum('bkgnqj,bknjd->bkgnqd', p[..., :band_w].astype(dtype), vb)
         + jnp.einsum('bkgnql,bkld->bkgnqd', p[..., band_w:band_w + n_ctx].astype(dtype), vc))
    y_x = from_heads(o.reshape(b, SWA_HEADS, s, SWA_HD))
    y_c = None
    if ctx_out:
        qc = to_heads(cq, SWA_HEADS).reshape(b, SWA_KV_HEADS, grp, n_ctx, SWA_HD)
        sc = jnp.einsum('bkgqd,bkld->bkgql', qc, kc).astype(jnp.float32) * scale
        sc_sink = jnp.broadcast_to(sink_f.reshape(1, SWA_KV_HEADS, grp, 1, 1), sc.shape[:-1] + (1,))
        pc = jax.nn.softmax(jnp.concatenate([sc, sc_sink], axis=-1), axis=-1)
        oc = jnp.einsum('bkgql,bkld->bkgqd', pc[..., :n_ctx].astype(dtype), vc)
        y_c = from_heads(oc.reshape(b, SWA_HEADS, n_ctx, SWA_HD))
    return y_x, y_c


def diff_mixer(xp, cp, lam_vecs, g_norm, lambda_init, rope, ctx_out):
    cos, sin = rope
    xq, xk, xv = xp
    cq, ck, cv = cp
    dtype = xv.dtype
    b, s, _ = xq.shape
    nb = s // DIFF_QBLOCK
    scale = DIFF_QK ** -0.5
    lv = lam_vecs.astype(jnp.float32)
    lam = jnp.exp(jnp.sum(lv[0] * lv[1])) - jnp.exp(jnp.sum(lv[2] * lv[3])) + lambda_init
    def pair(t, rotary):
        hh = to_heads(t, DIFF_HEADS)
        h1, h2 = hh[..., :DIFF_QK], hh[..., DIFF_QK:]
        if rotary:
            return apply_axial_rope(h1, cos, sin), apply_axial_rope(h2, cos, sin)
        return h1, h2
    def attend(q1, q2, k1, k2, v):
        p1 = jax.nn.softmax(jnp.einsum('bhqd,bhkd->bhqk', q1, k1).astype(jnp.float32) * scale, axis=-1)
        p2 = jax.nn.softmax(jnp.einsum('bhqd,bhkd->bhqk', q2, k2).astype(jnp.float32) * scale, axis=-1)
        return jnp.einsum('bhqk,bhkd->bhqd', (p1 - lam * p2).astype(dtype), v)
    q1, q2 = pair(xq, True)
    k1, k2 = pair(xk, True)
    v = to_heads(xv, DIFF_HEADS)
    k1c, k2c = pair(ck, False)
    vc = to_heads(cv, DIFF_HEADS)
    k1a = jnp.concatenate([k1, k1c], axis=2)
    k2a = jnp.concatenate([k2, k2c], axis=2)
    va = jnp.concatenate([v, vc], axis=2)
    def blocks(a):
        return a.reshape(b, DIFF_HEADS, nb, DIFF_QBLOCK, DIFF_QK).transpose(2, 0, 1, 3, 4)
    ob = lax.map(lambda qb: attend(qb[0], qb[1], k1a, k2a, va), (blocks(q1), blocks(q2)))
    o = ob.transpose(1, 2, 0, 3, 4).reshape(b, DIFF_HEADS, s, DIFF_V)
    out_scale = 1.0 - lambda_init
    y_x = head_rms(o, g_norm, dtype) * out_scale
    y_c = None
    if ctx_out:
        q1c, q2c = pair(cq, False)
        y_c = head_rms(attend(q1c, q2c, k1c, k2c, vc), g_norm, dtype) * out_scale
    return y_x, y_c


def hybrid_layer(x, ctx, mod_x, mod_c, g_ffn1, w_ffn1_in, w_ffn1_out, g_mix, w_in, w_out,
                 w_gla_gate, b_gla_gate, g_gla_norm, swa_sink, diff_lambda, g_diff_norm,
                 g_ffn2, w_ffn2_in, w_ffn2_out, lambda_init, rope_swa, rope_diff, ctx_out):
    def m(mod, i):
        return mod[:, :, i]
    x = x + 0.5 * m(mod_x, 2) * swiglu(modulate(rms_norm(x, g_ffn1), m(mod_x, 0), m(mod_x, 1)), w_ffn1_in, w_ffn1_out)
    ctx = ctx + 0.5 * m(mod_c, 2) * swiglu(modulate(rms_norm(ctx, g_ffn1), m(mod_c, 0), m(mod_c, 1)), w_ffn1_in, w_ffn1_out)
    px = split_cols(modulate(rms_norm(x, g_mix), m(mod_x, 3), m(mod_x, 4)) @ w_in, IN_SIZES)
    pc = split_cols(modulate(rms_norm(ctx, g_mix), m(mod_c, 3), m(mod_c, 4)) @ w_in, IN_SIZES)
    gla_x, gla_c = gla_mixer(px[0:6], pc[0:6], w_gla_gate, b_gla_gate, g_gla_norm, ctx_out)
    swa_x, swa_c = swa_mixer(px[6:9], pc[6:9], swa_sink, rope_swa, ctx_out)
    diff_x, diff_c = diff_mixer(px[9:12], pc[9:12], diff_lambda, g_diff_norm, lambda_init, rope_diff, ctx_out)
    x = x + m(mod_x, 5) * (jnp.concatenate([gla_x, swa_x, diff_x], axis=-1) @ w_out)
    x = x + 0.5 * m(mod_x, 8) * swiglu(modulate(rms_norm(x, g_ffn2), m(mod_x, 6), m(mod_x, 7)), w_ffn2_in, w_ffn2_out)
    if ctx_out:
        ctx = ctx + m(mod_c, 5) * (jnp.concatenate([gla_c, swa_c, diff_c], axis=-1) @ w_out)
        ctx = ctx + 0.5 * m(mod_c, 8) * swiglu(modulate(rms_norm(ctx, g_ffn2), m(mod_c, 6), m(mod_c, 7)), w_ffn2_in, w_ffn2_out)
    return x, ctx


def setup_inputs(seed: int = 0) -> dict:
    key = jax.random.key(seed)
    ks = jax.random.split(key, 24)
    L = DEPTH
    D = D_MODEL
    def nrm(k, shape, sd):
        return jax.random.normal(k, shape, jnp.float32) * sd
    def gain(k, shape):
        return 1.0 + 0.01 * jax.random.normal(k, shape, jnp.float32)
    return {
        'x': nrm(ks[0], (BATCH, SEQ, D), 1.0),
        'c': nrm(ks[1], (BATCH, D), 1.0),
        'ctx': nrm(ks[2], (BATCH, CTX_LEN, D), 1.0),
        'c_ctx': nrm(ks[3], (D,), 1.0),
        'w_mod': nrm(ks[4], (L, D, N_MOD * D), 0.3 * D ** -0.5),
        'b_mod': nrm(ks[5], (L, N_MOD * D), 0.01),
        'g_ffn1': gain(ks[6], (L, D)),
        'w_ffn1_in': nrm(ks[7], (L, D, 2 * D_FF), D ** -0.5),
        'w_ffn1_out': nrm(ks[8], (L, D_FF, D), D_FF ** -0.5),
        'g_mix': gain(ks[9], (L, D)),
        'w_in': nrm(ks[10], (L, D, IN_COLS), D ** -0.5),
        'w_out': nrm(ks[11], (L, MIX_WIDTH, D), MIX_WIDTH ** -0.5),
        'w_gla_gate': nrm(ks[12], (L, 2, GLA_GATE_RANK, GLA_QK_W), GLA_GATE_RANK ** -0.5),
        'b_gla_gate': nrm(ks[13], (L, 2, GLA_QK_W), 0.1),
        'g_gla_norm': gain(ks[14], (L, GLA_WIDTH)),
        'swa_sink': nrm(ks[15], (L, SWA_HEADS), 0.5),
        'diff_lambda': nrm(ks[16], (L, 4, DIFF_QK), 0.1),
        'g_diff_norm': gain(ks[17], (L, DIFF_WIDTH)),
        'g_ffn2': gain(ks[18], (L, D)),
        'w_ffn2_in': nrm(ks[19], (L, D, 2 * D_FF), D ** -0.5),
        'w_ffn2_out': nrm(ks[20], (L, D_FF, D), D_FF ** -0.5),
        'g_final': gain(ks[21], (D,)),
    }


def reference(x, c, ctx, c_ctx, w_mod, b_mod, g_ffn1, w_ffn1_in, w_ffn1_out, g_mix, w_in, w_out,
              w_gla_gate, b_gla_gate, g_gla_norm, swa_sink, diff_lambda, g_diff_norm,
              g_ffn2, w_ffn2_in, w_ffn2_out, g_final):
    b, s, d = x.shape
    rows = s // GRID_W
    rope_swa = axial_rope_tables(rows, SWA_HD)
    rope_diff = axial_rope_tables(rows, DIFF_QK)
    for l in range(DEPTH):
        mod_x = (jax.nn.silu(c) @ w_mod[l] + b_mod[l]).reshape(b, 1, N_MOD, d)
        mod_c = (jax.nn.silu(c_ctx) @ w_mod[l] + b_mod[l]).reshape(1, 1, N_MOD, d)
        lambda_init = 0.8 - 0.6 * math.exp(-0.3 * l)
        x, ctx = hybrid_layer(x, ctx, mod_x, mod_c, g_ffn1[l], w_ffn1_in[l], w_ffn1_out[l], g_mix[l],
                              w_in[l], w_out[l], w_gla_gate[l], b_gla_gate[l], g_gla_norm[l], swa_sink[l],
                              diff_lambda[l], g_diff_norm[l], g_ffn2[l], w_ffn2_in[l], w_ffn2_out[l],
                              lambda_init, rope_swa, rope_diff, l < DEPTH - 1)
    return rms_norm(x, g_final)
```

```python
import functools
import math

import numpy as np
import jax
import jax.numpy as jnp
from jax import lax
from jax.experimental import pallas as pl
from jax.experimental.pallas import tpu as pltpu

F32 = jnp.float32
BF16 = jnp.bfloat16

D_MODEL = 1024
DEPTH = 2
GRID_W = 64
N_MOD = 9
D_FF = 2816
RMS_EPS = 1e-6
ROPE_BASE = 10000.0

GLA_HEADS = 4
GLA_DK = 32
GLA_DV = 64
GLA_GATE_RANK = 16
GLA_TAU = 16.0
SWA_HEADS = 8
SWA_KV_HEADS = 2
SWA_HD = 64
SWA_WINDOW = 128
SWA_BLOCK = 128
DIFF_HEADS = 4
DIFF_QK = 32
DIFF_V = 64
DIFF_QBLOCK = 128

GLA_QK_W = GLA_HEADS * GLA_DK
GLA_WIDTH = GLA_HEADS * GLA_DV
SWA_WIDTH = SWA_HEADS * SWA_HD
SWA_KV_W = SWA_KV_HEADS * SWA_HD
DIFF_QK_W = DIFF_HEADS * 2 * DIFF_QK
DIFF_WIDTH = DIFF_HEADS * DIFF_V
MIX_WIDTH = GLA_WIDTH + SWA_WIDTH + DIFF_WIDTH
IN_SIZES = (GLA_QK_W, GLA_QK_W, GLA_WIDTH, GLA_GATE_RANK, GLA_GATE_RANK, GLA_WIDTH,
            SWA_WIDTH, SWA_KV_W, SWA_KV_W, DIFF_QK_W, DIFF_QK_W, DIFF_WIDTH)

LANE = 128
VMEM_LIMIT = 56 * 1024 * 1024
LOG2E = math.log2(math.e)

PG_W = 896
PS_W = 1024
PD_W = 768
P_W = PG_W + PS_W + PD_W
GLA_CHUNK = 64
GLA_LEVELS = 6
MOD_ROWS = 16
MOD_TN = 1152
FFN_TM = 512
FFN_TF = D_FF // 2


def _dot(a, b):
    return jnp.dot(a, b, preferred_element_type=F32)


def _dot_nt(a, b):
    return lax.dot_general(a, b, (((1,), (1,)), ((), ())), preferred_element_type=F32)


def _dot_tn(a, b):
    return lax.dot_general(a, b, (((0,), (0,)), ((), ())), preferred_element_type=F32)


def _split_bf16(x):
    hi = x.astype(BF16)
    lo = (x - hi.astype(F32)).astype(BF16)
    return hi, lo


def _rms(x, g):
    return x * lax.rsqrt(jnp.mean(x * x, axis=-1, keepdims=True) + RMS_EPS) * g


def _silu(x):
    return x * (1.0 / (1.0 + jnp.exp(-x)))


def _params(n_grid):
    return pltpu.CompilerParams(dimension_semantics=("arbitrary",) * n_grid, vmem_limit_bytes=VMEM_LIMIT)


def _const_spec(shape, n_grid, single=False):
    zeros = (0,) * len(shape)
    index_map = {1: lambda a: zeros, 2: lambda a, b: zeros}[n_grid]
    if single:
        return pl.BlockSpec(shape, index_map, pipeline_mode=pl.Buffered(1))
    return pl.BlockSpec(shape, index_map)


def _mod_kernel(a_ref, w_ref, b_ref, o_ref):
    a = _silu(a_ref[...]).astype(BF16)
    o_ref[0] = _dot(a, w_ref[0].astype(BF16)) + b_ref[0]


def _mod_call(cs, w_mod, b_mod):
    n_layers, d, n = w_mod.shape
    return pl.pallas_call(
        _mod_kernel,
        grid=(n_layers, n // MOD_TN),
        in_specs=[pl.BlockSpec((MOD_ROWS, d), lambda l, j: (0, 0)),
                  pl.BlockSpec((1, d, MOD_TN), lambda l, j: (l, 0, j)),
                  pl.BlockSpec((1, 1, MOD_TN), lambda l, j: (l, 0, j))],
        out_specs=pl.BlockSpec((1, MOD_ROWS, MOD_TN), lambda l, j: (l, 0, j)),
        out_shape=jax.ShapeDtypeStruct((n_layers, MOD_ROWS, n), F32),
        compiler_params=_params(2),
        name="mod",
    )(cs, w_mod, b_mod.reshape(n_layers, 1, n))


def _ffn_kernel(*refs, mod_off, final):
    if final:
        h_ref, mod_ref, g_ref, win_ref, wout_ref, gfin_ref, o_ref = refs
    else:
        h_ref, mod_ref, g_ref, win_ref, wout_ref, o_ref = refs
    x = h_ref[0]
    mod = mod_ref[0]
    shift, scale, gate = mod[mod_off:mod_off + 1], mod[mod_off + 1:mod_off + 2], mod[mod_off + 2:mod_off + 3]
    y = (_rms(x, g_ref[...]) * (1.0 + scale) + shift).astype(BF16)
    acc = jnp.zeros(x.shape, F32)
    for c in range(D_FF // FFN_TF):
        gt = _dot(y, win_ref[:, c * FFN_TF:(c + 1) * FFN_TF])
        up = _dot(y, win_ref[:, D_FF + c * FFN_TF:D_FF + (c + 1) * FFN_TF])
        act = (_silu(gt) * up).astype(BF16)
        acc = acc + _dot(act, wout_ref[c * FFN_TF:(c + 1) * FFN_TF, :])
    out = x + (0.5 * gate) * acc
    if final:
        out = _rms(out, gfin_ref[...])
    o_ref[0] = out


def _ffn_call(h, mod, g, w_in, w_out, mod_off, g_final=None):
    bx, t, d = h.shape
    tm = min(FFN_TM, t)
    final = g_final is not None
    in_specs = [pl.BlockSpec((1, tm, d), lambda b, i: (b, i, 0)),
                pl.BlockSpec((1, N_MOD, d), lambda b, i: (b, 0, 0)),
                _const_spec((1, d), 2),
                _const_spec((d, 2 * D_FF), 2, single=True),
                _const_spec((D_FF, d), 2, single=True)]
    args = [h, mod, g.reshape(1, d), w_in, w_out]
    if final:
        in_specs.append(_const_spec((1, d), 2))
        args.append(g_final.reshape(1, d))
    return pl.pallas_call(
        functools.partial(_ffn_kernel, mod_off=mod_off, final=final),
        grid=(bx, t // tm),
        in_specs=in_specs,
        out_specs=pl.BlockSpec((1, tm, d), lambda b, i: (b, i, 0)),
        out_shape=jax.ShapeDtypeStruct(h.shape, F32),
        compiler_params=_params(2),
        name="ffn",
    )(*args)


SWA_QSCALE = SWA_HD ** -0.5 * LOG2E
DIFF_QSCALE = DIFF_QK ** -0.5 * LOG2E


def _rot_half(blk, quarter, first):
    return jnp.where(first, pltpu.roll(blk, LANE - quarter, 1), pltpu.roll(blk, quarter, 1))


def _inproj_kernel(*refs, rope):
    if rope:
        h_ref, mod_ref, g_ref, w_ref, tab_ref, pg_ref, ps_ref, pd_ref = refs
    else:
        h_ref, mod_ref, g_ref, w_ref, pg_ref, ps_ref, pd_ref = refs
    x = h_ref[0]
    mod = mod_ref[0]
    y = (_rms(x, g_ref[...]) * (1.0 + mod[4:5]) + mod[3:4]).astype(BF16)
    p = _dot(y, w_ref[...])
    pg_ref[0] = p[:, :PG_W]
    swa = [p[:, PG_W + i * LANE:PG_W + (i + 1) * LANE] for i in range(PS_W // LANE)]
    dif = [p[:, PG_W + PS_W + i * LANE:PG_W + PS_W + (i + 1) * LANE] for i in range(PD_W // LANE)]
    if rope:
        tab = [tab_ref[:, i * LANE:(i + 1) * LANE] for i in range(8)]
        lane = lax.broadcasted_iota(jnp.int32, (1, LANE), 1)
        first_s = (lane % (SWA_HD // 2)) < (SWA_HD // 4)
        first_d = (lane % (DIFF_QK // 2)) < (DIFF_QK // 4)
        for i in range(6):
            c, s = (tab[0], tab[1]) if i < 4 else (tab[2], tab[3])
            swa[i] = swa[i] * c + _rot_half(swa[i], SWA_HD // 4, first_s) * s
        for i in range(4):
            c, s = (tab[4], tab[5]) if i < 2 else (tab[6], tab[7])
            dif[i] = dif[i] * c + _rot_half(dif[i], DIFF_QK // 4, first_d) * s
    else:
        for i in range(4):
            swa[i] = swa[i] * SWA_QSCALE
        for i in range(2):
            dif[i] = dif[i] * DIFF_QSCALE
    ps_ref[0] = jnp.concatenate(swa, axis=1).astype(BF16)
    pd_ref[0] = jnp.concatenate(dif, axis=1).astype(BF16)


def _inproj_call(h, mod, g, w, tab=None):
    bx, t, d = h.shape
    tm = min(FFN_TM, t)
    rope = tab is not None
    in_specs = [pl.BlockSpec((1, tm, d), lambda j, b: (b, j, 0)),
                pl.BlockSpec((1, N_MOD, d), lambda j, b: (b, 0, 0)),
                _const_spec((1, d), 2),
                _const_spec((d, P_W), 2, single=True)]
    args = [h, mod, g.reshape(1, d), w]
    if rope:
        in_specs.append(pl.BlockSpec((tm, 8 * LANE), lambda j, b: (j, 0)))
        args.append(tab)
    out_map = lambda j, b: (b, j, 0)
    return pl.pallas_call(
        functools.partial(_inproj_kernel, rope=rope),
        grid=(t // tm, bx),
        in_specs=in_specs,
        out_specs=[pl.BlockSpec((1, tm, PG_W), out_map), pl.BlockSpec((1, tm, PS_W), out_map),
                   pl.BlockSpec((1, tm, PD_W), out_map)],
        out_shape=[jax.ShapeDtypeStruct((bx, t, PG_W), F32), jax.ShapeDtypeStruct((bx, t, PS_W), BF16),
                   jax.ShapeDtypeStruct((bx, t, PD_W), BF16)],
        compiler_params=_params(2),
        name="inproj",
    )(*args)


GLA_SROWS = (GLA_LEVELS + 2) * GLA_CHUNK
GLA_LV_DIAG = GLA_LEVELS
GLA_LV_NONE = GLA_LEVELS + 1


def _gla_constants():
    c = GLA_CHUNK
    r = np.arange(c)[:, None]
    t = np.arange(c)[None, :]
    blocks = []
    for lv in range(GLA_LEVELS):
        half = 1 << lv
        mid = (r // (2 * half)) * (2 * half) + half
        second = r >= mid
        blocks.append(np.where(second, (t >= mid) & (t <= r), (t > r) & (t < mid)))
    blocks.append(t <= r)
    blocks.append(t > r)
    fwd = np.concatenate(blocks, axis=0).astype(np.float32)
    bwd = np.concatenate([b[::-1, ::-1] for b in blocks], axis=0).astype(np.float32)
    s2 = np.stack([np.concatenate([m, m], axis=1) for m in (fwd, bwd)])
    i = np.arange(c)[:, None]
    j = np.arange(c)[None, :]
    x = np.bitwise_xor(i, j)
    lvl = np.where(j > i, GLA_LV_NONE,
                   np.where(i == j, GLA_LV_DIAG, np.floor(np.log2(np.maximum(x, 1))).astype(np.int64)))
    lv_f = np.tile(lvl, (1, GLA_HEADS))
    lv_b = np.tile(lvl[::-1, ::-1], (1, GLA_HEADS))
    return jnp.asarray(s2, BF16), jnp.asarray(np.stack([lv_f, lv_b]), jnp.int32)


def _gla_kernel(*refs, ctx_out):
    if ctx_out:
        pgx_ref, pgc_ref, wg_ref, bg_ref, gn_ref, s2_ref, lv_ref, ox_ref, oc_ref, ofx_ref, ofc_ref, st_ref = refs
    else:
        pgx_ref, pgc_ref, wg_ref, bg_ref, gn_ref, s2_ref, lv_ref, ox_ref, ofx_ref, st_ref = refs
        oc_ref = ofc_ref = None
    c = GLA_CHUNK
    hv = GLA_HEADS * GLA_DV
    row = lax.broadcasted_iota(jnp.int32, (hv, GLA_QK_W), 0)
    lane = lax.broadcasted_iota(jnp.int32, (hv, GLA_QK_W), 1)
    head_qk = (row // GLA_DV) == (lane // GLA_DK)
    head_qk_b = head_qk.astype(F32).astype(BF16)
    row2 = lax.broadcasted_iota(jnp.int32, (hv, hv), 0)
    col2 = lax.broadcasted_iota(jnp.int32, (hv, hv), 1)
    head_v_b = ((row2 // GLA_DV) == (col2 // GLA_DV)).astype(F32).astype(BF16)
    gn = gn_ref[...]

    def rep4(a):
        return jnp.concatenate([a, a, a, a], axis=0)

    def chunk(src_ref, rows, d, w3, bias, with_out):
        blk = src_ref[0, rows, :]
        q = blk[:, 0:128] * (GLA_DK ** -0.5)
        k = blk[:, 128:256]
        v = blk[:, 256:512].astype(BF16)
        zh, zl = _split_bf16(blk[:, 768:896])
        z = _dot(jnp.concatenate([zh, zl, zh], axis=1), w3) + bias
        g = (jnp.minimum(z, 0.0) - jnp.log(1.0 + jnp.exp(-jnp.abs(z)))) * (1.0 / GLA_TAU)
        gh, gl = _split_bf16(g)
        f = jnp.exp(_dot(s2_ref[d], jnp.concatenate([gh, gl], axis=0)))
        st = st_ref[...]
        o = None
        if with_out:
            lv = lv_ref[d]
            att = jnp.where(lv == GLA_LV_DIAG,
                            _dot_nt(q.astype(BF16), rep4(k.astype(BF16)) * head_qk_b), 0.0)
            for level in range(GLA_LEVELS):
                fl = f[level * c:(level + 1) * c]
                s = _dot_nt((q * fl).astype(BF16), rep4((k * fl).astype(BF16)) * head_qk_b)
                att = jnp.where(lv == level, s, att)
            o = _dot(att.astype(BF16), rep4(v) * head_v_b)
            o = o + _dot_nt((q * f[GLA_LEVELS * c:(GLA_LEVELS + 1) * c]).astype(BF16), st.astype(BF16))
        upd = _dot_tn(v, (k * f[(GLA_LEVELS + 1) * c:(GLA_LEVELS + 2) * c]).astype(BF16))
        last = GLA_LEVELS * c + (c - 1 if d == 0 else 0)
        st_ref[...] = f[last:last + 1] * st + jnp.where(head_qk, upd, 0.0)
        return o, blk[:, 512:768]

    def finish(o, og):
        hi, lo = _split_bf16(o * o)
        ms = (_dot(hi, head_v_b) + _dot(lo, head_v_b)) * (1.0 / GLA_DV)
        return (o * lax.rsqrt(ms + RMS_EPS) * gn * _silu(og)).astype(BF16)

    def scan(src_ref, of_ref, out_ref, d, w3, bias):
        n = src_ref.shape[1] // c
        with_out = out_ref is not None

        def body(i, carry):
            ci = i if d == 0 else n - 1 - i
            rows = pl.ds(pl.multiple_of(ci * c, c), c)
            o, og = chunk(src_ref, rows, d, w3, bias, with_out)
            if with_out:
                if d == 0:
                    of_ref[rows, :] = o
                else:
                    out_ref[0, rows, :] = finish(of_ref[rows, :] + o, og)
            return carry

        lax.fori_loop(0, n, body, 0)

    for d in range(2):
        wh, wl = _split_bf16(wg_ref[d])
        w3 = jnp.concatenate([wh, wh, wl], axis=0)
        bias = bg_ref[d]
        st_ref[...] = jnp.zeros(st_ref.shape, F32)
        scan(pgc_ref, ofc_ref, oc_ref, d, w3, bias)
        scan(pgx_ref, ofx_ref, ox_ref, d, w3, bias)


def _gla_call(pgx, pgc, wg, bg, gn, s2, lv, ctx_out):
    b, s, _ = pgx.shape
    n_ctx = pgc.shape[1]
    hv = GLA_WIDTH
    in_specs = [pl.BlockSpec((1, s, PG_W), lambda i: (i, 0, 0)),
                pl.BlockSpec((1, n_ctx, PG_W), lambda i: (i, 0, 0)),
                _const_spec(wg.shape, 1), _const_spec(bg.shape, 1), _const_spec((1, hv), 1),
                _const_spec(s2.shape, 1), _const_spec(lv.shape, 1)]
    out_specs = [pl.BlockSpec((1, s, hv), lambda i: (i, 0, 0))]
    out_shape = [jax.ShapeDtypeStruct((b, s, hv), BF16)]
    scratch = [pltpu.VMEM((s, hv), F32)]
    if ctx_out:
        out_specs.append(pl.BlockSpec((1, n_ctx, hv), lambda i: (i, 0, 0)))
        out_shape.append(jax.ShapeDtypeStruct((b, n_ctx, hv), BF16))
        scratch.append(pltpu.VMEM((n_ctx, hv), F32))
    scratch.append(pltpu.VMEM((hv, GLA_QK_W), F32))
    res = pl.pallas_call(
        functools.partial(_gla_kernel, ctx_out=ctx_out),
        grid=(b,),
        in_specs=in_specs, out_specs=out_specs, out_shape=out_shape, scratch_shapes=scratch,
        compiler_params=_params(1),
        name="gla",
    )(pgx, pgc, wg, bg, gn.reshape(1, hv), s2, lv)
    return (res[0], res[1]) if ctx_out else (res[0], None)


def _swa_kernel(*refs, ctx_out):
    if ctx_out:
        px_ref, pc_ref, sink_ref, ox_ref, oc_ref = refs
    else:
        px_ref, pc_ref, sink_ref, ox_ref = refs
    s_len = px_ref.shape[1]
    n_ctx = pc_ref.shape[1]
    blk = SWA_BLOCK
    band = 3 * blk
    grp = SWA_HEADS // SWA_KV_HEADS
    lane = lax.broadcasted_iota(jnp.int32, (1, LANE), 1)
    low = lane < SWA_HD
    half_mask = [low.astype(F32).astype(BF16), (~low).astype(F32).astype(BF16)]

    def col(i):
        return slice(i * LANE, (i + 1) * LANE)

    def softmax_pv(q, scores_keys, sink_col):
        m = sink_col
        for sc, _ in scores_keys:
            m = jnp.maximum(m, jnp.max(sc, axis=-1, keepdims=True))
        l = jnp.exp2(sink_col - m)
        o = None
        for sc, v in scores_keys:
            p = jnp.exp2(sc - m)
            l = l + jnp.sum(p, axis=-1, keepdims=True)
            pv = _dot(p.astype(BF16), v)
            o = pv if o is None else o + pv
        return o / l

    for kv in range(SWA_KV_HEADS):
        kc = pc_ref[0, :, col(4 + kv)]
        vc = pc_ref[0, :, col(6 + kv)]
        pairs = (2 * kv, 2 * kv + 1)

        def sink_rows(n_rows, half):
            r = lax.broadcasted_iota(jnp.int32, (2 * n_rows, 1), 0)
            ha = grp * kv + half
            return jnp.where(r < n_rows, sink_ref[ha:ha + 1, 0:1], sink_ref[ha + 2:ha + 3, 0:1])

        def body(n, carry):
            r0 = pl.multiple_of(n * blk, blk)
            q = jnp.concatenate([px_ref[0, pl.ds(r0, blk), col(pairs[0])],
                                 px_ref[0, pl.ds(r0, blk), col(pairs[1])]], axis=0)
            start = pl.multiple_of(jnp.clip((n - 1) * blk, 0, s_len - band), blk)
            kb = px_ref[0, pl.ds(start, band), col(4 + kv)]
            vb = px_ref[0, pl.ds(start, band), col(6 + kv)]
            qpos = r0 + lax.broadcasted_iota(jnp.int32, (2 * blk, 1), 0) % blk
            kpos = start + lax.broadcasted_iota(jnp.int32, (1, band), 1)
            valid = jnp.abs(qpos - kpos) <= SWA_WINDOW
            outs = []
            for half in range(2):
                qz = q * half_mask[half]
                sb = jnp.where(valid, _dot_nt(qz, kb), -jnp.inf)
                sc = _dot_nt(qz, kc)
                outs.append(softmax_pv(qz, [(sb, vb), (sc, vc)], sink_rows(blk, half)))
            o2 = jnp.where(low, outs[0], outs[1]).astype(BF16)
            ox_ref[0, pl.ds(r0, blk), col(pairs[0])] = o2[:blk]
            ox_ref[0, pl.ds(r0, blk), col(pairs[1])] = o2[blk:]
            return carry

        lax.fori_loop(0, s_len // blk, body, 0)

        if ctx_out:
            q = jnp.concatenate([pc_ref[0, :, col(pairs[0])], pc_ref[0, :, col(pairs[1])]], axis=0)
            outs = []
            for half in range(2):
                qz = q * half_mask[half]
                outs.append(softmax_pv(qz, [(_dot_nt(qz, kc), vc)], sink_rows(n_ctx, half)))
            o2 = jnp.where(low, outs[0], outs[1]).astype(BF16)
            oc_ref[0, :, col(pairs[0])] = o2[:n_ctx]
            oc_ref[0, :, col(pairs[1])] = o2[n_ctx:]


def _swa_call(psx, psc, sink, ctx_out):
    b, s, _ = psx.shape
    n_ctx = psc.shape[1]
    in_specs = [pl.BlockSpec((1, s, PS_W), lambda i: (i, 0, 0)),
                pl.BlockSpec((1, n_ctx, PS_W), lambda i: (i, 0, 0)),
                _const_spec(sink.shape, 1)]
    out_specs = [pl.BlockSpec((1, s, SWA_WIDTH), lambda i: (i, 0, 0))]
    out_shape = [jax.ShapeDtypeStruct((b, s, SWA_WIDTH), BF16)]
    if ctx_out:
        out_specs.append(pl.BlockSpec((1, n_ctx, SWA_WIDTH), lambda i: (i, 0, 0)))
        out_shape.append(jax.ShapeDtypeStruct((b, n_ctx, SWA_WIDTH), BF16))
    res = pl.pallas_call(
        functools.partial(_swa_kernel, ctx_out=ctx_out),
        grid=(b,),
        in_specs=in_specs, out_specs=out_specs, out_shape=out_shape,
        compiler_params=_params(1),
        name="swa",
    )(psx, psc, sink)
    return (res[0], res[1]) if ctx_out else (res[0], None)


def _diff_kernel(*refs, ctx_out, lambda_init):
    if ctx_out:
        px_ref, pc_ref, lam_ref, gn_ref, ox_ref, oc_ref = refs
    else:
        px_ref, pc_ref, lam_ref, gn_ref, ox_ref = refs
    s_len = px_ref.shape[1]
    w = DIFF_QK_W
    lv = lam_ref[...]
    lam = (jnp.exp(jnp.sum(lv[0:1] * lv[1:2], axis=-1, keepdims=True))
           - jnp.exp(jnp.sum(lv[2:3] * lv[3:4], axis=-1, keepdims=True)) + lambda_init)
    lane = lax.broadcasted_iota(jnp.int32, (1, w), 1)
    row2 = lax.broadcasted_iota(jnp.int32, (w, w), 0)
    col2 = lax.broadcasted_iota(jnp.int32, (w, w), 1)
    head_ones = ((row2 // DIFF_V) == (col2 // DIFF_V)).astype(F32).astype(BF16)
    gn = gn_ref[...] * (1.0 - lambda_init)
    half_masks = [[((lane >= 2 * DIFF_QK * h + DIFF_QK * i) & (lane < 2 * DIFF_QK * h + DIFF_QK * (i + 1))
                    ).astype(F32).astype(BF16) for i in range(2)] for h in range(DIFF_HEADS)]
    head_masks = [(lane >= DIFF_V * h) & (lane < DIFF_V * (h + 1)) for h in range(DIFF_HEADS)]

    def attend(q, key_refs):
        o = jnp.zeros((q.shape[0], w), F32)
        for h in range(DIFF_HEADS):
            probs, inv = [], []
            for i in range(2):
                qm = q * half_masks[h][i]
                scores = [_dot_nt(qm, r[0, :, w:2 * w]) for r in key_refs]
                m = None
                for sc in scores:
                    mx = jnp.max(sc, axis=-1, keepdims=True)
                    m = mx if m is None else jnp.maximum(m, mx)
                ps = [jnp.exp2(sc - m) for sc in scores]
                l = None
                for p in ps:
                    sm = jnp.sum(p, axis=-1, keepdims=True)
                    l = sm if l is None else l + sm
                probs.append(ps)
                inv.append(1.0 / l)
            c1 = inv[0]
            c2 = lam * inv[1]
            oh = None
            for p1, p2, r in zip(probs[0], probs[1], key_refs):
                pv = _dot((p1 * c1 - p2 * c2).astype(BF16), r[0, :, 2 * w:3 * w])
                oh = pv if oh is None else oh + pv
            o = jnp.where(head_masks[h], oh, o)
        hi, lo = _split_bf16(o * o)
        ms = (_dot(hi, head_ones) + _dot(lo, head_ones)) * (1.0 / DIFF_V)
        return (o * lax.rsqrt(ms + RMS_EPS) * gn).astype(BF16)

    def body(n, carry):
        r0 = pl.multiple_of(n * DIFF_QBLOCK, DIFF_QBLOCK)
        q = px_ref[0, pl.ds(r0, DIFF_QBLOCK), 0:w]
        ox_ref[0, pl.ds(r0, DIFF_QBLOCK), :] = attend(q, [px_ref, pc_ref])
        return carry

    lax.fori_loop(0, s_len // DIFF_QBLOCK, body, 0)
    if ctx_out:
        oc_ref[0] = attend(pc_ref[0, :, 0:w], [pc_ref])


def _diff_call(pdx, pdc, lam, gn, lambda_init, ctx_out):
    b, s, _ = pdx.shape
    n_ctx = pdc.shape[1]
    in_specs = [pl.BlockSpec((1, s, PD_W), lambda i: (i, 0, 0)),
                pl.BlockSpec((1, n_ctx, PD_W), lambda i: (i, 0, 0)),
                _const_spec(lam.shape, 1), _const_spec((1, DIFF_WIDTH), 1)]
    out_specs = [pl.BlockSpec((1, s, DIFF_WIDTH), lambda i: (i, 0, 0))]
    out_shape = [jax.ShapeDtypeStruct((b, s, DIFF_WIDTH), BF16)]
    if ctx_out:
        out_specs.append(pl.BlockSpec((1, n_ctx, DIFF_WIDTH), lambda i: (i, 0, 0)))
        out_shape.append(jax.ShapeDtypeStruct((b, n_ctx, DIFF_WIDTH), BF16))
    res = pl.pallas_call(
        functools.partial(_diff_kernel, ctx_out=ctx_out, lambda_init=lambda_init),
        grid=(b,),
        in_specs=in_specs, out_specs=out_specs, out_shape=out_shape,
        compiler_params=_params(1),
        name="diff",
    )(pdx, pdc, lam, gn.reshape(1, DIFF_WIDTH))
    return (res[0], res[1]) if ctx_out else (res[0], None)


def _outproj_kernel(h_ref, mod_ref, yg_ref, ys_ref, yd_ref, w_ref, o_ref):
    a = GLA_WIDTH
    b = GLA_WIDTH + SWA_WIDTH
    acc = (_dot(yg_ref[0], w_ref[0:a, :]) + _dot(ys_ref[0], w_ref[a:b, :]) + _dot(yd_ref[0], w_ref[b:, :]))
    o_ref[0] = h_ref[0] + mod_ref[0][5:6] * acc


def _outproj_call(h, mod, yg, ys, yd, w):
    bx, t, d = h.shape
    tm = min(FFN_TM, t)
    tile = lambda width: pl.BlockSpec((1, tm, width), lambda b, i: (b, i, 0))
    return pl.pallas_call(
        _outproj_kernel,
        grid=(bx, t // tm),
        in_specs=[tile(d), pl.BlockSpec((1, N_MOD, d), lambda b, i: (b, 0, 0)),
                  tile(GLA_WIDTH), tile(SWA_WIDTH), tile(DIFF_WIDTH), _const_spec((MIX_WIDTH, d), 2)],
        out_specs=tile(d),
        out_shape=jax.ShapeDtypeStruct(h.shape, F32),
        compiler_params=_params(2),
        name="outproj",
    )(h, mod, yg, ys, yd, w)


def _prep_w_in(w):
    parts, off = [], 0
    for sz in IN_SIZES:
        parts.append(w[:, off:off + sz])
        off += sz
    gq, gk, gv, gf, gb, og, sq, sk, sv, dq, dk, dv = parts
    pad = jnp.zeros((w.shape[0], PG_W - (2 * GLA_QK_W + 2 * GLA_WIDTH + 2 * GLA_GATE_RANK)), w.dtype)
    k0, k1 = sk[:, :SWA_HD], sk[:, SWA_HD:]
    v0, v1 = sv[:, :SWA_HD], sv[:, SWA_HD:]
    cols = [gq, gk, gv, og, gf, gb, pad, sq, k0, k0, k1, k1, v0, v0, v1, v1, dq, dk, dv]
    return jnp.concatenate(cols, axis=1).astype(BF16)


def _prep_gate(w_gate, b_gate):
    wg = jnp.zeros((2, LANE, GLA_QK_W), F32)
    for d in range(2):
        wg = wg.at[d, GLA_GATE_RANK * d:GLA_GATE_RANK * (d + 1), :].set(w_gate[d])
    return wg, b_gate.reshape(2, 1, GLA_QK_W)


def _axial_angles(rows, head_dim):
    half = head_dim // 2
    row = jnp.repeat(jnp.arange(rows, dtype=F32), GRID_W)
    col = jnp.tile(jnp.arange(GRID_W, dtype=F32), rows)
    inv_freq = 1.0 / (ROPE_BASE ** (jnp.arange(0, half, 2, dtype=F32) / half))

    def axis_angles(pos):
        a = pos[:, None] * inv_freq[None, :]
        return jnp.concatenate([a, a], axis=-1)

    return jnp.concatenate([axis_angles(row), axis_angles(col)], axis=-1)


def _rope_table(seq):
    rows = seq // GRID_W
    blocks = []
    for head_dim, qscale in ((SWA_HD, SWA_QSCALE), (DIFF_QK, DIFF_QSCALE)):
        ang = _axial_angles(rows, head_dim)
        quarter = head_dim // 4
        sign = jnp.where((jnp.arange(head_dim) % (2 * quarter)) < quarter, -1.0, 1.0).astype(F32)
        reps = LANE // head_dim
        cos = jnp.tile(jnp.cos(ang), (1, reps))
        sin = jnp.tile(jnp.sin(ang) * sign[None, :], (1, reps))
        blocks += [cos * qscale, sin * qscale, cos, sin]
    return jnp.concatenate(blocks, axis=1)


def kernel(x, c, ctx, c_ctx, w_mod, b_mod, g_ffn1, w_ffn1_in, w_ffn1_out, g_mix, w_in, w_out, w_gla_gate,
           b_gla_gate, g_gla_norm, swa_sink, diff_lambda, g_diff_norm, g_ffn2, w_ffn2_in, w_ffn2_out, g_final):
    b, s, d = x.shape
    n_ctx = ctx.shape[1]
    depth = w_mod.shape[0]

    cs = jnp.zeros((MOD_ROWS, d), F32).at[:b].set(c).at[b].set(c_ctx)
    mod = _mod_call(cs, w_mod, b_mod).reshape(depth, MOD_ROWS, N_MOD, d)
    tab = _rope_table(s)
    s2, lv = _gla_constants()

    hx = x
    hc = ctx.reshape(1, b * n_ctx, d)
    for l in range(depth):
        ctx_out = l < depth - 1
        last = l == depth - 1
        mod_x = mod[l, :b]
        mod_c = mod[l, b:b + 1]
        lambda_init = 0.8 - 0.6 * math.exp(-0.3 * l)
        w1i, w1o = w_ffn1_in[l].astype(BF16), w_ffn1_out[l].astype(BF16)
        w2i, w2o = w_ffn2_in[l].astype(BF16), w_ffn2_out[l].astype(BF16)
        wi = _prep_w_in(w_in[l])
        wo = w_out[l].astype(BF16)
        wg, bg = _prep_gate(w_gla_gate[l], b_gla_gate[l])
        sink = jnp.broadcast_to((swa_sink[l] * LOG2E)[:, None], (SWA_HEADS, LANE))

        hx = _ffn_call(hx, mod_x, g_ffn1[l], w1i, w1o, 0)
        hc = _ffn_call(hc, mod_c, g_ffn1[l], w1i, w1o, 0)
        pgx, psx, pdx = _inproj_call(hx, mod_x, g_mix[l], wi, tab)
        pgc, psc, pdc = _inproj_call(hc, mod_c, g_mix[l], wi)
        pgc, psc, pdc = (a.reshape(b, n_ctx, a.shape[-1]) for a in (pgc, psc, pdc))

        gla_x, gla_c = _gla_call(pgx, pgc, wg, bg, g_gla_norm[l], s2, lv, ctx_out)
        swa_x, swa_c = _swa_call(psx, psc, sink, ctx_out)
        dif_x, dif_c = _diff_call(pdx, pdc, diff_lambda[l], g_diff_norm[l], lambda_init, ctx_out)

        hx = _outproj_call(hx, mod_x, gla_x, swa_x, dif_x, wo)
        hx = _ffn_call(hx, mod_x, g_ffn2[l], w2i, w2o, 6, g_final if last else None)
        if ctx_out:
            flat = lambda a: a.reshape(1, b * n_ctx, a.shape[-1])
            hc = _outproj_call(hc, mod_c, flat(gla_c), flat(swa_c), flat(dif_c), wo)
            hc = _ffn_call(hc, mod_c, g_ffn2[l], w2i, w2o, 6)
    return hx
```

```python
import functools
import math

import numpy as np
import jax
import jax.numpy as jnp
from jax import lax
from jax.experimental import pallas as pl
from jax.experimental.pallas import tpu as pltpu

F32 = jnp.float32
BF16 = jnp.bfloat16

D_MODEL = 1024
DEPTH = 2
GRID_W = 64
N_MOD = 9
D_FF = 2816
RMS_EPS = 1e-6
ROPE_BASE = 10000.0

GLA_HEADS = 4
GLA_DK = 32
GLA_DV = 64
GLA_GATE_RANK = 16
GLA_TAU = 16.0
SWA_HEADS = 8
SWA_KV_HEADS = 2
SWA_HD = 64
SWA_WINDOW = 128
SWA_BLOCK = 128
DIFF_HEADS = 4
DIFF_QK = 32
DIFF_V = 64
DIFF_QBLOCK = 128

GLA_QK_W = GLA_HEADS * GLA_DK
GLA_WIDTH = GLA_HEADS * GLA_DV
SWA_WIDTH = SWA_HEADS * SWA_HD
SWA_KV_W = SWA_KV_HEADS * SWA_HD
DIFF_QK_W = DIFF_HEADS * 2 * DIFF_QK
DIFF_WIDTH = DIFF_HEADS * DIFF_V
MIX_WIDTH = GLA_WIDTH + SWA_WIDTH + DIFF_WIDTH
IN_SIZES = (GLA_QK_W, GLA_QK_W, GLA_WIDTH, GLA_GATE_RANK, GLA_GATE_RANK, GLA_WIDTH,
            SWA_WIDTH, SWA_KV_W, SWA_KV_W, DIFF_QK_W, DIFF_QK_W, DIFF_WIDTH)

LANE = 128
VMEM_LIMIT = 56 * 1024 * 1024
LOG2E = math.log2(math.e)

PG_W = 896
PS_COLS = 896
PS_W = 768
PD_COLS = 768
PD_W = 512
P_W = PG_W + PS_COLS + PD_COLS
GLA_CHUNK = 64
GLA_LEVELS = 6
MOD_ROWS = 16
MOD_TN = 1152
FFN_TM = 512
FFN_TF = D_FF // 2


def _dot(a, b):
    return jnp.dot(a, b, preferred_element_type=F32)


def _dot_nt(a, b):
    return lax.dot_general(a, b, (((1,), (1,)), ((), ())), preferred_element_type=F32)


def _dot_tn(a, b):
    return lax.dot_general(a, b, (((0,), (0,)), ((), ())), preferred_element_type=F32)


def _split_bf16(x):
    hi = x.astype(BF16)
    lo = (x - hi.astype(F32)).astype(BF16)
    return hi, lo


def _rms(x, g):
    return x * lax.rsqrt(jnp.mean(x * x, axis=-1, keepdims=True) + RMS_EPS) * g


def _silu(x):
    return x * (1.0 / (1.0 + jnp.exp(-x)))


def _params(n_grid):
    return pltpu.CompilerParams(dimension_semantics=("arbitrary",) * n_grid, vmem_limit_bytes=VMEM_LIMIT)


def _const_spec(shape, n_grid, single=False):
    zeros = (0,) * len(shape)
    index_map = {1: lambda a: zeros, 2: lambda a, b: zeros}[n_grid]
    if single:
        return pl.BlockSpec(shape, index_map, pipeline_mode=pl.Buffered(1))
    return pl.BlockSpec(shape, index_map)


def _mod_kernel(a_ref, w_ref, b_ref, o_ref):
    a = _silu(a_ref[...]).astype(BF16)
    o_ref[0] = _dot(a, w_ref[0].astype(BF16)) + b_ref[0]


def _mod_call(cs, w_mod, b_mod):
    n_layers, d, n = w_mod.shape
    return pl.pallas_call(
        _mod_kernel,
        grid=(n_layers, n // MOD_TN),
        in_specs=[pl.BlockSpec((MOD_ROWS, d), lambda l, j: (0, 0)),
                  pl.BlockSpec((1, d, MOD_TN), lambda l, j: (l, 0, j)),
                  pl.BlockSpec((1, 1, MOD_TN), lambda l, j: (l, 0, j))],
        out_specs=pl.BlockSpec((1, MOD_ROWS, MOD_TN), lambda l, j: (l, 0, j)),
        out_shape=jax.ShapeDtypeStruct((n_layers, MOD_ROWS, n), F32),
        compiler_params=_params(2),
        name="mod",
    )(cs, w_mod, b_mod.reshape(n_layers, 1, n))


def _ffn_kernel(*refs, mod_off, final):
    if final:
        h_ref, mod_ref, g_ref, win_ref, wout_ref, gfin_ref, o_ref = refs
    else:
        h_ref, mod_ref, g_ref, win_ref, wout_ref, o_ref = refs
    x = h_ref[0]
    mod = mod_ref[0]
    shift, scale, gate = mod[mod_off:mod_off + 1], mod[mod_off + 1:mod_off + 2], mod[mod_off + 2:mod_off + 3]
    y = (_rms(x, g_ref[...]) * (1.0 + scale) + shift).astype(BF16)
    acc = jnp.zeros(x.shape, F32)
    for c in range(D_FF // FFN_TF):
        gt = _dot(y, win_ref[:, c * FFN_TF:(c + 1) * FFN_TF])
        up = _dot(y, win_ref[:, D_FF + c * FFN_TF:D_FF + (c + 1) * FFN_TF])
        act = (_silu(gt) * up).astype(BF16)
        acc = acc + _dot(act, wout_ref[c * FFN_TF:(c + 1) * FFN_TF, :])
    out = x + (0.5 * gate) * acc
    if final:
        out = _rms(out, gfin_ref[...])
    o_ref[0] = out


def _ffn_call(h, mod, g, w_in, w_out, mod_off, g_final=None):
    bx, t, d = h.shape
    tm = min(FFN_TM, t)
    final = g_final is not None
    in_specs = [pl.BlockSpec((1, tm, d), lambda b, i: (b, i, 0)),
                pl.BlockSpec((1, N_MOD, d), lambda b, i: (b, 0, 0)),
                _const_spec((1, d), 2),
                _const_spec((d, 2 * D_FF), 2, single=True),
                _const_spec((D_FF, d), 2, single=True)]
    args = [h, mod, g.reshape(1, d), w_in, w_out]
    if final:
        in_specs.append(_const_spec((1, d), 2))
        args.append(g_final.reshape(1, d))
    return pl.pallas_call(
        functools.partial(_ffn_kernel, mod_off=mod_off, final=final),
        grid=(bx, t // tm),
        in_specs=in_specs,
        out_specs=pl.BlockSpec((1, tm, d), lambda b, i: (b, i, 0)),
        out_shape=jax.ShapeDtypeStruct(h.shape, F32),
        compiler_params=_params(2),
        name="ffn",
    )(*args)


SWA_QSCALE = SWA_HD ** -0.5 * LOG2E
DIFF_QSCALE = DIFF_QK ** -0.5 * LOG2E


def _rot_half(blk, quarter, first):
    return jnp.where(first, pltpu.roll(blk, LANE - quarter, 1), pltpu.roll(blk, quarter, 1))


def _inproj_kernel(*refs, rope):
    if rope:
        h_ref, mod_ref, g_ref, w_ref, tab_ref, pg_ref, ps_ref, pd_ref, kts_ref, kt_ref = refs
    else:
        h_ref, mod_ref, g_ref, w_ref, pg_ref, ps_ref, pd_ref, kts_ref, kt_ref = refs
    x = h_ref[0]
    mod = mod_ref[0]
    y = (_rms(x, g_ref[...]) * (1.0 + mod[4:5]) + mod[3:4]).astype(BF16)
    p = _dot(y, w_ref[...])
    pg_ref[0] = p[:, :PG_W]
    swa = [p[:, PG_W + i * LANE:PG_W + (i + 1) * LANE] for i in range(PS_COLS // LANE)]
    dif = [p[:, PG_W + PS_COLS + i * LANE:PG_W + PS_COLS + (i + 1) * LANE] for i in range(PD_COLS // LANE)]
    if rope:
        tab = [tab_ref[:, i * LANE:(i + 1) * LANE] for i in range(8)]
        lane = lax.broadcasted_iota(jnp.int32, (1, LANE), 1)
        first_s = (lane % (SWA_HD // 2)) < (SWA_HD // 4)
        first_d = (lane % (DIFF_QK // 2)) < (DIFF_QK // 4)
        for i in range(5):
            c, s = (tab[0], tab[1]) if i < 4 else (tab[2], tab[3])
            swa[i] = swa[i] * c + _rot_half(swa[i], SWA_HD // 4, first_s) * s
        for i in range(4):
            c, s = (tab[4], tab[5]) if i < 2 else (tab[6], tab[7])
            dif[i] = dif[i] * c + _rot_half(dif[i], DIFF_QK // 4, first_d) * s
    else:
        for i in range(4):
            swa[i] = swa[i] * SWA_QSCALE
        for i in range(2):
            dif[i] = dif[i] * DIFF_QSCALE
    ps_ref[0] = jnp.concatenate(swa[0:4] + swa[5:7], axis=1).astype(BF16)
    kt = swa[4].T
    k0, k1 = kt[:SWA_HD], kt[SWA_HD:]
    kts_ref[0] = jnp.concatenate([k0, k0, k1, k1], axis=0).astype(BF16)
    pd_ref[0] = jnp.concatenate(dif[0:2] + dif[4:6], axis=1).astype(BF16)
    kt_ref[0] = jnp.concatenate(dif[2:4], axis=1).T.astype(BF16)


def _inproj_call(h, mod, g, w, tab=None):
    bx, t, d = h.shape
    tm = min(FFN_TM, t)
    rope = tab is not None
    in_specs = [pl.BlockSpec((1, tm, d), lambda j, b: (b, j, 0)),
                pl.BlockSpec((1, N_MOD, d), lambda j, b: (b, 0, 0)),
                _const_spec((1, d), 2),
                _const_spec((d, P_W), 2, single=True)]
    args = [h, mod, g.reshape(1, d), w]
    if rope:
        in_specs.append(pl.BlockSpec((tm, 8 * LANE), lambda j, b: (j, 0)))
        args.append(tab)
    out_map = lambda j, b: (b, j, 0)
    return pl.pallas_call(
        functools.partial(_inproj_kernel, rope=rope),
        grid=(t // tm, bx),
        in_specs=in_specs,
        out_specs=[pl.BlockSpec((1, tm, PG_W), out_map), pl.BlockSpec((1, tm, PS_W), out_map),
                   pl.BlockSpec((1, tm, PD_W), out_map),
                   pl.BlockSpec((1, 2 * LANE, tm), lambda j, b: (b, 0, j)),
                   pl.BlockSpec((1, DIFF_QK_W, tm), lambda j, b: (b, 0, j))],
        out_shape=[jax.ShapeDtypeStruct((bx, t, PG_W), F32), jax.ShapeDtypeStruct((bx, t, PS_W), BF16),
                   jax.ShapeDtypeStruct((bx, t, PD_W), BF16),
                   jax.ShapeDtypeStruct((bx, 2 * LANE, t), BF16),
                   jax.ShapeDtypeStruct((bx, DIFF_QK_W, t), BF16)],
        compiler_params=_params(2),
        name="inproj",
    )(*args)


GLA_SROWS = (GLA_LEVELS + 2) * GLA_CHUNK
GLA_LV_DIAG = GLA_LEVELS
GLA_LV_NONE = GLA_LEVELS + 1


def _gla_constants():
    c = GLA_CHUNK
    r = np.arange(c)[:, None]
    t = np.arange(c)[None, :]
    blocks = []
    for lv in range(GLA_LEVELS):
        half = 1 << lv
        mid = (r // (2 * half)) * (2 * half) + half
        second = r >= mid
        blocks.append(np.where(second, (t >= mid) & (t <= r), (t > r) & (t < mid)))
    blocks.append(t <= r)
    blocks.append(t > r)
    fwd = np.concatenate(blocks, axis=0).astype(np.float32)
    bwd = np.concatenate([b[::-1, ::-1] for b in blocks], axis=0).astype(np.float32)
    s2 = np.stack([np.concatenate([m, m], axis=1) for m in (fwd, bwd)])
    i = np.arange(c)[:, None]
    j = np.arange(c)[None, :]
    x = np.bitwise_xor(i, j)
    lvl = np.where(j > i, GLA_LV_NONE,
                   np.where(i == j, GLA_LV_DIAG, np.floor(np.log2(np.maximum(x, 1))).astype(np.int64)))
    lv_f = np.tile(lvl, (1, GLA_HEADS))
    lv_b = np.tile(lvl[::-1, ::-1], (1, GLA_HEADS))
    return jnp.asarray(s2, BF16), jnp.asarray(np.stack([lv_f, lv_b]), jnp.int32)


def _gla_kernel(*refs, ctx_out):
    if ctx_out:
        pgx_ref, pgc_ref, wg_ref, bg_ref, gn_ref, s2_ref, lv_ref, ox_ref, oc_ref, ofx_ref, ofc_ref, st_ref = refs
    else:
        pgx_ref, pgc_ref, wg_ref, bg_ref, gn_ref, s2_ref, lv_ref, ox_ref, ofx_ref, st_ref = refs
        oc_ref = ofc_ref = None
    c = GLA_CHUNK
    hv = GLA_HEADS * GLA_DV
    row = lax.broadcasted_iota(jnp.int32, (hv, GLA_QK_W), 0)
    lane = lax.broadcasted_iota(jnp.int32, (hv, GLA_QK_W), 1)
    head_qk = (row // GLA_DV) == (lane // GLA_DK)
    head_qk_b = head_qk.astype(F32).astype(BF16)
    row2 = lax.broadcasted_iota(jnp.int32, (hv, hv), 0)
    col2 = lax.broadcasted_iota(jnp.int32, (hv, hv), 1)
    head_v_b = ((row2 // GLA_DV) == (col2 // GLA_DV)).astype(F32).astype(BF16)
    gn = gn_ref[...]

    def rep4(a):
        return jnp.concatenate([a, a, a, a], axis=0)

    def chunk(src_ref, rows, d, w3, bias, with_out):
        blk = src_ref[0, rows, :]
        q = blk[:, 0:128] * (GLA_DK ** -0.5)
        k = blk[:, 128:256]
        v = blk[:, 256:512].astype(BF16)
        zh, zl = _split_bf16(blk[:, 768:896])
        z = _dot(jnp.concatenate([zh, zl, zh], axis=1), w3) + bias
        g = (jnp.minimum(z, 0.0) - jnp.log(1.0 + jnp.exp(-jnp.abs(z)))) * (1.0 / GLA_TAU)
        gh, gl = _split_bf16(g)
        f = jnp.exp(_dot(s2_ref[d], jnp.concatenate([gh, gl], axis=0)))
        st = st_ref[...]
        o = None
        if with_out:
            lv = lv_ref[d]
            att = jnp.where(lv == GLA_LV_DIAG,
                            _dot_nt(q.astype(BF16), rep4(k.astype(BF16)) * head_qk_b), 0.0)
            for level in range(GLA_LEVELS):
                fl = f[level * c:(level + 1) * c]
                s = _dot_nt((q * fl).astype(BF16), rep4((k * fl).astype(BF16)) * head_qk_b)
                att = jnp.where(lv == level, s, att)
            o = _dot(att.astype(BF16), rep4(v) * head_v_b)
            o = o + _dot_nt((q * f[GLA_LEVELS * c:(GLA_LEVELS + 1) * c]).astype(BF16), st.astype(BF16))
        upd = _dot_tn(v, (k * f[(GLA_LEVELS + 1) * c:(GLA_LEVELS + 2) * c]).astype(BF16))
        last = GLA_LEVELS * c + (c - 1 if d == 0 else 0)
        st_ref[...] = f[last:last + 1] * st + jnp.where(head_qk, upd, 0.0)
        return o, blk[:, 512:768]

    def finish(o, og):
        hi, lo = _split_bf16(o * o)
        ms = (_dot(hi, head_v_b) + _dot(lo, head_v_b)) * (1.0 / GLA_DV)
        return (o * lax.rsqrt(ms + RMS_EPS) * gn * _silu(og)).astype(BF16)

    def scan(src_ref, of_ref, out_ref, d, w3, bias):
        n = src_ref.shape[1] // c
        with_out = out_ref is not None

        def body(i, carry):
            ci = i if d == 0 else n - 1 - i
            rows = pl.ds(pl.multiple_of(ci * c, c), c)
            o, og = chunk(src_ref, rows, d, w3, bias, with_out)
            if with_out:
                if d == 0:
                    of_ref[rows, :] = o
                else:
                    out_ref[0, rows, :] = finish(of_ref[rows, :] + o, og)
            return carry

        lax.fori_loop(0, n, body, 0)

    for d in range(2):
        wh, wl = _split_bf16(wg_ref[d])
        w3 = jnp.concatenate([wh, wh, wl], axis=0)
        bias = bg_ref[d]
        st_ref[...] = jnp.zeros(st_ref.shape, F32)
        scan(pgc_ref, ofc_ref, oc_ref, d, w3, bias)
        scan(pgx_ref, ofx_ref, ox_ref, d, w3, bias)


def _gla_call(pgx, pgc, wg, bg, gn, s2, lv, ctx_out):
    b, s, _ = pgx.shape
    n_ctx = pgc.shape[1]
    hv = GLA_WIDTH
    in_specs = [pl.BlockSpec((1, s, PG_W), lambda i: (i, 0, 0)),
                pl.BlockSpec((1, n_ctx, PG_W), lambda i: (i, 0, 0)),
                _const_spec(wg.shape, 1), _const_spec(bg.shape, 1), _const_spec((1, hv), 1),
                _const_spec(s2.shape, 1), _const_spec(lv.shape, 1)]
    out_specs = [pl.BlockSpec((1, s, hv), lambda i: (i, 0, 0))]
    out_shape = [jax.ShapeDtypeStruct((b, s, hv), BF16)]
    scratch = [pltpu.VMEM((s, hv), F32)]
    if ctx_out:
        out_specs.append(pl.BlockSpec((1, n_ctx, hv), lambda i: (i, 0, 0)))
        out_shape.append(jax.ShapeDtypeStruct((b, n_ctx, hv), BF16))
        scratch.append(pltpu.VMEM((n_ctx, hv), F32))
    scratch.append(pltpu.VMEM((hv, GLA_QK_W), F32))
    res = pl.pallas_call(
        functools.partial(_gla_kernel, ctx_out=ctx_out),
        grid=(b,),
        in_specs=in_specs, out_specs=out_specs, out_shape=out_shape, scratch_shapes=scratch,
        compiler_params=_params(1),
        name="gla",
    )(pgx, pgc, wg, bg, gn.reshape(1, hv), s2, lv)
    return (res[0], res[1]) if ctx_out else (res[0], None)


def _swa_kernel(*refs, ctx_out):
    if ctx_out:
        sink_ref, px_ref, ktx_ref, pc_ref, ktc_ref, ox_ref, oc_ref = refs
    else:
        sink_ref, px_ref, ktx_ref, pc_ref, ktc_ref, ox_ref = refs
    s_len = px_ref.shape[1]
    n_ctx = pc_ref.shape[1]
    blk = SWA_BLOCK
    band = 3 * blk
    grp = SWA_HEADS // SWA_KV_HEADS
    lane = lax.broadcasted_iota(jnp.int32, (1, LANE), 1)
    low = lane < SWA_HD
    half_mask = [low.astype(F32).astype(BF16), (~low).astype(F32).astype(BF16)]
    rel = (lax.broadcasted_iota(jnp.int32, (blk, band), 0)
           - lax.broadcasted_iota(jnp.int32, (blk, band), 1))

    def col(i):
        return slice(i * LANE, (i + 1) * LANE)

    def krows(kv):
        return slice(kv * LANE, (kv + 1) * LANE)

    def sink_col(n_rows, kv, half):
        ha = grp * kv + half
        return jnp.concatenate([jnp.full((n_rows, 1), sink_ref[ha], F32),
                                jnp.full((n_rows, 1), sink_ref[ha + 2], F32)], axis=0)

    def softmax_pv(sc, v, snk):
        m = jnp.maximum(snk, jnp.max(sc, axis=-1, keepdims=True))
        p = jnp.exp2(sc - m).astype(BF16)
        r = _dot(p, jnp.concatenate([v, jnp.ones(v.shape, BF16)], axis=1))
        return r[:, :LANE] / (r[:, LANE:] + jnp.exp2(snk - m))

    def pipeline(tasks):
        outs = []
        nxt = tasks[0][0]()
        for t, (_, finish) in enumerate(tasks):
            cur = nxt
            if t + 1 < len(tasks):
                nxt = tasks[t + 1][0]()
            outs.append(finish(cur))
        return outs

    units = [(kv, half) for kv in range(SWA_KV_HEADS) for half in range(2)]
    blocks_per_step = 2

    def body(i, carry):
        tasks, rows = [], []
        for sub in range(blocks_per_step):
            n = i * blocks_per_step + sub
            r0 = pl.multiple_of(n * blk, blk)
            start = pl.multiple_of(jnp.clip((n - 1) * blk, 0, s_len - band), blk)
            bias = jnp.where(jnp.abs(rel + (r0 - start)) <= SWA_WINDOW, 0.0, -jnp.inf)
            bias = jnp.concatenate([bias, jnp.zeros((blk, n_ctx), F32)], axis=1)
            bias = jnp.concatenate([bias, bias], axis=0)
            rows.append(r0)
            for kv, half in units:
                def scores(kv=kv, half=half, r0=r0, start=start, bias=bias):
                    q = jnp.concatenate([px_ref[0, pl.ds(r0, blk), col(2 * kv)],
                                         px_ref[0, pl.ds(r0, blk), col(2 * kv + 1)]], axis=0) * half_mask[half]
                    kt = jnp.concatenate([ktx_ref[0, krows(kv), pl.ds(start, band)],
                                          ktc_ref[0, krows(kv), :]], axis=1)
                    return _dot(q, kt) + bias

                def finish(cur, kv=kv, half=half, start=start):
                    v = jnp.concatenate([px_ref[0, pl.ds(start, band), col(4 + kv)],
                                         pc_ref[0, :, col(4 + kv)]], axis=0)
                    return softmax_pv(cur, v, sink_col(blk, kv, half))

                tasks.append((scores, finish))
        outs = pipeline(tasks)
        for sub in range(blocks_per_step):
            for kv in range(SWA_KV_HEADS):
                t = (sub * SWA_KV_HEADS + kv) * 2
                o2 = jnp.where(low, outs[t], outs[t + 1]).astype(BF16)
                ox_ref[0, pl.ds(rows[sub], blk), col(2 * kv)] = o2[:blk]
                ox_ref[0, pl.ds(rows[sub], blk), col(2 * kv + 1)] = o2[blk:]
        return carry

    lax.fori_loop(0, s_len // (blk * blocks_per_step), body, 0)

    if ctx_out:
        tasks = []
        for kv, half in units:
            def scores(kv=kv, half=half):
                q = jnp.concatenate([pc_ref[0, :, col(2 * kv)], pc_ref[0, :, col(2 * kv + 1)]], axis=0)
                return _dot(q * half_mask[half], ktc_ref[0, krows(kv), :])

            def finish(cur, kv=kv, half=half):
                return softmax_pv(cur, pc_ref[0, :, col(4 + kv)], sink_col(n_ctx, kv, half))

            tasks.append((scores, finish))
        outs = pipeline(tasks)
        for kv in range(SWA_KV_HEADS):
            o2 = jnp.where(low, outs[2 * kv], outs[2 * kv + 1]).astype(BF16)
            oc_ref[0, :, col(2 * kv)] = o2[:n_ctx]
            oc_ref[0, :, col(2 * kv + 1)] = o2[n_ctx:]


def _swa_call(psx, ktx, psc, ktc, sink, ctx_out):
    b, s, _ = psx.shape
    n_ctx = psc.shape[1]
    in_specs = [pl.BlockSpec(memory_space=pltpu.SMEM),
                pl.BlockSpec((1, s, PS_W), lambda i: (i, 0, 0)),
                pl.BlockSpec((1, 2 * LANE, s), lambda i: (i, 0, 0)),
                pl.BlockSpec((1, n_ctx, PS_W), lambda i: (i, 0, 0)),
                pl.BlockSpec((1, 2 * LANE, n_ctx), lambda i: (0, 0, i))]
    out_specs = [pl.BlockSpec((1, s, SWA_WIDTH), lambda i: (i, 0, 0))]
    out_shape = [jax.ShapeDtypeStruct((b, s, SWA_WIDTH), BF16)]
    if ctx_out:
        out_specs.append(pl.BlockSpec((1, n_ctx, SWA_WIDTH), lambda i: (i, 0, 0)))
        out_shape.append(jax.ShapeDtypeStruct((b, n_ctx, SWA_WIDTH), BF16))
    res = pl.pallas_call(
        functools.partial(_swa_kernel, ctx_out=ctx_out),
        grid=(b,),
        in_specs=in_specs, out_specs=out_specs, out_shape=out_shape,
        compiler_params=_params(1),
        name="swa",
    )(sink, psx, ktx, psc, ktc)
    return (res[0], res[1]) if ctx_out else (res[0], None)


def _diff_kernel(*refs, ctx_out, lambda_init):
    if ctx_out:
        px_ref, ktx_ref, pc_ref, ktc_ref, lam_ref, gn_ref, ox_ref, oc_ref, kt_ref, va_ref = refs
    else:
        px_ref, ktx_ref, pc_ref, ktc_ref, lam_ref, gn_ref, ox_ref, kt_ref, va_ref = refs
    s_len = px_ref.shape[1]
    w = DIFF_QK_W
    lv = lam_ref[...]
    lam = (jnp.exp(jnp.sum(lv[0:1] * lv[1:2], axis=-1, keepdims=True))
           - jnp.exp(jnp.sum(lv[2:3] * lv[3:4], axis=-1, keepdims=True)) + lambda_init)
    lane = lax.broadcasted_iota(jnp.int32, (1, w), 1)
    row2 = lax.broadcasted_iota(jnp.int32, (w, w), 0)
    col2 = lax.broadcasted_iota(jnp.int32, (w, w), 1)
    head_ones = ((row2 // DIFF_V) == (col2 // DIFF_V)).astype(F32).astype(BF16)
    gn = gn_ref[...] * (1.0 - lambda_init)
    unit_masks = [((lane >= DIFF_QK * u) & (lane < DIFF_QK * (u + 1))).astype(F32).astype(BF16)
                  for u in range(2 * DIFF_HEADS)]
    head_masks = [(lane >= DIFF_V * h) & (lane < DIFF_V * (h + 1)) for h in range(DIFF_HEADS)]

    kt_ref[:, :s_len] = ktx_ref[0]
    kt_ref[:, s_len:] = ktc_ref[0]
    for h in range(DIFF_HEADS):
        keep = head_masks[h].astype(F32).astype(BF16)
        va_ref[h, :s_len, :] = px_ref[0, :, w:2 * w] * keep + (1.0 - keep)
        va_ref[h, s_len:, :] = pc_ref[0, :, w:2 * w] * keep + (1.0 - keep)

    def attend(q, k0):
        def scores(u):
            return _dot(q * unit_masks[u], kt_ref[:, k0:])

        def softmax_pv(sc, u):
            m = jnp.max(sc, axis=-1, keepdims=True)
            return _dot(jnp.exp2(sc - m).astype(BF16), va_ref[u // 2, k0:, :])

        units = []
        nxt = scores(0)
        for u in range(2 * DIFF_HEADS):
            cur = nxt
            if u + 1 < 2 * DIFF_HEADS:
                nxt = scores(u + 1)
            units.append(softmax_pv(cur, u))
        o = jnp.zeros((q.shape[0], w), F32)
        for h in range(DIFF_HEADS):
            r1, r2 = units[2 * h], units[2 * h + 1]
            oh = r1 / pltpu.roll(r1, w // 2, 1) - lam * (r2 / pltpu.roll(r2, w // 2, 1))
            o = jnp.where(head_masks[h], oh, o)
        hi, lo = _split_bf16(o * o)
        ms = (_dot(hi, head_ones) + _dot(lo, head_ones)) * (1.0 / DIFF_V)
        return (o * lax.rsqrt(ms + RMS_EPS) * gn).astype(BF16)

    def body(n, carry):
        r0 = pl.multiple_of(n * DIFF_QBLOCK, DIFF_QBLOCK)
        q = px_ref[0, pl.ds(r0, DIFF_QBLOCK), 0:w]
        ox_ref[0, pl.ds(r0, DIFF_QBLOCK), :] = attend(q, 0)
        return carry

    lax.fori_loop(0, s_len // DIFF_QBLOCK, body, 0)
    if ctx_out:
        oc_ref[0] = attend(pc_ref[0, :, 0:w], s_len)


def _diff_call(pdx, ktx, pdc, ktc, lam, gn, lambda_init, ctx_out):
    b, s, _ = pdx.shape
    n_ctx = pdc.shape[1]
    in_specs = [pl.BlockSpec((1, s, PD_W), lambda i: (i, 0, 0)),
                pl.BlockSpec((1, DIFF_QK_W, s), lambda i: (i, 0, 0)),
                pl.BlockSpec((1, n_ctx, PD_W), lambda i: (i, 0, 0)),
                pl.BlockSpec((1, DIFF_QK_W, n_ctx), lambda i: (0, 0, i)),
                _const_spec(lam.shape, 1), _const_spec((1, DIFF_WIDTH), 1)]
    out_specs = [pl.BlockSpec((1, s, DIFF_WIDTH), lambda i: (i, 0, 0))]
    out_shape = [jax.ShapeDtypeStruct((b, s, DIFF_WIDTH), BF16)]
    if ctx_out:
        out_specs.append(pl.BlockSpec((1, n_ctx, DIFF_WIDTH), lambda i: (i, 0, 0)))
        out_shape.append(jax.ShapeDtypeStruct((b, n_ctx, DIFF_WIDTH), BF16))
    res = pl.pallas_call(
        functools.partial(_diff_kernel, ctx_out=ctx_out, lambda_init=lambda_init),
        grid=(b,),
        in_specs=in_specs, out_specs=out_specs, out_shape=out_shape,
        scratch_shapes=[pltpu.VMEM((DIFF_QK_W, s + n_ctx), BF16),
                        pltpu.VMEM((DIFF_HEADS, s + n_ctx, DIFF_WIDTH), BF16)],
        compiler_params=_params(1),
        name="diff",
    )(pdx, ktx, pdc, ktc, lam, gn.reshape(1, DIFF_WIDTH))
    return (res[0], res[1]) if ctx_out else (res[0], None)


def _outproj_kernel(h_ref, mod_ref, yg_ref, ys_ref, yd_ref, w_ref, o_ref):
    a = GLA_WIDTH
    b = GLA_WIDTH + SWA_WIDTH
    acc = (_dot(yg_ref[0], w_ref[0:a, :]) + _dot(ys_ref[0], w_ref[a:b, :]) + _dot(yd_ref[0], w_ref[b:, :]))
    o_ref[0] = h_ref[0] + mod_ref[0][5:6] * acc


def _outproj_call(h, mod, yg, ys, yd, w):
    bx, t, d = h.shape
    tm = min(FFN_TM, t)
    tile = lambda width: pl.BlockSpec((1, tm, width), lambda b, i: (b, i, 0))
    return pl.pallas_call(
        _outproj_kernel,
        grid=(bx, t // tm),
        in_specs=[tile(d), pl.BlockSpec((1, N_MOD, d), lambda b, i: (b, 0, 0)),
                  tile(GLA_WIDTH), tile(SWA_WIDTH), tile(DIFF_WIDTH), _const_spec((MIX_WIDTH, d), 2)],
        out_specs=tile(d),
        out_shape=jax.ShapeDtypeStruct(h.shape, F32),
        compiler_params=_params(2),
        name="outproj",
    )(h, mod, yg, ys, yd, w)


def _prep_w_in(w):
    parts, off = [], 0
    for sz in IN_SIZES:
        parts.append(w[:, off:off + sz])
        off += sz
    gq, gk, gv, gf, gb, og, sq, sk, sv, dq, dk, dv = parts
    pad = jnp.zeros((w.shape[0], PG_W - (2 * GLA_QK_W + 2 * GLA_WIDTH + 2 * GLA_GATE_RANK)), w.dtype)
    v0, v1 = sv[:, :SWA_HD], sv[:, SWA_HD:]
    cols = [gq, gk, gv, og, gf, gb, pad, sq, sk, v0, v0, v1, v1, dq, dk, dv]
    return jnp.concatenate(cols, axis=1).astype(BF16)


def _prep_gate(w_gate, b_gate):
    wg = jnp.zeros((2, LANE, GLA_QK_W), F32)
    for d in range(2):
        wg = wg.at[d, GLA_GATE_RANK * d:GLA_GATE_RANK * (d + 1), :].set(w_gate[d])
    return wg, b_gate.reshape(2, 1, GLA_QK_W)


def _axial_angles(rows, head_dim):
    half = head_dim // 2
    row = jnp.repeat(jnp.arange(rows, dtype=F32), GRID_W)
    col = jnp.tile(jnp.arange(GRID_W, dtype=F32), rows)
    inv_freq = 1.0 / (ROPE_BASE ** (jnp.arange(0, half, 2, dtype=F32) / half))

    def axis_angles(pos):
        a = pos[:, None] * inv_freq[None, :]
        return jnp.concatenate([a, a], axis=-1)

    return jnp.concatenate([axis_angles(row), axis_angles(col)], axis=-1)


def _rope_table(seq):
    rows = seq // GRID_W
    blocks = []
    for head_dim, qscale in ((SWA_HD, SWA_QSCALE), (DIFF_QK, DIFF_QSCALE)):
        ang = _axial_angles(rows, head_dim)
        quarter = head_dim // 4
        sign = jnp.where((jnp.arange(head_dim) % (2 * quarter)) < quarter, -1.0, 1.0).astype(F32)
        reps = LANE // head_dim
        cos = jnp.tile(jnp.cos(ang), (1, reps))
        sin = jnp.tile(jnp.sin(ang) * sign[None, :], (1, reps))
        blocks += [cos * qscale, sin * qscale, cos, sin]
    return jnp.concatenate(blocks, axis=1)


def kernel(x, c, ctx, c_ctx, w_mod, b_mod, g_ffn1, w_ffn1_in, w_ffn1_out, g_mix, w_in, w_out, w_gla_gate,
           b_gla_gate, g_gla_norm, swa_sink, diff_lambda, g_diff_norm, g_ffn2, w_ffn2_in, w_ffn2_out, g_final):
    b, s, d = x.shape
    n_ctx = ctx.shape[1]
    depth = w_mod.shape[0]

    cs = jnp.zeros((MOD_ROWS, d), F32).at[:b].set(c).at[b].set(c_ctx)
    mod = _mod_call(cs, w_mod, b_mod).reshape(depth, MOD_ROWS, N_MOD, d)
    tab = _rope_table(s)
    s2, lv = _gla_constants()

    hx = x
    hc = ctx.reshape(1, b * n_ctx, d)
    for l in range(depth):
        ctx_out = l < depth - 1
        last = l == depth - 1
        mod_x = mod[l, :b]
        mod_c = mod[l, b:b + 1]
        lambda_init = 0.8 - 0.6 * math.exp(-0.3 * l)
        w1i, w1o = w_ffn1_in[l].astype(BF16), w_ffn1_out[l].astype(BF16)
        w2i, w2o = w_ffn2_in[l].astype(BF16), w_ffn2_out[l].astype(BF16)
        wi = _prep_w_in(w_in[l])
        wo = w_out[l].astype(BF16)
        wg, bg = _prep_gate(w_gla_gate[l], b_gla_gate[l])
        sink = swa_sink[l] * LOG2E

        hx = _ffn_call(hx, mod_x, g_ffn1[l], w1i, w1o, 0)
        hc = _ffn_call(hc, mod_c, g_ffn1[l], w1i, w1o, 0)
        pgx, psx, pdx, ksx, ktx = _inproj_call(hx, mod_x, g_mix[l], wi, tab)
        pgc, psc, pdc, ksc, ktc = _inproj_call(hc, mod_c, g_mix[l], wi)
        pgc, psc, pdc = (a.reshape(b, n_ctx, a.shape[-1]) for a in (pgc, psc, pdc))

        gla_x, gla_c = _gla_call(pgx, pgc, wg, bg, g_gla_norm[l], s2, lv, ctx_out)
        swa_x, swa_c = _swa_call(psx, ksx, psc, ksc, sink, ctx_out)
        dif_x, dif_c = _diff_call(pdx, ktx, pdc, ktc, diff_lambda[l], g_diff_norm[l], lambda_init, ctx_out)

        hx = _outproj_call(hx, mod_x, gla_x, swa_x, dif_x, wo)
        hx = _ffn_call(hx, mod_x, g_ffn2[l], w2i, w2o, 6, g_final if last else None)
        if ctx_out:
            flat = lambda a: a.reshape(1, b * n_ctx, a.shape[-1])
            hc = _outproj_call(hc, mod_c, flat(gla_c), flat(swa_c), flat(dif_c), wo)
            hc = _ffn_call(hc, mod_c, g_ffn2[l], w2i, w2o, 6)
    return hx
```

```python
import functools
import math

import numpy as np
import jax
import jax.numpy as jnp
from jax import lax
from jax.experimental import pallas as pl
from jax.experimental.pallas import tpu as pltpu

F32 = jnp.float32
BF16 = jnp.bfloat16

D_MODEL = 1024
DEPTH = 2
GRID_W = 64
N_MOD = 9
D_FF = 2816
RMS_EPS = 1e-6
ROPE_BASE = 10000.0

GLA_HEADS = 4
GLA_DK = 32
GLA_DV = 64
GLA_GATE_RANK = 16
GLA_TAU = 16.0
SWA_HEADS = 8
SWA_KV_HEADS = 2
SWA_HD = 64
SWA_WINDOW = 128
SWA_BLOCK = 128
DIFF_HEADS = 4
DIFF_QK = 32
DIFF_V = 64
DIFF_QBLOCK = 128

GLA_QK_W = GLA_HEADS * GLA_DK
GLA_WIDTH = GLA_HEADS * GLA_DV
SWA_WIDTH = SWA_HEADS * SWA_HD
SWA_KV_W = SWA_KV_HEADS * SWA_HD
DIFF_QK_W = DIFF_HEADS * 2 * DIFF_QK
DIFF_WIDTH = DIFF_HEADS * DIFF_V
MIX_WIDTH = GLA_WIDTH + SWA_WIDTH + DIFF_WIDTH
IN_SIZES = (GLA_QK_W, GLA_QK_W, GLA_WIDTH, GLA_GATE_RANK, GLA_GATE_RANK, GLA_WIDTH,
            SWA_WIDTH, SWA_KV_W, SWA_KV_W, DIFF_QK_W, DIFF_QK_W, DIFF_WIDTH)

LANE = 128
VMEM_LIMIT = 56 * 1024 * 1024
LOG2E = math.log2(math.e)

PG_W = 896
PS_COLS = 896
PS_W = 768
PD_COLS = 768
PD_W = 512
P_W = PG_W + PS_COLS + PD_COLS
GLA_CHUNK = 64
GLA_LEVELS = 6
GLA_GROUP = 4
MOD_ROWS = 16
MOD_TN = 1152
FFN_TM = 512
FFN_TF = D_FF // 2


def _dot(a, b):
    return jnp.dot(a, b, preferred_element_type=F32)


def _dot_nt(a, b):
    return lax.dot_general(a, b, (((1,), (1,)), ((), ())), preferred_element_type=F32)


def _dot_tn(a, b):
    return lax.dot_general(a, b, (((0,), (0,)), ((), ())), preferred_element_type=F32)


def _split_bf16(x):
    hi = x.astype(BF16)
    lo = (x - hi.astype(F32)).astype(BF16)
    return hi, lo


def _rms(x, g):
    return x * lax.rsqrt(jnp.mean(x * x, axis=-1, keepdims=True) + RMS_EPS) * g


def _silu(x):
    return x * (1.0 / (1.0 + jnp.exp(-x)))


def _params(n_grid):
    return pltpu.CompilerParams(dimension_semantics=("arbitrary",) * n_grid, vmem_limit_bytes=VMEM_LIMIT)


def _const_spec(shape, n_grid, single=False):
    zeros = (0,) * len(shape)
    index_map = {1: lambda a: zeros, 2: lambda a, b: zeros}[n_grid]
    if single:
        return pl.BlockSpec(shape, index_map, pipeline_mode=pl.Buffered(1))
    return pl.BlockSpec(shape, index_map)


def _layer_spec(shape, layer):
    return pl.BlockSpec((None,) + tuple(shape), lambda b, i: (layer, 0, 0), pipeline_mode=pl.Buffered(1))


def _mod_kernel(a_ref, w_ref, b_ref, o_ref):
    a = _silu(a_ref[...]).astype(BF16)
    o_ref[0] = _dot(a, w_ref[0].astype(BF16)) + b_ref[0]


def _mod_call(cs, w_mod, b_mod):
    n_layers, d, n = w_mod.shape
    return pl.pallas_call(
        _mod_kernel,
        grid=(n_layers, n // MOD_TN),
        in_specs=[pl.BlockSpec((MOD_ROWS, d), lambda l, j: (0, 0)),
                  pl.BlockSpec((1, d, MOD_TN), lambda l, j: (l, 0, j)),
                  pl.BlockSpec((1, 1, MOD_TN), lambda l, j: (l, 0, j))],
        out_specs=pl.BlockSpec((1, MOD_ROWS, MOD_TN), lambda l, j: (l, 0, j)),
        out_shape=jax.ShapeDtypeStruct((n_layers, MOD_ROWS, n), F32),
        compiler_params=_params(2),
        name="mod",
    )(cs, w_mod, b_mod.reshape(n_layers, 1, n))


def _ffn_kernel(*refs, mod_off, final):
    if final:
        h_ref, mod_ref, g_ref, win_ref, wout_ref, gfin_ref, o_ref = refs
    else:
        h_ref, mod_ref, g_ref, win_ref, wout_ref, o_ref = refs
    x = h_ref[0]
    mod = mod_ref[0]
    shift, scale, gate = mod[mod_off:mod_off + 1], mod[mod_off + 1:mod_off + 2], mod[mod_off + 2:mod_off + 3]
    y = (_rms(x, g_ref[...]) * (1.0 + scale) + shift).astype(BF16)
    acc = jnp.zeros(x.shape, F32)
    for c in range(D_FF // FFN_TF):
        gt = _dot(y, win_ref[:, c * FFN_TF:(c + 1) * FFN_TF])
        up = _dot(y, win_ref[:, D_FF + c * FFN_TF:D_FF + (c + 1) * FFN_TF])
        act = (_silu(gt) * up).astype(BF16)
        acc = acc + _dot(act, wout_ref[c * FFN_TF:(c + 1) * FFN_TF, :])
    out = x + (0.5 * gate) * acc
    if final:
        out = _rms(out, gfin_ref[...])
    o_ref[0] = out


def _ffn_call(h, mod, g, w_in, w_out, layer, mod_off, g_final=None):
    bx, t, d = h.shape
    tm = min(FFN_TM, t)
    final = g_final is not None
    in_specs = [pl.BlockSpec((1, tm, d), lambda b, i: (b, i, 0)),
                pl.BlockSpec((1, N_MOD, d), lambda b, i: (b, 0, 0)),
                _const_spec((1, d), 2),
                _layer_spec((d, 2 * D_FF), layer),
                _layer_spec((D_FF, d), layer)]
    args = [h, mod, g.reshape(1, d), w_in, w_out]
    if final:
        in_specs.append(_const_spec((1, d), 2))
        args.append(g_final.reshape(1, d))
    return pl.pallas_call(
        functools.partial(_ffn_kernel, mod_off=mod_off, final=final),
        grid=(bx, t // tm),
        in_specs=in_specs,
        out_specs=pl.BlockSpec((1, tm, d), lambda b, i: (b, i, 0)),
        out_shape=jax.ShapeDtypeStruct(h.shape, F32),
        compiler_params=_params(2),
        name="ffn",
    )(*args)


SWA_QSCALE = SWA_HD ** -0.5 * LOG2E
DIFF_QSCALE = DIFF_QK ** -0.5 * LOG2E


def _rot_half(blk, quarter, first):
    return jnp.where(first, pltpu.roll(blk, LANE - quarter, 1), pltpu.roll(blk, quarter, 1))


def _inproj_kernel(*refs, rope):
    if rope:
        h_ref, mod_ref, g_ref, w_ref, tab_ref, pg_ref, ps_ref, pd_ref, kts_ref, kt_ref = refs
    else:
        h_ref, mod_ref, g_ref, w_ref, pg_ref, ps_ref, pd_ref, kts_ref, kt_ref = refs
    x = h_ref[0]
    mod = mod_ref[0]
    y = (_rms(x, g_ref[...]) * (1.0 + mod[4:5]) + mod[3:4]).astype(BF16)
    p = _dot(y, w_ref[...])
    pg_ref[0] = p[:, :PG_W]
    swa = [p[:, PG_W + i * LANE:PG_W + (i + 1) * LANE] for i in range(PS_COLS // LANE)]
    dif = [p[:, PG_W + PS_COLS + i * LANE:PG_W + PS_COLS + (i + 1) * LANE] for i in range(PD_COLS // LANE)]
    if rope:
        tab = [tab_ref[:, i * LANE:(i + 1) * LANE] for i in range(8)]
        lane = lax.broadcasted_iota(jnp.int32, (1, LANE), 1)
        first_s = (lane % (SWA_HD // 2)) < (SWA_HD // 4)
        first_d = (lane % (DIFF_QK // 2)) < (DIFF_QK // 4)
        for i in range(5):
            c, s = (tab[0], tab[1]) if i < 4 else (tab[2], tab[3])
            swa[i] = swa[i] * c + _rot_half(swa[i], SWA_HD // 4, first_s) * s
        for i in range(4):
            c, s = (tab[4], tab[5]) if i < 2 else (tab[6], tab[7])
            dif[i] = dif[i] * c + _rot_half(dif[i], DIFF_QK // 4, first_d) * s
    else:
        for i in range(4):
            swa[i] = swa[i] * SWA_QSCALE
        for i in range(2):
            dif[i] = dif[i] * DIFF_QSCALE
    ps_ref[0] = jnp.concatenate(swa[0:4] + swa[5:7], axis=1).astype(BF16)
    kt = swa[4].T
    k0, k1 = kt[:SWA_HD], kt[SWA_HD:]
    kts_ref[0] = jnp.concatenate([k0, k0, k1, k1], axis=0).astype(BF16)
    pd_ref[0] = jnp.concatenate(dif[0:2] + dif[4:6], axis=1).astype(BF16)
    kt_ref[0] = jnp.concatenate(dif[2:4], axis=1).T.astype(BF16)


def _inproj_call(h, mod, g, w, layer, tab=None):
    bx, t, d = h.shape
    tm = min(FFN_TM, t)
    rope = tab is not None
    in_specs = [pl.BlockSpec((1, tm, d), lambda j, b: (b, j, 0)),
                pl.BlockSpec((1, N_MOD, d), lambda j, b: (b, 0, 0)),
                _const_spec((1, d), 2),
                _layer_spec((d, P_W), layer)]
    args = [h, mod, g.reshape(1, d), w]
    if rope:
        in_specs.append(pl.BlockSpec((tm, 8 * LANE), lambda j, b: (j, 0)))
        args.append(tab)
    out_map = lambda j, b: (b, j, 0)
    return pl.pallas_call(
        functools.partial(_inproj_kernel, rope=rope),
        grid=(t // tm, bx),
        in_specs=in_specs,
        out_specs=[pl.BlockSpec((1, tm, PG_W), out_map), pl.BlockSpec((1, tm, PS_W), out_map),
                   pl.BlockSpec((1, tm, PD_W), out_map),
                   pl.BlockSpec((1, 2 * LANE, tm), lambda j, b: (b, 0, j)),
                   pl.BlockSpec((1, DIFF_QK_W, tm), lambda j, b: (b, 0, j))],
        out_shape=[jax.ShapeDtypeStruct((bx, t, PG_W), F32), jax.ShapeDtypeStruct((bx, t, PS_W), BF16),
                   jax.ShapeDtypeStruct((bx, t, PD_W), BF16),
                   jax.ShapeDtypeStruct((bx, 2 * LANE, t), BF16),
                   jax.ShapeDtypeStruct((bx, DIFF_QK_W, t), BF16)],
        compiler_params=_params(2),
        name="inproj",
    )(*args)


GLA_SROWS = (GLA_LEVELS + 2) * GLA_CHUNK
GLA_LV_DIAG = GLA_LEVELS
GLA_LV_NONE = GLA_LEVELS + 1


def _gla_constants():
    c = GLA_CHUNK
    r = np.arange(c)[:, None]
    t = np.arange(c)[None, :]
    blocks = []
    for lv in range(GLA_LEVELS):
        half = 1 << lv
        mid = (r // (2 * half)) * (2 * half) + half
        second = r >= mid
        blocks.append(np.where(second, (t >= mid) & (t <= r), (t > r) & (t < mid)))
    blocks.append(t <= r)
    blocks.append(t > r)
    fwd = np.concatenate(blocks, axis=0).astype(np.float32)
    bwd = np.concatenate([b[::-1, ::-1] for b in blocks], axis=0).astype(np.float32)
    s2 = np.stack([np.concatenate([m, m], axis=1) for m in (fwd, bwd)])
    i = np.arange(c)[:, None]
    j = np.arange(c)[None, :]
    x = np.bitwise_xor(i, j)
    lvl = np.where(j > i, GLA_LV_NONE,
                   np.where(i == j, GLA_LV_DIAG, np.floor(np.log2(np.maximum(x, 1))).astype(np.int64)))
    lv_f = np.tile(lvl, (1, GLA_HEADS))
    lv_b = np.tile(lvl[::-1, ::-1], (1, GLA_HEADS))
    return jnp.asarray(s2, BF16), jnp.asarray(np.stack([lv_f, lv_b]), jnp.int32)


def _gla_kernel(*refs, ctx_out):
    if ctx_out:
        (pgx_ref, pgc_ref, wg_ref, bg_ref, gn_ref, s2_ref, lv_ref, ox_ref, oc_ref,
         oi_ref, qc_ref, upd_ref, dec_ref, st_ref) = refs
    else:
        (pgx_ref, pgc_ref, wg_ref, bg_ref, gn_ref, s2_ref, lv_ref, ox_ref,
         oi_ref, qc_ref, upd_ref, dec_ref, st_ref) = refs
        oc_ref = None
    c = GLA_CHUNK
    grp = GLA_GROUP
    hv = GLA_HEADS * GLA_DV
    n_ctx = pgc_ref.shape[1]
    s_len = pgx_ref.shape[1]
    row = lax.broadcasted_iota(jnp.int32, (hv, GLA_QK_W), 0)
    lane = lax.broadcasted_iota(jnp.int32, (hv, GLA_QK_W), 1)
    head_qk = (row // GLA_DV) == (lane // GLA_DK)
    head_qk_b = head_qk.astype(F32).astype(BF16)
    row2 = lax.broadcasted_iota(jnp.int32, (hv, hv), 0)
    col2 = lax.broadcasted_iota(jnp.int32, (hv, hv), 1)
    head_v_b = ((row2 // GLA_DV) == (col2 // GLA_DV)).astype(F32).astype(BF16)
    gn = gn_ref[...]
    wh, wl = _split_bf16(jnp.concatenate([wg_ref[0], wg_ref[1]], axis=1))
    w3 = jnp.concatenate([wh, wh, wl], axis=0)
    bias = jnp.concatenate([bg_ref[0], bg_ref[1]], axis=1)

    def rep4(a):
        return jnp.concatenate([a, a, a, a], axis=0)

    def local(src_ref, src_row, dst_row, with_out):
        blks = [src_ref[0, pl.ds(pl.multiple_of(src_row + g * c, c), c), :] for g in range(grp)]
        qs = [b[:, 0:128] * (GLA_DK ** -0.5) for b in blks]
        ks = [b[:, 128:256] for b in blks]
        vs = [b[:, 256:512].astype(BF16) for b in blks]
        zh, zl = _split_bf16(jnp.concatenate([b[:, 768:896] for b in blks], axis=0))
        z = _dot(jnp.concatenate([zh, zl, zh], axis=1), w3) + bias
        gate = (jnp.minimum(z, 0.0) - jnp.log(1.0 + jnp.exp(-jnp.abs(z)))) * (1.0 / GLA_TAU)
        gh, gl = _split_bf16(gate)
        if with_out:
            kreps = [rep4(k.astype(BF16)) * head_qk_b for k in ks]
            vbds = [rep4(v) * head_v_b for v in vs]
            diag = [_dot_nt(q.astype(BF16), kr) for q, kr in zip(qs, kreps)]
        for d in range(2):
            cols = slice(d * GLA_QK_W, (d + 1) * GLA_QK_W)
            g2 = jnp.concatenate([jnp.concatenate([gh[g * c:(g + 1) * c, cols], gl[g * c:(g + 1) * c, cols]], axis=0)
                                  for g in range(grp)], axis=1)
            f_all = jnp.exp(_dot(s2_ref[d], g2))
            fs = [f_all[:, g * GLA_QK_W:(g + 1) * GLA_QK_W] for g in range(grp)]
            last = GLA_LEVELS * c + (c - 1 if d == 0 else 0)
            for g in range(grp):
                rows = pl.ds(pl.multiple_of(dst_row + g * c, c), c)
                ci = (dst_row + g * c) // c
                f = fs[g]
                upd = _dot_tn(vs[g], (ks[g] * f[(GLA_LEVELS + 1) * c:(GLA_LEVELS + 2) * c]).astype(BF16))
                upd_ref[d, ci] = jnp.where(head_qk, upd, 0.0)
                dec_ref[d, ci] = jnp.broadcast_to(f[last:last + 1], (8, GLA_QK_W))
                if with_out:
                    qc_ref[d, rows, :] = (qs[g] * f[GLA_LEVELS * c:(GLA_LEVELS + 1) * c]).astype(BF16)
            if with_out:
                lv = lv_ref[d]
                atts = [jnp.where(lv == GLA_LV_DIAG, dg, 0.0) for dg in diag]
                for level in range(GLA_LEVELS):
                    for g in range(grp):
                        fl = fs[g][level * c:(level + 1) * c]
                        s = _dot_nt((qs[g] * fl).astype(BF16), rep4((ks[g] * fl).astype(BF16)) * head_qk_b)
                        atts[g] = jnp.where(lv == level, s, atts[g])
                for g in range(grp):
                    rows = pl.ds(pl.multiple_of(dst_row + g * c, c), c)
                    oi_ref[d, rows, :] = _dot(atts[g].astype(BF16), vbds[g])

    def local_pass(src_ref, dst0, with_out):
        def body(i, carry):
            local(src_ref, i * (grp * c), dst0 + i * (grp * c), with_out)
            return carry
        lax.fori_loop(0, src_ref.shape[1] // (grp * c), body, 0)

    def scan_pass(first, n, with_out):
        def body(i, carry):
            for d, ci in ((0, first + i), (1, first + n - 1 - i)):
                st = st_ref[d]
                if with_out:
                    rows = pl.ds(pl.multiple_of(ci * c, c), c)
                    oi_ref[d, rows, :] = oi_ref[d, rows, :] + _dot_nt(qc_ref[d, rows, :], st.astype(BF16))
                st_ref[d] = dec_ref[d, ci][0:1] * st + upd_ref[d, ci]
            return carry
        lax.fori_loop(0, n, body, 0)

    def finish_pass(src_ref, src0, out_ref, n_rows):
        tile = grp * c

        def body(i, carry):
            r = pl.multiple_of(i * tile, tile)
            rs = pl.ds(pl.multiple_of(src0 + r, tile), tile)
            o = oi_ref[0, rs, :] + oi_ref[1, rs, :]
            hi, lo = _split_bf16(o * o)
            ms = (_dot(hi, head_v_b) + _dot(lo, head_v_b)) * (1.0 / GLA_DV)
            og = src_ref[0, pl.ds(r, tile), 512:768]
            out_ref[0, pl.ds(r, tile), :] = (o * lax.rsqrt(ms + RMS_EPS) * gn * _silu(og)).astype(BF16)
            return carry
        lax.fori_loop(0, n_rows // tile, body, 0)

    local_pass(pgc_ref, 0, ctx_out)
    local_pass(pgx_ref, n_ctx, True)
    st_ref[...] = jnp.zeros(st_ref.shape, F32)
    scan_pass(0, n_ctx // c, ctx_out)
    scan_pass(n_ctx // c, s_len // c, True)
    finish_pass(pgx_ref, n_ctx, ox_ref, s_len)
    if ctx_out:
        finish_pass(pgc_ref, 0, oc_ref, n_ctx)


def _gla_call(pgx, pgc, wg, bg, gn, s2, lv, ctx_out):
    b, s, _ = pgx.shape
    n_ctx = pgc.shape[1]
    hv = GLA_WIDTH
    n_rows = s + n_ctx
    in_specs = [pl.BlockSpec((1, s, PG_W), lambda i: (i, 0, 0)),
                pl.BlockSpec((1, n_ctx, PG_W), lambda i: (i, 0, 0)),
                _const_spec(wg.shape, 1), _const_spec(bg.shape, 1), _const_spec((1, hv), 1),
                _const_spec(s2.shape, 1), _const_spec(lv.shape, 1)]
    out_specs = [pl.BlockSpec((1, s, hv), lambda i: (i, 0, 0))]
    out_shape = [jax.ShapeDtypeStruct((b, s, hv), BF16)]
    if ctx_out:
        out_specs.append(pl.BlockSpec((1, n_ctx, hv), lambda i: (i, 0, 0)))
        out_shape.append(jax.ShapeDtypeStruct((b, n_ctx, hv), BF16))
    scratch = [pltpu.VMEM((2, n_rows, hv), F32),
               pltpu.VMEM((2, n_rows, GLA_QK_W), BF16),
               pltpu.VMEM((2, n_rows // GLA_CHUNK, hv, GLA_QK_W), F32),
               pltpu.VMEM((2, n_rows // GLA_CHUNK, 8, GLA_QK_W), F32),
               pltpu.VMEM((2, hv, GLA_QK_W), F32)]
    res = pl.pallas_call(
        functools.partial(_gla_kernel, ctx_out=ctx_out),
        grid=(b,),
        in_specs=in_specs, out_specs=out_specs, out_shape=out_shape, scratch_shapes=scratch,
        compiler_params=_params(1),
        name="gla",
    )(pgx, pgc, wg, bg, gn.reshape(1, hv), s2, lv)
    return (res[0], res[1]) if ctx_out else (res[0], None)


def _swa_kernel(*refs, ctx_out):
    if ctx_out:
        sink_ref, px_ref, ktx_ref, pc_ref, ktc_ref, ox_ref, oc_ref = refs
    else:
        sink_ref, px_ref, ktx_ref, pc_ref, ktc_ref, ox_ref = refs
    s_len = px_ref.shape[1]
    n_ctx = pc_ref.shape[1]
    blk = SWA_BLOCK
    band = 3 * blk
    grp = SWA_HEADS // SWA_KV_HEADS
    lane = lax.broadcasted_iota(jnp.int32, (1, LANE), 1)
    low = lane < SWA_HD
    half_mask = [low.astype(F32).astype(BF16), (~low).astype(F32).astype(BF16)]
    rel = (lax.broadcasted_iota(jnp.int32, (blk, band), 0)
           - lax.broadcasted_iota(jnp.int32, (blk, band), 1))

    def col(i):
        return slice(i * LANE, (i + 1) * LANE)

    def krows(kv):
        return slice(kv * LANE, (kv + 1) * LANE)

    def sink_col(n_rows, kv, half):
        ha = grp * kv + half
        return jnp.concatenate([jnp.full((n_rows, 1), sink_ref[ha], F32),
                                jnp.full((n_rows, 1), sink_ref[ha + 2], F32)], axis=0)

    def softmax_pv(sc, v, snk):
        m = jnp.maximum(snk, jnp.max(sc, axis=-1, keepdims=True))
        p = jnp.exp2(sc - m).astype(BF16)
        r = _dot(p, jnp.concatenate([v, jnp.ones(v.shape, BF16)], axis=1))
        return r[:, :LANE] / (r[:, LANE:] + jnp.exp2(snk - m))

    def pipeline(tasks):
        outs = []
        nxt = tasks[0][0]()
        for t, (_, finish) in enumerate(tasks):
            cur = nxt
            if t + 1 < len(tasks):
                nxt = tasks[t + 1][0]()
            outs.append(finish(cur))
        return outs

    units = [(kv, half) for kv in range(SWA_KV_HEADS) for half in range(2)]
    blocks_per_step = 2

    def body(i, carry):
        tasks, rows = [], []
        for sub in range(blocks_per_step):
            n = i * blocks_per_step + sub
            r0 = pl.multiple_of(n * blk, blk)
            start = pl.multiple_of(jnp.clip((n - 1) * blk, 0, s_len - band), blk)
            bias = jnp.where(jnp.abs(rel + (r0 - start)) <= SWA_WINDOW, 0.0, -jnp.inf)
            bias = jnp.concatenate([bias, jnp.zeros((blk, n_ctx), F32)], axis=1)
            bias = jnp.concatenate([bias, bias], axis=0)
            rows.append(r0)
            for kv, half in units:
                def scores(kv=kv, half=half, r0=r0, start=start, bias=bias):
                    q = jnp.concatenate([px_ref[0, pl.ds(r0, blk), col(2 * kv)],
                                         px_ref[0, pl.ds(r0, blk), col(2 * kv + 1)]], axis=0) * half_mask[half]
                    kt = jnp.concatenate([ktx_ref[0, krows(kv), pl.ds(start, band)],
                                          ktc_ref[0, krows(kv), :]], axis=1)
                    return _dot(q, kt) + bias

                def finish(cur, kv=kv, half=half, start=start):
                    v = jnp.concatenate([px_ref[0, pl.ds(start, band), col(4 + kv)],
                                         pc_ref[0, :, col(4 + kv)]], axis=0)
                    return softmax_pv(cur, v, sink_col(blk, kv, half))

                tasks.append((scores, finish))
        outs = pipeline(tasks)
        for sub in range(blocks_per_step):
            for kv in range(SWA_KV_HEADS):
                t = (sub * SWA_KV_HEADS + kv) * 2
                o2 = jnp.where(low, outs[t], outs[t + 1]).astype(BF16)
                ox_ref[0, pl.ds(rows[sub], blk), col(2 * kv)] = o2[:blk]
                ox_ref[0, pl.ds(rows[sub], blk), col(2 * kv + 1)] = o2[blk:]
        return carry

    lax.fori_loop(0, s_len // (blk * blocks_per_step), body, 0)

    if ctx_out:
        tasks = []
        for kv, half in units:
            def scores(kv=kv, half=half):
                q = jnp.concatenate([pc_ref[0, :, col(2 * kv)], pc_ref[0, :, col(2 * kv + 1)]], axis=0)
                return _dot(q * half_mask[half], ktc_ref[0, krows(kv), :])

            def finish(cur, kv=kv, half=half):
                return softmax_pv(cur, pc_ref[0, :, col(4 + kv)], sink_col(n_ctx, kv, half))

            tasks.append((scores, finish))
        outs = pipeline(tasks)
        for kv in range(SWA_KV_HEADS):
            o2 = jnp.where(low, outs[2 * kv], outs[2 * kv + 1]).astype(BF16)
            oc_ref[0, :, col(2 * kv)] = o2[:n_ctx]
            oc_ref[0, :, col(2 * kv + 1)] = o2[n_ctx:]


def _swa_call(psx, ktx, psc, ktc, sink, ctx_out):
    b, s, _ = psx.shape
    n_ctx = psc.shape[1]
    in_specs = [pl.BlockSpec(memory_space=pltpu.SMEM),
                pl.BlockSpec((1, s, PS_W), lambda i: (i, 0, 0)),
                pl.BlockSpec((1, 2 * LANE, s), lambda i: (i, 0, 0)),
                pl.BlockSpec((1, n_ctx, PS_W), lambda i: (i, 0, 0)),
                pl.BlockSpec((1, 2 * LANE, n_ctx), lambda i: (0, 0, i))]
    out_specs = [pl.BlockSpec((1, s, SWA_WIDTH), lambda i: (i, 0, 0))]
    out_shape = [jax.ShapeDtypeStruct((b, s, SWA_WIDTH), BF16)]
    if ctx_out:
        out_specs.append(pl.BlockSpec((1, n_ctx, SWA_WIDTH), lambda i: (i, 0, 0)))
        out_shape.append(jax.ShapeDtypeStruct((b, n_ctx, SWA_WIDTH), BF16))
    res = pl.pallas_call(
        functools.partial(_swa_kernel, ctx_out=ctx_out),
        grid=(b,),
        in_specs=in_specs, out_specs=out_specs, out_shape=out_shape,
        compiler_params=_params(1),
        name="swa",
    )(sink, psx, ktx, psc, ktc)
    return (res[0], res[1]) if ctx_out else (res[0], None)


def _diff_kernel(*refs, ctx_out, lambda_init):
    if ctx_out:
        px_ref, ktx_ref, pc_ref, ktc_ref, lam_ref, gn_ref, ox_ref, oc_ref, kt_ref, va_ref = refs
    else:
        px_ref, ktx_ref, pc_ref, ktc_ref, lam_ref, gn_ref, ox_ref, kt_ref, va_ref = refs
    s_len = px_ref.shape[1]
    w = DIFF_QK_W
    lv = lam_ref[...]
    lam = (jnp.exp(jnp.sum(lv[0:1] * lv[1:2], axis=-1, keepdims=True))
           - jnp.exp(jnp.sum(lv[2:3] * lv[3:4], axis=-1, keepdims=True)) + lambda_init)
    lane = lax.broadcasted_iota(jnp.int32, (1, w), 1)
    row2 = lax.broadcasted_iota(jnp.int32, (w, w), 0)
    col2 = lax.broadcasted_iota(jnp.int32, (w, w), 1)
    head_ones = ((row2 // DIFF_V) == (col2 // DIFF_V)).astype(F32).astype(BF16)
    gn = gn_ref[...] * (1.0 - lambda_init)
    unit_masks = [((lane >= DIFF_QK * u) & (lane < DIFF_QK * (u + 1))).astype(F32).astype(BF16)
                  for u in range(2 * DIFF_HEADS)]
    head_masks = [(lane >= DIFF_V * h) & (lane < DIFF_V * (h + 1)) for h in range(DIFF_HEADS)]

    kt_ref[:, :s_len] = ktx_ref[0]
    kt_ref[:, s_len:] = ktc_ref[0]
    for h in range(DIFF_HEADS):
        keep = head_masks[h].astype(F32).astype(BF16)
        va_ref[h, :s_len, :] = px_ref[0, :, w:2 * w] * keep + (1.0 - keep)
        va_ref[h, s_len:, :] = pc_ref[0, :, w:2 * w] * keep + (1.0 - keep)

    def attend(q, k0):
        def scores(u):
            return _dot(q * unit_masks[u], kt_ref[:, k0:])

        def softmax_pv(sc, u):
            m = jnp.max(sc, axis=-1, keepdims=True)
            return _dot(jnp.exp2(sc - m).astype(BF16), va_ref[u // 2, k0:, :])

        units = []
        nxt = scores(0)
        for u in range(2 * DIFF_HEADS):
            cur = nxt
            if u + 1 < 2 * DIFF_HEADS:
                nxt = scores(u + 1)
            units.append(softmax_pv(cur, u))
        o = jnp.zeros((q.shape[0], w), F32)
        for h in range(DIFF_HEADS):
            r1, r2 = units[2 * h], units[2 * h + 1]
            oh = r1 / pltpu.roll(r1, w // 2, 1) - lam * (r2 / pltpu.roll(r2, w // 2, 1))
            o = jnp.where(head_masks[h], oh, o)
        hi, lo = _split_bf16(o * o)
        ms = (_dot(hi, head_ones) + _dot(lo, head_ones)) * (1.0 / DIFF_V)
        return (o * lax.rsqrt(ms + RMS_EPS) * gn).astype(BF16)

    def body(n, carry):
        r0 = pl.multiple_of(n * DIFF_QBLOCK, DIFF_QBLOCK)
        q = px_ref[0, pl.ds(r0, DIFF_QBLOCK), 0:w]
        ox_ref[0, pl.ds(r0, DIFF_QBLOCK), :] = attend(q, 0)
        return carry

    lax.fori_loop(0, s_len // DIFF_QBLOCK, body, 0)
    if ctx_out:
        oc_ref[0] = attend(pc_ref[0, :, 0:w], s_len)


def _diff_call(pdx, ktx, pdc, ktc, lam, gn, lambda_init, ctx_out):
    b, s, _ = pdx.shape
    n_ctx = pdc.shape[1]
    in_specs = [pl.BlockSpec((1, s, PD_W), lambda i: (i, 0, 0)),
                pl.BlockSpec((1, DIFF_QK_W, s), lambda i: (i, 0, 0)),
                pl.BlockSpec((1, n_ctx, PD_W), lambda i: (i, 0, 0)),
                pl.BlockSpec((1, DIFF_QK_W, n_ctx), lambda i: (0, 0, i)),
                _const_spec(lam.shape, 1), _const_spec((1, DIFF_WIDTH), 1)]
    out_specs = [pl.BlockSpec((1, s, DIFF_WIDTH), lambda i: (i, 0, 0))]
    out_shape = [jax.ShapeDtypeStruct((b, s, DIFF_WIDTH), BF16)]
    if ctx_out:
        out_specs.append(pl.BlockSpec((1, n_ctx, DIFF_WIDTH), lambda i: (i, 0, 0)))
        out_shape.append(jax.ShapeDtypeStruct((b, n_ctx, DIFF_WIDTH), BF16))
    res = pl.pallas_call(
        functools.partial(_diff_kernel, ctx_out=ctx_out, lambda_init=lambda_init),
        grid=(b,),
        in_specs=in_specs, out_specs=out_specs, out_shape=out_shape,
        scratch_shapes=[pltpu.VMEM((DIFF_QK_W, s + n_ctx), BF16),
                        pltpu.VMEM((DIFF_HEADS, s + n_ctx, DIFF_WIDTH), BF16)],
        compiler_params=_params(1),
        name="diff",
    )(pdx, ktx, pdc, ktc, lam, gn.reshape(1, DIFF_WIDTH))
    return (res[0], res[1]) if ctx_out else (res[0], None)


def _outproj_kernel(h_ref, mod_ref, yg_ref, ys_ref, yd_ref, w_ref, o_ref):
    a = GLA_WIDTH
    b = GLA_WIDTH + SWA_WIDTH
    acc = (_dot(yg_ref[0], w_ref[0:a, :]) + _dot(ys_ref[0], w_ref[a:b, :]) + _dot(yd_ref[0], w_ref[b:, :]))
    o_ref[0] = h_ref[0] + mod_ref[0][5:6] * acc


def _outproj_call(h, mod, yg, ys, yd, w, layer):
    bx, t, d = h.shape
    tm = min(FFN_TM, t)
    tile = lambda width: pl.BlockSpec((1, tm, width), lambda b, i: (b, i, 0))
    return pl.pallas_call(
        _outproj_kernel,
        grid=(bx, t // tm),
        in_specs=[tile(d), pl.BlockSpec((1, N_MOD, d), lambda b, i: (b, 0, 0)),
                  tile(GLA_WIDTH), tile(SWA_WIDTH), tile(DIFF_WIDTH), _layer_spec((MIX_WIDTH, d), layer)],
        out_specs=tile(d),
        out_shape=jax.ShapeDtypeStruct(h.shape, F32),
        compiler_params=_params(2),
        name="outproj",
    )(h, mod, yg, ys, yd, w)


def _prep_w_in(w):
    parts, off = [], 0
    for sz in IN_SIZES:
        parts.append(w[..., off:off + sz].astype(BF16))
        off += sz
    gq, gk, gv, gf, gb, og, sq, sk, sv, dq, dk, dv = parts
    pad = jnp.zeros(w.shape[:-1] + (PG_W - (2 * GLA_QK_W + 2 * GLA_WIDTH + 2 * GLA_GATE_RANK),), BF16)
    v0, v1 = sv[..., :SWA_HD], sv[..., SWA_HD:]
    cols = [gq, gk, gv, og, gf, gb, pad, sq, sk, v0, v0, v1, v1, dq, dk, dv]
    return jnp.concatenate(cols, axis=-1)


def _prep_gate(w_gate, b_gate):
    wg = jnp.zeros((2, LANE, GLA_QK_W), F32)
    for d in range(2):
        wg = wg.at[d, GLA_GATE_RANK * d:GLA_GATE_RANK * (d + 1), :].set(w_gate[d])
    return wg, b_gate.reshape(2, 1, GLA_QK_W)


def _axial_angles(rows, head_dim):
    half = head_dim // 2
    row = jnp.repeat(jnp.arange(rows, dtype=F32), GRID_W)
    col = jnp.tile(jnp.arange(GRID_W, dtype=F32), rows)
    inv_freq = 1.0 / (ROPE_BASE ** (jnp.arange(0, half, 2, dtype=F32) / half))

    def axis_angles(pos):
        a = pos[:, None] * inv_freq[None, :]
        return jnp.concatenate([a, a], axis=-1)

    return jnp.concatenate([axis_angles(row), axis_angles(col)], axis=-1)


def _rope_table(seq):
    rows = seq // GRID_W
    blocks = []
    for head_dim, qscale in ((SWA_HD, SWA_QSCALE), (DIFF_QK, DIFF_QSCALE)):
        ang = _axial_angles(rows, head_dim)
        quarter = head_dim // 4
        sign = jnp.where((jnp.arange(head_dim) % (2 * quarter)) < quarter, -1.0, 1.0).astype(F32)
        reps = LANE // head_dim
        cos = jnp.tile(jnp.cos(ang), (1, reps))
        sin = jnp.tile(jnp.sin(ang) * sign[None, :], (1, reps))
        blocks += [cos * qscale, sin * qscale, cos, sin]
    return jnp.concatenate(blocks, axis=1)


def kernel(x, c, ctx, c_ctx, w_mod, b_mod, g_ffn1, w_ffn1_in, w_ffn1_out, g_mix, w_in, w_out, w_gla_gate,
           b_gla_gate, g_gla_norm, swa_sink, diff_lambda, g_diff_norm, g_ffn2, w_ffn2_in, w_ffn2_out, g_final):
    b, s, d = x.shape
    n_ctx = ctx.shape[1]
    depth = w_mod.shape[0]

    cs = jnp.zeros((MOD_ROWS, d), F32).at[:b].set(c).at[b].set(c_ctx)
    mod = _mod_call(cs, w_mod, b_mod).reshape(depth, MOD_ROWS, N_MOD, d)
    tab = _rope_table(s)
    s2, lv = _gla_constants()

    w1i, w1o = w_ffn1_in.astype(BF16), w_ffn1_out.astype(BF16)
    w2i, w2o = w_ffn2_in.astype(BF16), w_ffn2_out.astype(BF16)
    wi = _prep_w_in(w_in)
    wo = w_out.astype(BF16)

    hx = x
    hc = ctx.reshape(1, b * n_ctx, d)
    for l in range(depth):
        ctx_out = l < depth - 1
        last = l == depth - 1
        mod_x = mod[l, :b]
        mod_c = mod[l, b:b + 1]
        lambda_init = 0.8 - 0.6 * math.exp(-0.3 * l)
        wg, bg = _prep_gate(w_gla_gate[l], b_gla_gate[l])
        sink = swa_sink[l] * LOG2E

        hx = _ffn_call(hx, mod_x, g_ffn1[l], w1i, w1o, l, 0)
        hc = _ffn_call(hc, mod_c, g_ffn1[l], w1i, w1o, l, 0)
        pgx, psx, pdx, ksx, ktx = _inproj_call(hx, mod_x, g_mix[l], wi, l, tab)
        pgc, psc, pdc, ksc, ktc = _inproj_call(hc, mod_c, g_mix[l], wi, l)
        pgc, psc, pdc = (a.reshape(b, n_ctx, a.shape[-1]) for a in (pgc, psc, pdc))

        gla_x, gla_c = _gla_call(pgx, pgc, wg, bg, g_gla_norm[l], s2, lv, ctx_out)
        swa_x, swa_c = _swa_call(psx, ksx, psc, ksc, sink, ctx_out)
        dif_x, dif_c = _diff_call(pdx, ktx, pdc, ktc, diff_lambda[l], g_diff_norm[l], lambda_init, ctx_out)

        hx = _outproj_call(hx, mod_x, gla_x, swa_x, dif_x, wo, l)
        hx = _ffn_call(hx, mod_x, g_ffn2[l], w2i, w2o, l, 6, g_final if last else None)
        if ctx_out:
            flat = lambda a: a.reshape(1, b * n_ctx, a.shape[-1])
            hc = _outproj_call(hc, mod_c, flat(gla_c), flat(swa_c), flat(dif_c), wo, l)
            hc = _ffn_call(hc, mod_c, g_ffn2[l], w2i, w2o, l, 6)
    return hx
```

```python
import functools
import math

import numpy as np
import jax
import jax.numpy as jnp
from jax import lax
from jax.experimental import pallas as pl
from jax.experimental.pallas import tpu as pltpu

F32 = jnp.float32
BF16 = jnp.bfloat16

D_MODEL = 1024
DEPTH = 2
GRID_W = 64
N_MOD = 9
D_FF = 2816
RMS_EPS = 1e-6
ROPE_BASE = 10000.0

GLA_HEADS = 4
GLA_DK = 32
GLA_DV = 64
GLA_GATE_RANK = 16
GLA_TAU = 16.0
SWA_HEADS = 8
SWA_KV_HEADS = 2
SWA_HD = 64
SWA_WINDOW = 128
SWA_BLOCK = 128
DIFF_HEADS = 4
DIFF_QK = 32
DIFF_V = 64
DIFF_QBLOCK = 128

GLA_QK_W = GLA_HEADS * GLA_DK
GLA_WIDTH = GLA_HEADS * GLA_DV
SWA_WIDTH = SWA_HEADS * SWA_HD
SWA_KV_W = SWA_KV_HEADS * SWA_HD
DIFF_QK_W = DIFF_HEADS * 2 * DIFF_QK
DIFF_WIDTH = DIFF_HEADS * DIFF_V
MIX_WIDTH = GLA_WIDTH + SWA_WIDTH + DIFF_WIDTH
IN_SIZES = (GLA_QK_W, GLA_QK_W, GLA_WIDTH, GLA_GATE_RANK, GLA_GATE_RANK, GLA_WIDTH,
            SWA_WIDTH, SWA_KV_W, SWA_KV_W, DIFF_QK_W, DIFF_QK_W, DIFF_WIDTH)

LANE = 128
VMEM_LIMIT = 56 * 1024 * 1024
LOG2E = math.log2(math.e)

PG_W = 896
PS_COLS = 896
PS_W = 768
PD_COLS = 768
PD_W = 512
P_W = PG_W + PS_COLS + PD_COLS
GLA_CHUNK = 64
GLA_LEVELS = 6
GLA_GROUP = 4
MOD_ROWS = 16
MOD_TN = 1152
FFN_TM = 512
FFN_TF = D_FF // 2


def _dot(a, b):
    return jnp.dot(a, b, preferred_element_type=F32)


def _dot_nt(a, b):
    return lax.dot_general(a, b, (((1,), (1,)), ((), ())), preferred_element_type=F32)


def _dot_tn(a, b):
    return lax.dot_general(a, b, (((0,), (0,)), ((), ())), preferred_element_type=F32)


def _split_bf16(x):
    hi = x.astype(BF16)
    lo = (x - hi.astype(F32)).astype(BF16)
    return hi, lo


def _rms(x, g):
    return x * lax.rsqrt(jnp.mean(x * x, axis=-1, keepdims=True) + RMS_EPS) * g


def _silu(x):
    return x * (1.0 / (1.0 + jnp.exp(-x)))


def _params(n_grid):
    return pltpu.CompilerParams(dimension_semantics=("arbitrary",) * n_grid, vmem_limit_bytes=VMEM_LIMIT)


def _const_spec(shape, n_grid, single=False):
    zeros = (0,) * len(shape)
    index_map = {1: lambda a: zeros, 2: lambda a, b: zeros}[n_grid]
    if single:
        return pl.BlockSpec(shape, index_map, pipeline_mode=pl.Buffered(1))
    return pl.BlockSpec(shape, index_map)


def _layer_spec(shape, layer):
    return pl.BlockSpec((None,) + tuple(shape), lambda b, i: (layer, 0, 0), pipeline_mode=pl.Buffered(1))


def _mod_kernel(a_ref, w_ref, b_ref, o_ref):
    a = _silu(a_ref[...]).astype(BF16)
    o_ref[0] = _dot(a, w_ref[0].astype(BF16)) + b_ref[0]


def _mod_call(cs, w_mod, b_mod):
    n_layers, d, n = w_mod.shape
    return pl.pallas_call(
        _mod_kernel,
        grid=(n_layers, n // MOD_TN),
        in_specs=[pl.BlockSpec((MOD_ROWS, d), lambda l, j: (0, 0)),
                  pl.BlockSpec((1, d, MOD_TN), lambda l, j: (l, 0, j)),
                  pl.BlockSpec((1, 1, MOD_TN), lambda l, j: (l, 0, j))],
        out_specs=pl.BlockSpec((1, MOD_ROWS, MOD_TN), lambda l, j: (l, 0, j)),
        out_shape=jax.ShapeDtypeStruct((n_layers, MOD_ROWS, n), F32),
        compiler_params=_params(2),
        name="mod",
    )(cs, w_mod, b_mod.reshape(n_layers, 1, n))


def _ffn_kernel(*refs, mod_off, mix, final):
    refs = list(refs)
    h_ref, mod_ref, g_ref, win_ref, wout_ref = refs[:5]
    o_ref = refs[-1]
    x = h_ref[0]
    mod = mod_ref[0]
    if mix:
        yg_ref, ys_ref, yd_ref, wmix_ref = refs[5:9]
        a = GLA_WIDTH
        b = GLA_WIDTH + SWA_WIDTH
        proj = (_dot(yg_ref[0], wmix_ref[0:a, :]) + _dot(ys_ref[0], wmix_ref[a:b, :])
                + _dot(yd_ref[0], wmix_ref[b:, :]))
        x = x + mod[5:6] * proj
    if final:
        gfin_ref = refs[-2]
    shift, scale, gate = mod[mod_off:mod_off + 1], mod[mod_off + 1:mod_off + 2], mod[mod_off + 2:mod_off + 3]
    y = (_rms(x, g_ref[...]) * (1.0 + scale) + shift).astype(BF16)
    acc = jnp.zeros(x.shape, F32)
    for c in range(D_FF // FFN_TF):
        gt = _dot(y, win_ref[:, c * FFN_TF:(c + 1) * FFN_TF])
        up = _dot(y, win_ref[:, D_FF + c * FFN_TF:D_FF + (c + 1) * FFN_TF])
        act = (_silu(gt) * up).astype(BF16)
        acc = acc + _dot(act, wout_ref[c * FFN_TF:(c + 1) * FFN_TF, :])
    out = x + (0.5 * gate) * acc
    if final:
        out = _rms(out, gfin_ref[...])
    o_ref[0] = out


def _ffn_call(h, mod, g, w_in, w_out, layer, mod_off, mix=None, g_final=None):
    bx, t, d = h.shape
    tm = min(FFN_TM, t)
    final = g_final is not None
    tile = lambda width: pl.BlockSpec((1, tm, width), lambda b, i: (b, i, 0))
    in_specs = [tile(d),
                pl.BlockSpec((1, N_MOD, d), lambda b, i: (b, 0, 0)),
                _const_spec((1, d), 2),
                _layer_spec((d, 2 * D_FF), layer),
                _layer_spec((D_FF, d), layer)]
    args = [h, mod, g.reshape(1, d), w_in, w_out]
    if mix is not None:
        in_specs += [tile(GLA_WIDTH), tile(SWA_WIDTH), tile(DIFF_WIDTH), _layer_spec((MIX_WIDTH, d), layer)]
        args += list(mix)
    if final:
        in_specs.append(_const_spec((1, d), 2))
        args.append(g_final.reshape(1, d))
    return pl.pallas_call(
        functools.partial(_ffn_kernel, mod_off=mod_off, mix=mix is not None, final=final),
        grid=(bx, t // tm),
        in_specs=in_specs,
        out_specs=pl.BlockSpec((1, tm, d), lambda b, i: (b, i, 0)),
        out_shape=jax.ShapeDtypeStruct(h.shape, F32),
        compiler_params=_params(2),
        name="ffn",
    )(*args)


SWA_QSCALE = SWA_HD ** -0.5 * LOG2E
DIFF_QSCALE = DIFF_QK ** -0.5 * LOG2E


def _rot_half(blk, quarter, first):
    return jnp.where(first, pltpu.roll(blk, LANE - quarter, 1), pltpu.roll(blk, quarter, 1))


def _inproj_kernel(*refs, rope):
    if rope:
        h_ref, mod_ref, g_ref, w_ref, tab_ref, pg_ref, ps_ref, pd_ref, kts_ref, kt_ref = refs
    else:
        h_ref, mod_ref, g_ref, w_ref, pg_ref, ps_ref, pd_ref, kts_ref, kt_ref = refs
    x = h_ref[0]
    mod = mod_ref[0]
    y = (_rms(x, g_ref[...]) * (1.0 + mod[4:5]) + mod[3:4]).astype(BF16)
    p = _dot(y, w_ref[...])
    pg_ref[0] = p[:, :PG_W]
    swa = [p[:, PG_W + i * LANE:PG_W + (i + 1) * LANE] for i in range(PS_COLS // LANE)]
    dif = [p[:, PG_W + PS_COLS + i * LANE:PG_W + PS_COLS + (i + 1) * LANE] for i in range(PD_COLS // LANE)]
    if rope:
        tab = [tab_ref[:, i * LANE:(i + 1) * LANE] for i in range(8)]
        lane = lax.broadcasted_iota(jnp.int32, (1, LANE), 1)
        first_s = (lane % (SWA_HD // 2)) < (SWA_HD // 4)
        first_d = (lane % (DIFF_QK // 2)) < (DIFF_QK // 4)
        for i in range(5):
            c, s = (tab[0], tab[1]) if i < 4 else (tab[2], tab[3])
            swa[i] = swa[i] * c + _rot_half(swa[i], SWA_HD // 4, first_s) * s
        for i in range(4):
            c, s = (tab[4], tab[5]) if i < 2 else (tab[6], tab[7])
            dif[i] = dif[i] * c + _rot_half(dif[i], DIFF_QK // 4, first_d) * s
    else:
        for i in range(4):
            swa[i] = swa[i] * SWA_QSCALE
        for i in range(2):
            dif[i] = dif[i] * DIFF_QSCALE
    ps_ref[0] = jnp.concatenate(swa[0:4] + swa[5:7], axis=1).astype(BF16)
    kt = swa[4].T
    k0, k1 = kt[:SWA_HD], kt[SWA_HD:]
    kts_ref[0] = jnp.concatenate([k0, k0, k1, k1], axis=0).astype(BF16)
    pd_ref[0] = jnp.concatenate(dif[0:2] + dif[4:6], axis=1).astype(BF16)
    kt_ref[0] = jnp.concatenate(dif[2:4], axis=1).T.astype(BF16)


def _inproj_call(h, mod, g, w, layer, tab=None):
    bx, t, d = h.shape
    tm = min(FFN_TM, t)
    rope = tab is not None
    in_specs = [pl.BlockSpec((1, tm, d), lambda j, b: (b, j, 0)),
                pl.BlockSpec((1, N_MOD, d), lambda j, b: (b, 0, 0)),
                _const_spec((1, d), 2),
                _layer_spec((d, P_W), layer)]
    args = [h, mod, g.reshape(1, d), w]
    if rope:
        in_specs.append(pl.BlockSpec((tm, 8 * LANE), lambda j, b: (j, 0)))
        args.append(tab)
    out_map = lambda j, b: (b, j, 0)
    return pl.pallas_call(
        functools.partial(_inproj_kernel, rope=rope),
        grid=(t // tm, bx),
        in_specs=in_specs,
        out_specs=[pl.BlockSpec((1, tm, PG_W), out_map), pl.BlockSpec((1, tm, PS_W), out_map),
                   pl.BlockSpec((1, tm, PD_W), out_map),
                   pl.BlockSpec((1, 2 * LANE, tm), lambda j, b: (b, 0, j)),
                   pl.BlockSpec((1, DIFF_QK_W, tm), lambda j, b: (b, 0, j))],
        out_shape=[jax.ShapeDtypeStruct((bx, t, PG_W), F32), jax.ShapeDtypeStruct((bx, t, PS_W), BF16),
                   jax.ShapeDtypeStruct((bx, t, PD_W), BF16),
                   jax.ShapeDtypeStruct((bx, 2 * LANE, t), BF16),
                   jax.ShapeDtypeStruct((bx, DIFF_QK_W, t), BF16)],
        compiler_params=_params(2),
        name="inproj",
    )(*args)


GLA_SROWS = (GLA_LEVELS + 2) * GLA_CHUNK
GLA_LV_DIAG = GLA_LEVELS
GLA_LV_NONE = GLA_LEVELS + 1


def _gla_constants():
    c = GLA_CHUNK
    r = np.arange(c)[:, None]
    t = np.arange(c)[None, :]
    blocks = []
    for lv in range(GLA_LEVELS):
        half = 1 << lv
        mid = (r // (2 * half)) * (2 * half) + half
        second = r >= mid
        blocks.append(np.where(second, (t >= mid) & (t <= r), (t > r) & (t < mid)))
    blocks.append(t <= r)
    blocks.append(t > r)
    fwd = np.concatenate(blocks, axis=0).astype(np.float32)
    bwd = np.concatenate([b[::-1, ::-1] for b in blocks], axis=0).astype(np.float32)
    s2 = np.stack([np.concatenate([m, m], axis=1) for m in (fwd, bwd)])
    i = np.arange(c)[:, None]
    j = np.arange(c)[None, :]
    x = np.bitwise_xor(i, j)
    lvl = np.where(j > i, GLA_LV_NONE,
                   np.where(i == j, GLA_LV_DIAG, np.floor(np.log2(np.maximum(x, 1))).astype(np.int64)))
    lv_f = np.tile(lvl, (2, 2))
    lv_b = np.tile(lvl[::-1, ::-1], (2, 2))
    return jnp.asarray(s2, BF16), jnp.asarray(np.stack([lv_f, lv_b]), jnp.int32)


def _gla_kernel(*refs, ctx_out):
    if ctx_out:
        (pgx_ref, pgc_ref, wg_ref, bg_ref, gn_ref, s2_ref, lv_ref, ox_ref, oc_ref,
         oi_ref, qc_ref, upd_ref, dec_ref, st_ref) = refs
    else:
        (pgx_ref, pgc_ref, wg_ref, bg_ref, gn_ref, s2_ref, lv_ref, ox_ref,
         oi_ref, qc_ref, upd_ref, dec_ref, st_ref) = refs
        oc_ref = None
    c = GLA_CHUNK
    grp = GLA_GROUP
    hv = GLA_HEADS * GLA_DV
    n_ctx = pgc_ref.shape[1]
    s_len = pgx_ref.shape[1]
    row = lax.broadcasted_iota(jnp.int32, (hv, GLA_QK_W), 0)
    lane = lax.broadcasted_iota(jnp.int32, (hv, GLA_QK_W), 1)
    head_qk = (row // GLA_DV) == (lane // GLA_DK)
    head_qk_b = head_qk.astype(F32).astype(BF16)
    row2 = lax.broadcasted_iota(jnp.int32, (hv, hv), 0)
    col2 = lax.broadcasted_iota(jnp.int32, (hv, hv), 1)
    head_v_b = ((row2 // GLA_DV) == (col2 // GLA_DV)).astype(F32).astype(BF16)
    gn = gn_ref[...]
    wh, wl = _split_bf16(jnp.concatenate([wg_ref[0], wg_ref[1]], axis=1))
    w3 = jnp.concatenate([wh, wh, wl], axis=0)
    bias = jnp.concatenate([bg_ref[0], bg_ref[1]], axis=1)

    lane1 = lax.broadcasted_iota(jnp.int32, (1, GLA_QK_W), 1)
    pair_mask = [((lane1 // (2 * GLA_DK)) == p).astype(F32).astype(BF16) for p in range(2)]
    parity_mask = [(((lane1 // GLA_DK) % 2) == hh).astype(F32).astype(BF16) for hh in range(2)]
    lane_v = lax.broadcasted_iota(jnp.int32, (1, hv), 1)
    hv_mask = [((lane_v // GLA_DV) == h).astype(F32).astype(BF16) for h in range(GLA_HEADS)]

    def pair_rows(a):
        a = a.astype(BF16)
        return jnp.concatenate([a * pair_mask[0], a * pair_mask[1]], axis=0)

    def parity_rows(a):
        a = a.astype(BF16)
        return jnp.concatenate([a * parity_mask[0], a * parity_mask[1]], axis=0)

    def local(src_ref, src_row, dst_row, with_out):
        blks = [src_ref[0, pl.ds(pl.multiple_of(src_row + g * c, c), c), :] for g in range(grp)]
        qs = [b[:, 0:128] * (GLA_DK ** -0.5) for b in blks]
        ks = [b[:, 128:256] for b in blks]
        vs = [b[:, 256:512].astype(BF16) for b in blks]
        zh, zl = _split_bf16(jnp.concatenate([b[:, 768:896] for b in blks], axis=0))
        z = _dot(jnp.concatenate([zh, zl, zh], axis=1), w3) + bias
        gate = (jnp.minimum(z, 0.0) - jnp.log(1.0 + jnp.exp(-jnp.abs(z)))) * (1.0 / GLA_TAU)
        gh, gl = _split_bf16(gate)
        f_all = []
        for d in range(2):
            cols = slice(d * GLA_QK_W, (d + 1) * GLA_QK_W)
            g2 = jnp.concatenate([jnp.concatenate([gh[g * c:(g + 1) * c, cols], gl[g * c:(g + 1) * c, cols]], axis=0)
                                  for g in range(grp)], axis=1)
            f_all.append(jnp.exp(_dot(s2_ref[d], g2)))

        def fac(d, g, block):
            return f_all[d][block * c:(block + 1) * c, g * GLA_QK_W:(g + 1) * GLA_QK_W]

        if with_out:
            vbds = [[jnp.concatenate([v * hv_mask[2 * p], v * hv_mask[2 * p + 1]], axis=0) for p in range(2)]
                    for v in vs]
            diag = [_dot_nt(pair_rows(q), parity_rows(k)) for q, k in zip(qs, ks)]
        for d in range(2):
            for g in range(grp):
                rows = pl.ds(pl.multiple_of(dst_row + g * c, c), c)
                ci = (dst_row + g * c) // c
                upd = _dot_tn(vs[g], (ks[g] * fac(d, g, GLA_LEVELS + 1)).astype(BF16))
                upd_ref[d, ci] = jnp.where(head_qk, upd, 0.0)
                fcum = fac(d, g, GLA_LEVELS)
                last = c - 1 if d == 0 else 0
                dec_ref[d, ci] = jnp.broadcast_to(fcum[last:last + 1], (8, GLA_QK_W))
                if with_out:
                    qc_ref[d, rows, :] = (qs[g] * fcum).astype(BF16)
            if with_out:
                lv = lv_ref[d]
                atts = [jnp.where(lv == GLA_LV_DIAG, dg, 0.0) for dg in diag]
                for level in range(GLA_LEVELS):
                    for g in range(grp):
                        fl = fac(d, g, level)
                        s = _dot_nt(pair_rows(qs[g] * fl), parity_rows(ks[g] * fl))
                        atts[g] = jnp.where(lv == level, s, atts[g])
                for g in range(grp):
                    rows = pl.ds(pl.multiple_of(dst_row + g * c, c), c)
                    att = atts[g].astype(BF16)
                    oi_ref[d, rows, :] = _dot(att[:c], vbds[g][0]) + _dot(att[c:], vbds[g][1])

    def local_pass(src_ref, dst0, with_out):
        def body(i, carry):
            local(src_ref, i * (grp * c), dst0 + i * (grp * c), with_out)
            return carry
        lax.fori_loop(0, src_ref.shape[1] // (grp * c), body, 0)

    def scan_pass(first, n, with_out):
        def body(i, carry):
            for d, ci in ((0, first + i), (1, first + n - 1 - i)):
                st = st_ref[d]
                if with_out:
                    rows = pl.ds(pl.multiple_of(ci * c, c), c)
                    oi_ref[d, rows, :] = oi_ref[d, rows, :] + _dot_nt(qc_ref[d, rows, :], st.astype(BF16))
                st_ref[d] = dec_ref[d, ci][0:1] * st + upd_ref[d, ci]
            return carry
        lax.fori_loop(0, n, body, 0)

    def finish_pass(src_ref, src0, out_ref, n_rows):
        tile = grp * c

        def body(i, carry):
            r = pl.multiple_of(i * tile, tile)
            rs = pl.ds(pl.multiple_of(src0 + r, tile), tile)
            o = oi_ref[0, rs, :] + oi_ref[1, rs, :]
            hi, lo = _split_bf16(o * o)
            ms = (_dot(hi, head_v_b) + _dot(lo, head_v_b)) * (1.0 / GLA_DV)
            og = src_ref[0, pl.ds(r, tile), 512:768]
            out_ref[0, pl.ds(r, tile), :] = (o * lax.rsqrt(ms + RMS_EPS) * gn * _silu(og)).astype(BF16)
            return carry
        lax.fori_loop(0, n_rows // tile, body, 0)

    local_pass(pgc_ref, 0, ctx_out)
    local_pass(pgx_ref, n_ctx, True)
    st_ref[...] = jnp.zeros(st_ref.shape, F32)
    scan_pass(0, n_ctx // c, ctx_out)
    scan_pass(n_ctx // c, s_len // c, True)
    finish_pass(pgx_ref, n_ctx, ox_ref, s_len)
    if ctx_out:
        finish_pass(pgc_ref, 0, oc_ref, n_ctx)


def _gla_call(pgx, pgc, wg, bg, gn, s2, lv, ctx_out):
    b, s, _ = pgx.shape
    n_ctx = pgc.shape[1]
    hv = GLA_WIDTH
    n_rows = s + n_ctx
    in_specs = [pl.BlockSpec((1, s, PG_W), lambda i: (i, 0, 0)),
                pl.BlockSpec((1, n_ctx, PG_W), lambda i: (i, 0, 0)),
                _const_spec(wg.shape, 1), _const_spec(bg.shape, 1), _const_spec((1, hv), 1),
                _const_spec(s2.shape, 1), _const_spec(lv.shape, 1)]
    out_specs = [pl.BlockSpec((1, s, hv), lambda i: (i, 0, 0))]
    out_shape = [jax.ShapeDtypeStruct((b, s, hv), BF16)]
    if ctx_out:
        out_specs.append(pl.BlockSpec((1, n_ctx, hv), lambda i: (i, 0, 0)))
        out_shape.append(jax.ShapeDtypeStruct((b, n_ctx, hv), BF16))
    scratch = [pltpu.VMEM((2, n_rows, hv), F32),
               pltpu.VMEM((2, n_rows, GLA_QK_W), BF16),
               pltpu.VMEM((2, n_rows // GLA_CHUNK, hv, GLA_QK_W), F32),
               pltpu.VMEM((2, n_rows // GLA_CHUNK, 8, GLA_QK_W), F32),
               pltpu.VMEM((2, hv, GLA_QK_W), F32)]
    res = pl.pallas_call(
        functools.partial(_gla_kernel, ctx_out=ctx_out),
        grid=(b,),
        in_specs=in_specs, out_specs=out_specs, out_shape=out_shape, scratch_shapes=scratch,
        compiler_params=_params(1),
        name="gla",
    )(pgx, pgc, wg, bg, gn.reshape(1, hv), s2, lv)
    return (res[0], res[1]) if ctx_out else (res[0], None)


def _swa_kernel(*refs, ctx_out):
    if ctx_out:
        sink_ref, px_ref, ktx_ref, pc_ref, ktc_ref, ox_ref, oc_ref = refs
    else:
        sink_ref, px_ref, ktx_ref, pc_ref, ktc_ref, ox_ref = refs
    s_len = px_ref.shape[1]
    n_ctx = pc_ref.shape[1]
    blk = SWA_BLOCK
    band = 3 * blk
    grp = SWA_HEADS // SWA_KV_HEADS
    lane = lax.broadcasted_iota(jnp.int32, (1, LANE), 1)
    low = lane < SWA_HD
    half_mask = [low.astype(F32).astype(BF16), (~low).astype(F32).astype(BF16)]
    rel = (lax.broadcasted_iota(jnp.int32, (blk, band), 0)
           - lax.broadcasted_iota(jnp.int32, (blk, band), 1))

    def col(i):
        return slice(i * LANE, (i + 1) * LANE)

    def krows(kv):
        return slice(kv * LANE, (kv + 1) * LANE)

    def sink_col(n_rows, kv, half):
        ha = grp * kv + half
        return jnp.concatenate([jnp.full((n_rows, 1), sink_ref[ha], F32),
                                jnp.full((n_rows, 1), sink_ref[ha + 2], F32)], axis=0)

    def softmax_pv(sc, v, snk):
        m = jnp.maximum(snk, jnp.max(sc, axis=-1, keepdims=True))
        p = jnp.exp2(sc - m).astype(BF16)
        r = _dot(p, jnp.concatenate([v, jnp.ones(v.shape, BF16)], axis=1))
        return r[:, :LANE] / (r[:, LANE:] + jnp.exp2(snk - m))

    def pipeline(tasks):
        outs = []
        nxt = tasks[0][0]()
        for t, (_, finish) in enumerate(tasks):
            cur = nxt
            if t + 1 < len(tasks):
                nxt = tasks[t + 1][0]()
            outs.append(finish(cur))
        return outs

    units = [(kv, half) for kv in range(SWA_KV_HEADS) for half in range(2)]
    blocks_per_step = 2

    def body(i, carry):
        tasks, rows = [], []
        for sub in range(blocks_per_step):
            n = i * blocks_per_step + sub
            r0 = pl.multiple_of(n * blk, blk)
            start = pl.multiple_of(jnp.clip((n - 1) * blk, 0, s_len - band), blk)
            bias = jnp.where(jnp.abs(rel + (r0 - start)) <= SWA_WINDOW, 0.0, -jnp.inf)
            bias = jnp.concatenate([bias, jnp.zeros((blk, n_ctx), F32)], axis=1)
            bias = jnp.concatenate([bias, bias], axis=0)
            rows.append(r0)
            for kv, half in units:
                def scores(kv=kv, half=half, r0=r0, start=start, bias=bias):
                    q = jnp.concatenate([px_ref[0, pl.ds(r0, blk), col(2 * kv)],
                                         px_ref[0, pl.ds(r0, blk), col(2 * kv + 1)]], axis=0) * half_mask[half]
                    kt = jnp.concatenate([ktx_ref[0, krows(kv), pl.ds(start, band)],
                                          ktc_ref[0, krows(kv), :]], axis=1)
                    return _dot(q, kt) + bias

                def finish(cur, kv=kv, half=half, start=start):
                    v = jnp.concatenate([px_ref[0, pl.ds(start, band), col(4 + kv)],
                                         pc_ref[0, :, col(4 + kv)]], axis=0)
                    return softmax_pv(cur, v, sink_col(blk, kv, half))

                tasks.append((scores, finish))
        outs = pipeline(tasks)
        for sub in range(blocks_per_step):
            for kv in range(SWA_KV_HEADS):
                t = (sub * SWA_KV_HEADS + kv) * 2
                o2 = jnp.where(low, outs[t], outs[t + 1]).astype(BF16)
                ox_ref[0, pl.ds(rows[sub], blk), col(2 * kv)] = o2[:blk]
                ox_ref[0, pl.ds(rows[sub], blk), col(2 * kv + 1)] = o2[blk:]
        return carry

    lax.fori_loop(0, s_len // (blk * blocks_per_step), body, 0)

    if ctx_out:
        tasks = []
        for kv, half in units:
            def scores(kv=kv, half=half):
                q = jnp.concatenate([pc_ref[0, :, col(2 * kv)], pc_ref[0, :, col(2 * kv + 1)]], axis=0)
                return _dot(q * half_mask[half], ktc_ref[0, krows(kv), :])

            def finish(cur, kv=kv, half=half):
                return softmax_pv(cur, pc_ref[0, :, col(4 + kv)], sink_col(n_ctx, kv, half))

            tasks.append((scores, finish))
        outs = pipeline(tasks)
        for kv in range(SWA_KV_HEADS):
            o2 = jnp.where(low, outs[2 * kv], outs[2 * kv + 1]).astype(BF16)
            oc_ref[0, :, col(2 * kv)] = o2[:n_ctx]
            oc_ref[0, :, col(2 * kv + 1)] = o2[n_ctx:]


def _swa_call(psx, ktx, psc, ktc, sink, ctx_out):
    b, s, _ = psx.shape
    n_ctx = psc.shape[1]
    in_specs = [pl.BlockSpec(memory_space=pltpu.SMEM),
                pl.BlockSpec((1, s, PS_W), lambda i: (i, 0, 0)),
                pl.BlockSpec((1, 2 * LANE, s), lambda i: (i, 0, 0)),
                pl.BlockSpec((1, n_ctx, PS_W), lambda i: (i, 0, 0)),
                pl.BlockSpec((1, 2 * LANE, n_ctx), lambda i: (0, 0, i))]
    out_specs = [pl.BlockSpec((1, s, SWA_WIDTH), lambda i: (i, 0, 0))]
    out_shape = [jax.ShapeDtypeStruct((b, s, SWA_WIDTH), BF16)]
    if ctx_out:
        out_specs.append(pl.BlockSpec((1, n_ctx, SWA_WIDTH), lambda i: (i, 0, 0)))
        out_shape.append(jax.ShapeDtypeStruct((b, n_ctx, SWA_WIDTH), BF16))
    res = pl.pallas_call(
        functools.partial(_swa_kernel, ctx_out=ctx_out),
        grid=(b,),
        in_specs=in_specs, out_specs=out_specs, out_shape=out_shape,
        compiler_params=_params(1),
        name="swa",
    )(sink, psx, ktx, psc, ktc)
    return (res[0], res[1]) if ctx_out else (res[0], None)


def _diff_kernel(*refs, ctx_out, lambda_init):
    if ctx_out:
        px_ref, ktx_ref, pc_ref, ktc_ref, lam_ref, gn_ref, ox_ref, oc_ref, kt_ref, va_ref, sc_ref = refs
    else:
        px_ref, ktx_ref, pc_ref, ktc_ref, lam_ref, gn_ref, ox_ref, kt_ref, va_ref, sc_ref = refs
    s_len = px_ref.shape[1]
    w = DIFF_QK_W
    lv = lam_ref[...]
    lam = (jnp.exp(jnp.sum(lv[0:1] * lv[1:2], axis=-1, keepdims=True))
           - jnp.exp(jnp.sum(lv[2:3] * lv[3:4], axis=-1, keepdims=True)) + lambda_init)
    lane = lax.broadcasted_iota(jnp.int32, (1, w), 1)
    row2 = lax.broadcasted_iota(jnp.int32, (w, w), 0)
    col2 = lax.broadcasted_iota(jnp.int32, (w, w), 1)
    head_ones = ((row2 // DIFF_V) == (col2 // DIFF_V)).astype(F32).astype(BF16)
    gn = gn_ref[...] * (1.0 - lambda_init)
    unit_masks = [((lane >= DIFF_QK * u) & (lane < DIFF_QK * (u + 1))).astype(F32).astype(BF16)
                  for u in range(2 * DIFF_HEADS)]
    head_masks = [(lane >= DIFF_V * h) & (lane < DIFF_V * (h + 1)) for h in range(DIFF_HEADS)]

    kt_ref[:, :s_len] = ktx_ref[0]
    kt_ref[:, s_len:] = ktc_ref[0]
    for h in range(DIFF_HEADS):
        keep = head_masks[h].astype(F32).astype(BF16)
        va_ref[h, :s_len, :] = px_ref[0, :, w:2 * w] * keep + (1.0 - keep)
        va_ref[h, s_len:, :] = pc_ref[0, :, w:2 * w] * keep + (1.0 - keep)

    def attend(q, k0, q_next=None):
        def scores(qq, u):
            return _dot(qq * unit_masks[u], kt_ref[:, k0:])

        def softmax_pv(sc, u):
            m = jnp.max(sc, axis=-1, keepdims=True)
            return _dot(jnp.exp2(sc - m).astype(BF16), va_ref[u // 2, k0:, :])

        units = []
        nxt = scores(q, 0) if q_next is None else sc_ref[...]
        for u in range(2 * DIFF_HEADS):
            cur = nxt
            if u + 1 < 2 * DIFF_HEADS:
                nxt = scores(q, u + 1)
            elif q_next is not None:
                sc_ref[...] = scores(q_next, 0)
            units.append(softmax_pv(cur, u))
        o = jnp.zeros((q.shape[0], w), F32)
        for h in range(DIFF_HEADS):
            r1, r2 = units[2 * h], units[2 * h + 1]
            oh = r1 / pltpu.roll(r1, w // 2, 1) - lam * (r2 / pltpu.roll(r2, w // 2, 1))
            o = jnp.where(head_masks[h], oh, o)
        hi, lo = _split_bf16(o * o)
        ms = (_dot(hi, head_ones) + _dot(lo, head_ones)) * (1.0 / DIFF_V)
        return (o * lax.rsqrt(ms + RMS_EPS) * gn).astype(BF16)

    n_blocks = s_len // DIFF_QBLOCK

    def q_block(n):
        return px_ref[0, pl.ds(pl.multiple_of(n * DIFF_QBLOCK, DIFF_QBLOCK), DIFF_QBLOCK), 0:w]

    def body(n, carry):
        out = attend(q_block(n), 0, q_block(jnp.minimum(n + 1, n_blocks - 1)))
        ox_ref[0, pl.ds(pl.multiple_of(n * DIFF_QBLOCK, DIFF_QBLOCK), DIFF_QBLOCK), :] = out
        return carry

    sc_ref[...] = _dot(q_block(0) * unit_masks[0], kt_ref[...])
    lax.fori_loop(0, n_blocks, body, 0)
    if ctx_out:
        oc_ref[0] = attend(pc_ref[0, :, 0:w], s_len)


def _diff_call(pdx, ktx, pdc, ktc, lam, gn, lambda_init, ctx_out):
    b, s, _ = pdx.shape
    n_ctx = pdc.shape[1]
    in_specs = [pl.BlockSpec((1, s, PD_W), lambda i: (i, 0, 0)),
                pl.BlockSpec((1, DIFF_QK_W, s), lambda i: (i, 0, 0)),
                pl.BlockSpec((1, n_ctx, PD_W), lambda i: (i, 0, 0)),
                pl.BlockSpec((1, DIFF_QK_W, n_ctx), lambda i: (0, 0, i)),
                _const_spec(lam.shape, 1), _const_spec((1, DIFF_WIDTH), 1)]
    out_specs = [pl.BlockSpec((1, s, DIFF_WIDTH), lambda i: (i, 0, 0))]
    out_shape = [jax.ShapeDtypeStruct((b, s, DIFF_WIDTH), BF16)]
    if ctx_out:
        out_specs.append(pl.BlockSpec((1, n_ctx, DIFF_WIDTH), lambda i: (i, 0, 0)))
        out_shape.append(jax.ShapeDtypeStruct((b, n_ctx, DIFF_WIDTH), BF16))
    res = pl.pallas_call(
        functools.partial(_diff_kernel, ctx_out=ctx_out, lambda_init=lambda_init),
        grid=(b,),
        in_specs=in_specs, out_specs=out_specs, out_shape=out_shape,
        scratch_shapes=[pltpu.VMEM((DIFF_QK_W, s + n_ctx), BF16),
                        pltpu.VMEM((DIFF_HEADS, s + n_ctx, DIFF_WIDTH), BF16),
                        pltpu.VMEM((DIFF_QBLOCK, s + n_ctx), F32)],
        compiler_params=_params(1),
        name="diff",
    )(pdx, ktx, pdc, ktc, lam, gn.reshape(1, DIFF_WIDTH))
    return (res[0], res[1]) if ctx_out else (res[0], None)


def _prep_w_in(w):
    parts, off = [], 0
    for sz in IN_SIZES:
        parts.append(w[..., off:off + sz].astype(BF16))
        off += sz
    gq, gk, gv, gf, gb, og, sq, sk, sv, dq, dk, dv = parts
    pad = jnp.zeros(w.shape[:-1] + (PG_W - (2 * GLA_QK_W + 2 * GLA_WIDTH + 2 * GLA_GATE_RANK),), BF16)
    v0, v1 = sv[..., :SWA_HD], sv[..., SWA_HD:]
    cols = [gq, gk, gv, og, gf, gb, pad, sq, sk, v0, v0, v1, v1, dq, dk, dv]
    return jnp.concatenate(cols, axis=-1)


def _prep_gate(w_gate, b_gate):
    wg = jnp.zeros((2, LANE, GLA_QK_W), F32)
    for d in range(2):
        wg = wg.at[d, GLA_GATE_RANK * d:GLA_GATE_RANK * (d + 1), :].set(w_gate[d])
    return wg, b_gate.reshape(2, 1, GLA_QK_W)


def _axial_angles(rows, head_dim):
    half = head_dim // 2
    row = jnp.repeat(jnp.arange(rows, dtype=F32), GRID_W)
    col = jnp.tile(jnp.arange(GRID_W, dtype=F32), rows)
    inv_freq = 1.0 / (ROPE_BASE ** (jnp.arange(0, half, 2, dtype=F32) / half))

    def axis_angles(pos):
        a = pos[:, None] * inv_freq[None, :]
        return jnp.concatenate([a, a], axis=-1)

    return jnp.concatenate([axis_angles(row), axis_angles(col)], axis=-1)


def _rope_table(seq):
    rows = seq // GRID_W
    blocks = []
    for head_dim, qscale in ((SWA_HD, SWA_QSCALE), (DIFF_QK, DIFF_QSCALE)):
        ang = _axial_angles(rows, head_dim)
        quarter = head_dim // 4
        sign = jnp.where((jnp.arange(head_dim) % (2 * quarter)) < quarter, -1.0, 1.0).astype(F32)
        reps = LANE // head_dim
        cos = jnp.tile(jnp.cos(ang), (1, reps))
        sin = jnp.tile(jnp.sin(ang) * sign[None, :], (1, reps))
        blocks += [cos * qscale, sin * qscale, cos, sin]
    return jnp.concatenate(blocks, axis=1)


def kernel(x, c, ctx, c_ctx, w_mod, b_mod, g_ffn1, w_ffn1_in, w_ffn1_out, g_mix, w_in, w_out, w_gla_gate,
           b_gla_gate, g_gla_norm, swa_sink, diff_lambda, g_diff_norm, g_ffn2, w_ffn2_in, w_ffn2_out, g_final):
    b, s, d = x.shape
    n_ctx = ctx.shape[1]
    depth = w_mod.shape[0]

    cs = jnp.zeros((MOD_ROWS, d), F32).at[:b].set(c).at[b].set(c_ctx)
    mod = _mod_call(cs, w_mod, b_mod).reshape(depth, MOD_ROWS, N_MOD, d)
    tab = _rope_table(s)
    s2, lv = _gla_constants()

    w1i, w1o = w_ffn1_in.astype(BF16), w_ffn1_out.astype(BF16)
    w2i, w2o = w_ffn2_in.astype(BF16), w_ffn2_out.astype(BF16)
    wi = _prep_w_in(w_in)
    wo = w_out.astype(BF16)

    hx = x
    hc = ctx.reshape(1, b * n_ctx, d)
    for l in range(depth):
        ctx_out = l < depth - 1
        last = l == depth - 1
        mod_x = mod[l, :b]
        mod_c = mod[l, b:b + 1]
        lambda_init = 0.8 - 0.6 * math.exp(-0.3 * l)
        wg, bg = _prep_gate(w_gla_gate[l], b_gla_gate[l])
        sink = swa_sink[l] * LOG2E

        hx = _ffn_call(hx, mod_x, g_ffn1[l], w1i, w1o, l, 0)
        hc = _ffn_call(hc, mod_c, g_ffn1[l], w1i, w1o, l, 0)
        pgx, psx, pdx, ksx, ktx = _inproj_call(hx, mod_x, g_mix[l], wi, l, tab)
        pgc, psc, pdc, ksc, ktc = _inproj_call(hc, mod_c, g_mix[l], wi, l)
        pgc, psc, pdc = (a.reshape(b, n_ctx, a.shape[-1]) for a in (pgc, psc, pdc))

        gla_x, gla_c = _gla_call(pgx, pgc, wg, bg, g_gla_norm[l], s2, lv, ctx_out)
        swa_x, swa_c = _swa_call(psx, ksx, psc, ksc, sink, ctx_out)
        dif_x, dif_c = _diff_call(pdx, ktx, pdc, ktc, diff_lambda[l], g_diff_norm[l], lambda_init, ctx_out)

        hx = _ffn_call(hx, mod_x, g_ffn2[l], w2i, w2o, l, 6, mix=(gla_x, swa_x, dif_x, wo),
                       g_final=g_final if last else None)
        if ctx_out:
            flat = lambda a: a.reshape(1, b * n_ctx, a.shape[-1])
            hc = _ffn_call(hc, mod_c, g_ffn2[l], w2i, w2o, l, 6, mix=(flat(gla_c), flat(swa_c), flat(dif_c), wo))
    return hx
```

```python
import functools
import math

import numpy as np
import jax
import jax.numpy as jnp
from jax import lax
from jax.experimental import pallas as pl
from jax.experimental.pallas import tpu as pltpu

F32 = jnp.float32
BF16 = jnp.bfloat16

D_MODEL = 1024
DEPTH = 2
GRID_W = 64
N_MOD = 9
D_FF = 2816
RMS_EPS = 1e-6
ROPE_BASE = 10000.0

GLA_HEADS = 4
GLA_DK = 32
GLA_DV = 64
GLA_GATE_RANK = 16
GLA_TAU = 16.0
SWA_HEADS = 8
SWA_KV_HEADS = 2
SWA_HD = 64
SWA_WINDOW = 128
SWA_BLOCK = 128
DIFF_HEADS = 4
DIFF_QK = 32
DIFF_V = 64
DIFF_QBLOCK = 512

GLA_QK_W = GLA_HEADS * GLA_DK
GLA_WIDTH = GLA_HEADS * GLA_DV
SWA_WIDTH = SWA_HEADS * SWA_HD
SWA_KV_W = SWA_KV_HEADS * SWA_HD
DIFF_QK_W = DIFF_HEADS * 2 * DIFF_QK
DIFF_WIDTH = DIFF_HEADS * DIFF_V
MIX_WIDTH = GLA_WIDTH + SWA_WIDTH + DIFF_WIDTH
IN_SIZES = (GLA_QK_W, GLA_QK_W, GLA_WIDTH, GLA_GATE_RANK, GLA_GATE_RANK, GLA_WIDTH,
            SWA_WIDTH, SWA_KV_W, SWA_KV_W, DIFF_QK_W, DIFF_QK_W, DIFF_WIDTH)

LANE = 128
VMEM_LIMIT = 56 * 1024 * 1024
LOG2E = math.log2(math.e)

PG_W = 896
PS_COLS = 896
PS_W = 768
PD_COLS = 768
PD_W = 512
P_W = PG_W + PS_COLS + PD_COLS
GLA_CHUNK = 64
GLA_LEVELS = 6
GLA_GROUP = 4
GLA_SCAN_UNROLL = 4
SWA_PIPE_DEPTH = 2
MOD_ROWS = 16
MOD_TN = 1152
FFN_TM = 512
FFN_TF = D_FF // 2


def _dot(a, b):
    return jnp.dot(a, b, preferred_element_type=F32)


def _dot_nt(a, b):
    return lax.dot_general(a, b, (((1,), (1,)), ((), ())), preferred_element_type=F32)


def _dot_tn(a, b):
    return lax.dot_general(a, b, (((0,), (0,)), ((), ())), preferred_element_type=F32)


def _split_bf16(x):
    hi = x.astype(BF16)
    lo = (x - hi.astype(F32)).astype(BF16)
    return hi, lo


def _rms(x, g):
    return x * lax.rsqrt(jnp.mean(x * x, axis=-1, keepdims=True) + RMS_EPS) * g


def _silu(x):
    return x * (1.0 / (1.0 + jnp.exp(-x)))


def _params(n_grid):
    return pltpu.CompilerParams(dimension_semantics=("arbitrary",) * n_grid, vmem_limit_bytes=VMEM_LIMIT)


def _const_spec(shape, n_grid, single=False):
    zeros = (0,) * len(shape)
    index_map = {1: lambda a: zeros, 2: lambda a, b: zeros}[n_grid]
    if single:
        return pl.BlockSpec(shape, index_map, pipeline_mode=pl.Buffered(1))
    return pl.BlockSpec(shape, index_map)


def _layer_spec(shape, layer):
    return pl.BlockSpec((None,) + tuple(shape), lambda b, i: (layer, 0, 0), pipeline_mode=pl.Buffered(1))


def _mod_kernel(a_ref, w_ref, b_ref, o_ref):
    a = _silu(a_ref[...]).astype(BF16)
    o_ref[0] = _dot(a, w_ref[0].astype(BF16)) + b_ref[0]


def _mod_call(cs, w_mod, b_mod):
    n_layers, d, n = w_mod.shape
    return pl.pallas_call(
        _mod_kernel,
        grid=(n_layers, n // MOD_TN),
        in_specs=[pl.BlockSpec((MOD_ROWS, d), lambda l, j: (0, 0)),
                  pl.BlockSpec((1, d, MOD_TN), lambda l, j: (l, 0, j)),
                  pl.BlockSpec((1, 1, MOD_TN), lambda l, j: (l, 0, j))],
        out_specs=pl.BlockSpec((1, MOD_ROWS, MOD_TN), lambda l, j: (l, 0, j)),
        out_shape=jax.ShapeDtypeStruct((n_layers, MOD_ROWS, n), F32),
        compiler_params=_params(2),
        name="mod",
    )(cs, w_mod, b_mod.reshape(n_layers, 1, n))


def _ffn_kernel(*refs, mod_off, mix, final):
    refs = list(refs)
    h_ref, mod_ref, g_ref, win_ref, wout_ref = refs[:5]
    o_ref = refs[-1]
    x = h_ref[0]
    mod = mod_ref[0]
    if mix:
        yg_ref, ys_ref, yd_ref, wmix_ref = refs[5:9]
        a = GLA_WIDTH
        b = GLA_WIDTH + SWA_WIDTH
        proj = (_dot(yg_ref[0], wmix_ref[0:a, :]) + _dot(ys_ref[0], wmix_ref[a:b, :])
                + _dot(yd_ref[0], wmix_ref[b:, :]))
        x = x + mod[5:6] * proj
    if final:
        gfin_ref = refs[-2]
    shift, scale, gate = mod[mod_off:mod_off + 1], mod[mod_off + 1:mod_off + 2], mod[mod_off + 2:mod_off + 3]
    y = (_rms(x, g_ref[...]) * (1.0 + scale) + shift).astype(BF16)
    acc = jnp.zeros(x.shape, F32)
    for c in range(D_FF // FFN_TF):
        gt = _dot(y, win_ref[:, c * FFN_TF:(c + 1) * FFN_TF])
        up = _dot(y, win_ref[:, D_FF + c * FFN_TF:D_FF + (c + 1) * FFN_TF])
        act = (_silu(gt) * up).astype(BF16)
        acc = acc + _dot(act, wout_ref[c * FFN_TF:(c + 1) * FFN_TF, :])
    out = x + (0.5 * gate) * acc
    if final:
        out = _rms(out, gfin_ref[...])
    o_ref[0] = out


def _ffn_call(h, mod, g, w_in, w_out, layer, mod_off, mix=None, g_final=None):
    bx, t, d = h.shape
    tm = min(FFN_TM, t)
    final = g_final is not None
    tile = lambda width: pl.BlockSpec((1, tm, width), lambda b, i: (b, i, 0))
    in_specs = [tile(d),
                pl.BlockSpec((1, N_MOD, d), lambda b, i: (b, 0, 0)),
                _const_spec((1, d), 2),
                _layer_spec((d, 2 * D_FF), layer),
                _layer_spec((D_FF, d), layer)]
    args = [h, mod, g.reshape(1, d), w_in, w_out]
    if mix is not None:
        in_specs += [tile(GLA_WIDTH), tile(SWA_WIDTH), tile(DIFF_WIDTH), _layer_spec((MIX_WIDTH, d), layer)]
        args += list(mix)
    if final:
        in_specs.append(_const_spec((1, d), 2))
        args.append(g_final.reshape(1, d))
    return pl.pallas_call(
        functools.partial(_ffn_kernel, mod_off=mod_off, mix=mix is not None, final=final),
        grid=(bx, t // tm),
        in_specs=in_specs,
        out_specs=pl.BlockSpec((1, tm, d), lambda b, i: (b, i, 0)),
        out_shape=jax.ShapeDtypeStruct(h.shape, F32),
        compiler_params=_params(2),
        name="ffn",
    )(*args)


SWA_QSCALE = SWA_HD ** -0.5 * LOG2E
DIFF_QSCALE = DIFF_QK ** -0.5 * LOG2E


def _rot_half(blk, quarter, first):
    return jnp.where(first, pltpu.roll(blk, LANE - quarter, 1), pltpu.roll(blk, quarter, 1))


def _inproj_kernel(*refs, rope):
    if rope:
        h_ref, mod_ref, g_ref, w_ref, tab_ref, pg_ref, ps_ref, pd_ref, kts_ref, kt_ref = refs
    else:
        h_ref, mod_ref, g_ref, w_ref, pg_ref, ps_ref, pd_ref, kts_ref, kt_ref = refs
    x = h_ref[0]
    mod = mod_ref[0]
    y = (_rms(x, g_ref[...]) * (1.0 + mod[4:5]) + mod[3:4]).astype(BF16)
    p = _dot(y, w_ref[...])
    pg_ref[0] = p[:, :PG_W]
    swa = [p[:, PG_W + i * LANE:PG_W + (i + 1) * LANE] for i in range(PS_COLS // LANE)]
    dif = [p[:, PG_W + PS_COLS + i * LANE:PG_W + PS_COLS + (i + 1) * LANE] for i in range(PD_COLS // LANE)]
    if rope:
        tab = [tab_ref[:, i * LANE:(i + 1) * LANE] for i in range(8)]
        lane = lax.broadcasted_iota(jnp.int32, (1, LANE), 1)
        first_s = (lane % (SWA_HD // 2)) < (SWA_HD // 4)
        first_d = (lane % (DIFF_QK // 2)) < (DIFF_QK // 4)
        for i in range(5):
            c, s = (tab[0], tab[1]) if i < 4 else (tab[2], tab[3])
            swa[i] = swa[i] * c + _rot_half(swa[i], SWA_HD // 4, first_s) * s
        for i in range(4):
            c, s = (tab[4], tab[5]) if i < 2 else (tab[6], tab[7])
            dif[i] = dif[i] * c + _rot_half(dif[i], DIFF_QK // 4, first_d) * s
    else:
        for i in range(4):
            swa[i] = swa[i] * SWA_QSCALE
        for i in range(2):
            dif[i] = dif[i] * DIFF_QSCALE
    ps_ref[0] = jnp.concatenate(swa[0:4] + swa[5:7], axis=1).astype(BF16)
    kt = swa[4].T
    k0, k1 = kt[:SWA_HD], kt[SWA_HD:]
    kts_ref[0] = jnp.concatenate([k0, k0, k1, k1], axis=0).astype(BF16)
    pd_ref[0] = jnp.concatenate(dif[0:2] + dif[4:6], axis=1).astype(BF16)
    kt_ref[0] = jnp.concatenate(dif[2:4], axis=1).T.astype(BF16)


def _inproj_call(h, mod, g, w, layer, tab=None):
    bx, t, d = h.shape
    tm = min(FFN_TM, t)
    rope = tab is not None
    in_specs = [pl.BlockSpec((1, tm, d), lambda j, b: (b, j, 0)),
                pl.BlockSpec((1, N_MOD, d), lambda j, b: (b, 0, 0)),
                _const_spec((1, d), 2),
                _layer_spec((d, P_W), layer)]
    args = [h, mod, g.reshape(1, d), w]
    if rope:
        in_specs.append(pl.BlockSpec((tm, 8 * LANE), lambda j, b: (j, 0)))
        args.append(tab)
    out_map = lambda j, b: (b, j, 0)
    return pl.pallas_call(
        functools.partial(_inproj_kernel, rope=rope),
        grid=(t // tm, bx),
        in_specs=in_specs,
        out_specs=[pl.BlockSpec((1, tm, PG_W), out_map), pl.BlockSpec((1, tm, PS_W), out_map),
                   pl.BlockSpec((1, tm, PD_W), out_map),
                   pl.BlockSpec((1, 2 * LANE, tm), lambda j, b: (b, 0, j)),
                   pl.BlockSpec((1, DIFF_QK_W, tm), lambda j, b: (b, 0, j))],
        out_shape=[jax.ShapeDtypeStruct((bx, t, PG_W), F32), jax.ShapeDtypeStruct((bx, t, PS_W), BF16),
                   jax.ShapeDtypeStruct((bx, t, PD_W), BF16),
                   jax.ShapeDtypeStruct((bx, 2 * LANE, t), BF16),
                   jax.ShapeDtypeStruct((bx, DIFF_QK_W, t), BF16)],
        compiler_params=_params(2),
        name="inproj",
    )(*args)


GLA_SROWS = (GLA_LEVELS + 2) * GLA_CHUNK
GLA_LV_DIAG = GLA_LEVELS
GLA_LV_NONE = GLA_LEVELS + 1


def _gla_constants():
    c = GLA_CHUNK
    r = np.arange(c)[:, None]
    t = np.arange(c)[None, :]
    blocks = []
    for lv in range(GLA_LEVELS):
        half = 1 << lv
        mid = (r // (2 * half)) * (2 * half) + half
        second = r >= mid
        blocks.append(np.where(second, (t >= mid) & (t <= r), (t > r) & (t < mid)))
    blocks.append(t <= r)
    blocks.append(t > r)
    fwd = np.concatenate(blocks, axis=0).astype(np.float32)
    bwd = np.concatenate([b[::-1, ::-1] for b in blocks], axis=0).astype(np.float32)
    s2 = np.stack([np.concatenate([m, m], axis=1) for m in (fwd, bwd)])
    i = np.arange(c)[:, None]
    j = np.arange(c)[None, :]
    x = np.bitwise_xor(i, j)
    lvl = np.where(j > i, GLA_LV_NONE,
                   np.where(i == j, GLA_LV_DIAG, np.floor(np.log2(np.maximum(x, 1))).astype(np.int64)))
    lv_f = np.tile(lvl, (2, 2))
    lv_b = np.tile(lvl[::-1, ::-1], (2, 2))
    return jnp.asarray(s2, BF16), jnp.asarray(np.stack([lv_f, lv_b]), jnp.int32)


def _gla_kernel(*refs, ctx_out):
    if ctx_out:
        (pgx_ref, pgc_ref, wg_ref, bg_ref, gn_ref, s2_ref, lv_ref, ox_ref, oc_ref,
         oi_ref, qc_ref, upd_ref, dec_ref, st_ref) = refs
    else:
        (pgx_ref, pgc_ref, wg_ref, bg_ref, gn_ref, s2_ref, lv_ref, ox_ref,
         oi_ref, qc_ref, upd_ref, dec_ref, st_ref) = refs
        oc_ref = None
    c = GLA_CHUNK
    grp = GLA_GROUP
    hv = GLA_HEADS * GLA_DV
    n_ctx = pgc_ref.shape[1]
    s_len = pgx_ref.shape[1]
    row = lax.broadcasted_iota(jnp.int32, (hv, GLA_QK_W), 0)
    lane = lax.broadcasted_iota(jnp.int32, (hv, GLA_QK_W), 1)
    head_qk = (row // GLA_DV) == (lane // GLA_DK)
    head_qk_b = head_qk.astype(F32).astype(BF16)
    row2 = lax.broadcasted_iota(jnp.int32, (hv, hv), 0)
    col2 = lax.broadcasted_iota(jnp.int32, (hv, hv), 1)
    head_v_b = ((row2 // GLA_DV) == (col2 // GLA_DV)).astype(F32).astype(BF16)
    gn = gn_ref[...]
    wh, wl = _split_bf16(jnp.concatenate([wg_ref[0], wg_ref[1]], axis=1))
    w3 = jnp.concatenate([wh, wh, wl], axis=0)
    bias = jnp.concatenate([bg_ref[0], bg_ref[1]], axis=1)

    lane1 = lax.broadcasted_iota(jnp.int32, (1, GLA_QK_W), 1)
    pair_mask = [((lane1 // (2 * GLA_DK)) == p).astype(F32).astype(BF16) for p in range(2)]
    parity_mask = [(((lane1 // GLA_DK) % 2) == hh).astype(F32).astype(BF16) for hh in range(2)]
    lane_v = lax.broadcasted_iota(jnp.int32, (1, hv), 1)
    hv_mask = [((lane_v // GLA_DV) == h).astype(F32).astype(BF16) for h in range(GLA_HEADS)]

    def pair_rows(a):
        a = a.astype(BF16)
        return jnp.concatenate([a * pair_mask[0], a * pair_mask[1]], axis=0)

    def parity_rows(a):
        a = a.astype(BF16)
        return jnp.concatenate([a * parity_mask[0], a * parity_mask[1]], axis=0)

    def local(src_ref, src_row, dst_row, with_out):
        blks = [src_ref[0, pl.ds(pl.multiple_of(src_row + g * c, c), c), :] for g in range(grp)]
        qs = [b[:, 0:128] * (GLA_DK ** -0.5) for b in blks]
        ks = [b[:, 128:256] for b in blks]
        vs = [b[:, 256:512].astype(BF16) for b in blks]
        zh, zl = _split_bf16(jnp.concatenate([b[:, 768:896] for b in blks], axis=0))
        z = _dot(jnp.concatenate([zh, zl, zh], axis=1), w3) + bias
        gate = (jnp.minimum(z, 0.0) - jnp.log(1.0 + jnp.exp(-jnp.abs(z)))) * (1.0 / GLA_TAU)
        gh, gl = _split_bf16(gate)
        f_all = []
        for d in range(2):
            cols = slice(d * GLA_QK_W, (d + 1) * GLA_QK_W)
            g2 = jnp.concatenate([jnp.concatenate([gh[g * c:(g + 1) * c, cols], gl[g * c:(g + 1) * c, cols]], axis=0)
                                  for g in range(grp)], axis=1)
            f_all.append(jnp.exp(_dot(s2_ref[d], g2)))

        def fac(d, g, block):
            return f_all[d][block * c:(block + 1) * c, g * GLA_QK_W:(g + 1) * GLA_QK_W]

        if with_out:
            vbds = [[jnp.concatenate([v * hv_mask[2 * p], v * hv_mask[2 * p + 1]], axis=0) for p in range(2)]
                    for v in vs]
            diag = [_dot_nt(pair_rows(q), parity_rows(k)) for q, k in zip(qs, ks)]
        for d in range(2):
            for g in range(grp):
                rows = pl.ds(pl.multiple_of(dst_row + g * c, c), c)
                ci = (dst_row + g * c) // c
                upd = _dot_tn(vs[g], (ks[g] * fac(d, g, GLA_LEVELS + 1)).astype(BF16))
                upd_ref[d, ci] = jnp.where(head_qk, upd, 0.0)
                fcum = fac(d, g, GLA_LEVELS)
                last = c - 1 if d == 0 else 0
                dec_ref[d, ci] = jnp.broadcast_to(fcum[last:last + 1], (8, GLA_QK_W))
                if with_out:
                    qc_ref[d, rows, :] = (qs[g] * fcum).astype(BF16)
            if with_out:
                lv = lv_ref[d]
                atts = [jnp.where(lv == GLA_LV_DIAG, dg, 0.0) for dg in diag]
                for level in range(GLA_LEVELS):
                    for g in range(grp):
                        fl = fac(d, g, level)
                        s = _dot_nt(pair_rows(qs[g] * fl), parity_rows(ks[g] * fl))
                        atts[g] = jnp.where(lv == level, s, atts[g])
                for g in range(grp):
                    rows = pl.ds(pl.multiple_of(dst_row + g * c, c), c)
                    att = atts[g].astype(BF16)
                    oi_ref[d, rows, :] = _dot(att[:c], vbds[g][0]) + _dot(att[c:], vbds[g][1])

    def local_pass(src_ref, dst0, with_out):
        def body(i, carry):
            local(src_ref, i * (grp * c), dst0 + i * (grp * c), with_out)
            return carry
        lax.fori_loop(0, src_ref.shape[1] // (grp * c), body, 0)

    def scan_pass(first, n, with_out):
        def body(i, carry):
            for d in range(2):
                st = st_ref[d]
                for j in range(GLA_SCAN_UNROLL):
                    step = i * GLA_SCAN_UNROLL + j
                    ci = first + (step if d == 0 else n - 1 - step)
                    if with_out:
                        rows = pl.ds(pl.multiple_of(ci * c, c), c)
                        oi_ref[d, rows, :] = oi_ref[d, rows, :] + _dot_nt(qc_ref[d, rows, :], st.astype(BF16))
                    st = dec_ref[d, ci][0:1] * st + upd_ref[d, ci]
                st_ref[d] = st
            return carry
        lax.fori_loop(0, n // GLA_SCAN_UNROLL, body, 0)

    def finish_pass(src_ref, src0, out_ref, n_rows):
        tile = grp * c

        def body(i, carry):
            r = pl.multiple_of(i * tile, tile)
            rs = pl.ds(pl.multiple_of(src0 + r, tile), tile)
            o = oi_ref[0, rs, :] + oi_ref[1, rs, :]
            hi, lo = _split_bf16(o * o)
            ms = (_dot(hi, head_v_b) + _dot(lo, head_v_b)) * (1.0 / GLA_DV)
            og = src_ref[0, pl.ds(r, tile), 512:768]
            out_ref[0, pl.ds(r, tile), :] = (o * lax.rsqrt(ms + RMS_EPS) * gn * _silu(og)).astype(BF16)
            return carry
        lax.fori_loop(0, n_rows // tile, body, 0)

    local_pass(pgc_ref, 0, ctx_out)
    local_pass(pgx_ref, n_ctx, True)
    st_ref[...] = jnp.zeros(st_ref.shape, F32)
    scan_pass(0, n_ctx // c, ctx_out)
    scan_pass(n_ctx // c, s_len // c, True)
    finish_pass(pgx_ref, n_ctx, ox_ref, s_len)
    if ctx_out:
        finish_pass(pgc_ref, 0, oc_ref, n_ctx)


def _gla_call(pgx, pgc, wg, bg, gn, s2, lv, ctx_out):
    b, s, _ = pgx.shape
    n_ctx = pgc.shape[1]
    hv = GLA_WIDTH
    n_rows = s + n_ctx
    in_specs = [pl.BlockSpec((1, s, PG_W), lambda i: (i, 0, 0)),
                pl.BlockSpec((1, n_ctx, PG_W), lambda i: (i, 0, 0)),
                _const_spec(wg.shape, 1), _const_spec(bg.shape, 1), _const_spec((1, hv), 1),
                _const_spec(s2.shape, 1), _const_spec(lv.shape, 1)]
    out_specs = [pl.BlockSpec((1, s, hv), lambda i: (i, 0, 0))]
    out_shape = [jax.ShapeDtypeStruct((b, s, hv), BF16)]
    if ctx_out:
        out_specs.append(pl.BlockSpec((1, n_ctx, hv), lambda i: (i, 0, 0)))
        out_shape.append(jax.ShapeDtypeStruct((b, n_ctx, hv), BF16))
    scratch = [pltpu.VMEM((2, n_rows, hv), F32),
               pltpu.VMEM((2, n_rows, GLA_QK_W), BF16),
               pltpu.VMEM((2, n_rows // GLA_CHUNK, hv, GLA_QK_W), F32),
               pltpu.VMEM((2, n_rows // GLA_CHUNK, 8, GLA_QK_W), F32),
               pltpu.VMEM((2, hv, GLA_QK_W), F32)]
    res = pl.pallas_call(
        functools.partial(_gla_kernel, ctx_out=ctx_out),
        grid=(b,),
        in_specs=in_specs, out_specs=out_specs, out_shape=out_shape, scratch_shapes=scratch,
        compiler_params=_params(1),
        name="gla",
    )(pgx, pgc, wg, bg, gn.reshape(1, hv), s2, lv)
    return (res[0], res[1]) if ctx_out else (res[0], None)


def _swa_kernel(*refs, ctx_out):
    if ctx_out:
        sink_ref, px_ref, ktx_ref, pc_ref, ktc_ref, ox_ref, oc_ref, sc_ref = refs
    else:
        sink_ref, px_ref, ktx_ref, pc_ref, ktc_ref, ox_ref, sc_ref = refs
    s_len = px_ref.shape[1]
    n_ctx = pc_ref.shape[1]
    blk = SWA_BLOCK
    band = 3 * blk
    grp = SWA_HEADS // SWA_KV_HEADS
    lane = lax.broadcasted_iota(jnp.int32, (1, LANE), 1)
    low = lane < SWA_HD
    half_mask = [low.astype(F32).astype(BF16), (~low).astype(F32).astype(BF16)]
    rel = (lax.broadcasted_iota(jnp.int32, (blk, band), 0)
           - lax.broadcasted_iota(jnp.int32, (blk, band), 1))

    def col(i):
        return slice(i * LANE, (i + 1) * LANE)

    def krows(kv):
        return slice(kv * LANE, (kv + 1) * LANE)

    def sink_col(n_rows, kv, half):
        ha = grp * kv + half
        return jnp.concatenate([jnp.full((n_rows, 1), sink_ref[ha], F32),
                                jnp.full((n_rows, 1), sink_ref[ha + 2], F32)], axis=0)

    def softmax_pv(sc, v, snk):
        m = jnp.maximum(snk, jnp.max(sc, axis=-1, keepdims=True))
        p = jnp.exp2(sc - m).astype(BF16)
        r = _dot(p, jnp.concatenate([v, jnp.ones(v.shape, BF16)], axis=1))
        return r[:, :LANE] / (r[:, LANE:] + jnp.exp2(snk - m))

    def pipeline(tasks, depth, carried=(), prefetch=()):
        queue = list(carried)
        for t in range(len(queue), min(depth, len(tasks))):
            queue.append(tasks[t][0]())
        outs = []
        for t, (_, finish) in enumerate(tasks):
            ahead = t + depth
            if ahead < len(tasks):
                queue.append(tasks[ahead][0]())
            elif ahead - len(tasks) < len(prefetch):
                prefetch[ahead - len(tasks)]()
            outs.append(finish(queue[t]))
        return outs

    units = [(kv, half) for kv in range(SWA_KV_HEADS) for half in range(2)]
    blocks_per_step = 2
    n_steps = s_len // (blk * blocks_per_step)
    depth = SWA_PIPE_DEPTH

    def block_tasks(n):
        r0 = pl.multiple_of(n * blk, blk)
        start = pl.multiple_of(jnp.clip((n - 1) * blk, 0, s_len - band), blk)
        tasks = []
        for kv, half in units:
            def scores(kv=kv, half=half):
                bias = jnp.where(jnp.abs(rel + (r0 - start)) <= SWA_WINDOW, 0.0, -jnp.inf)
                bias = jnp.concatenate([bias, jnp.zeros((blk, n_ctx), F32)], axis=1)
                q = jnp.concatenate([px_ref[0, pl.ds(r0, blk), col(2 * kv)],
                                     px_ref[0, pl.ds(r0, blk), col(2 * kv + 1)]], axis=0) * half_mask[half]
                kt = jnp.concatenate([ktx_ref[0, krows(kv), pl.ds(start, band)],
                                      ktc_ref[0, krows(kv), :]], axis=1)
                return _dot(q, kt) + jnp.concatenate([bias, bias], axis=0)

            def finish(cur, kv=kv, half=half):
                v = jnp.concatenate([px_ref[0, pl.ds(start, band), col(4 + kv)],
                                     pc_ref[0, :, col(4 + kv)]], axis=0)
                return softmax_pv(cur, v, sink_col(blk, kv, half))

            tasks.append((scores, finish))
        return tasks, r0

    def prefetch_into(slot, scores):
        def run():
            sc_ref[slot] = scores()
        return run

    def body(i, carry):
        tasks, rows = [], []
        for sub in range(blocks_per_step):
            t, r0 = block_tasks(i * blocks_per_step + sub)
            tasks += t
            rows.append(r0)
        nxt, _ = block_tasks(jnp.minimum(i + 1, n_steps - 1) * blocks_per_step)
        outs = pipeline(tasks, depth, carried=[sc_ref[k] for k in range(depth)],
                        prefetch=[prefetch_into(k, nxt[k][0]) for k in range(depth)])
        for sub in range(blocks_per_step):
            for kv in range(SWA_KV_HEADS):
                t = (sub * SWA_KV_HEADS + kv) * 2
                o2 = jnp.where(low, outs[t], outs[t + 1]).astype(BF16)
                ox_ref[0, pl.ds(rows[sub], blk), col(2 * kv)] = o2[:blk]
                ox_ref[0, pl.ds(rows[sub], blk), col(2 * kv + 1)] = o2[blk:]
        return carry

    first, _ = block_tasks(jnp.int32(0))
    for k in range(depth):
        sc_ref[k] = first[k][0]()
    lax.fori_loop(0, n_steps, body, 0)

    if ctx_out:
        tasks = []
        for kv, half in units:
            def scores(kv=kv, half=half):
                q = jnp.concatenate([pc_ref[0, :, col(2 * kv)], pc_ref[0, :, col(2 * kv + 1)]], axis=0)
                return _dot(q * half_mask[half], ktc_ref[0, krows(kv), :])

            def finish(cur, kv=kv, half=half):
                return softmax_pv(cur, pc_ref[0, :, col(4 + kv)], sink_col(n_ctx, kv, half))

            tasks.append((scores, finish))
        outs = pipeline(tasks, 1)
        for kv in range(SWA_KV_HEADS):
            o2 = jnp.where(low, outs[2 * kv], outs[2 * kv + 1]).astype(BF16)
            oc_ref[0, :, col(2 * kv)] = o2[:n_ctx]
            oc_ref[0, :, col(2 * kv + 1)] = o2[n_ctx:]


def _swa_call(psx, ktx, psc, ktc, sink, ctx_out):
    b, s, _ = psx.shape
    n_ctx = psc.shape[1]
    in_specs = [pl.BlockSpec(memory_space=pltpu.SMEM),
                pl.BlockSpec((1, s, PS_W), lambda i: (i, 0, 0)),
                pl.BlockSpec((1, 2 * LANE, s), lambda i: (i, 0, 0)),
                pl.BlockSpec((1, n_ctx, PS_W), lambda i: (i, 0, 0)),
                pl.BlockSpec((1, 2 * LANE, n_ctx), lambda i: (0, 0, i))]
    out_specs = [pl.BlockSpec((1, s, SWA_WIDTH), lambda i: (i, 0, 0))]
    out_shape = [jax.ShapeDtypeStruct((b, s, SWA_WIDTH), BF16)]
    if ctx_out:
        out_specs.append(pl.BlockSpec((1, n_ctx, SWA_WIDTH), lambda i: (i, 0, 0)))
        out_shape.append(jax.ShapeDtypeStruct((b, n_ctx, SWA_WIDTH), BF16))
    res = pl.pallas_call(
        functools.partial(_swa_kernel, ctx_out=ctx_out),
        grid=(b,),
        in_specs=in_specs, out_specs=out_specs, out_shape=out_shape,
        scratch_shapes=[pltpu.VMEM((SWA_PIPE_DEPTH, 2 * SWA_BLOCK, 3 * SWA_BLOCK + n_ctx), F32)],
        compiler_params=_params(1),
        name="swa",
    )(sink, psx, ktx, psc, ktc)
    return (res[0], res[1]) if ctx_out else (res[0], None)


def _diff_kernel(*refs, ctx_out, lambda_init):
    if ctx_out:
        px_ref, ktx_ref, pc_ref, ktc_ref, lam_ref, gn_ref, ox_ref, oc_ref, kt_ref, va_ref, sc_ref = refs
    else:
        px_ref, ktx_ref, pc_ref, ktc_ref, lam_ref, gn_ref, ox_ref, kt_ref, va_ref, sc_ref = refs
    s_len = px_ref.shape[1]
    w = DIFF_QK_W
    lv = lam_ref[...]
    lam = (jnp.exp(jnp.sum(lv[0:1] * lv[1:2], axis=-1, keepdims=True))
           - jnp.exp(jnp.sum(lv[2:3] * lv[3:4], axis=-1, keepdims=True)) + lambda_init)
    lane = lax.broadcasted_iota(jnp.int32, (1, w), 1)
    row2 = lax.broadcasted_iota(jnp.int32, (w, w), 0)
    col2 = lax.broadcasted_iota(jnp.int32, (w, w), 1)
    head_ones = ((row2 // DIFF_V) == (col2 // DIFF_V)).astype(F32).astype(BF16)
    gn = gn_ref[...] * (1.0 - lambda_init)
    unit_masks = [((lane >= DIFF_QK * u) & (lane < DIFF_QK * (u + 1))).astype(F32).astype(BF16)
                  for u in range(2 * DIFF_HEADS)]
    head_masks = [(lane >= DIFF_V * h) & (lane < DIFF_V * (h + 1)) for h in range(DIFF_HEADS)]

    kt_ref[:, :s_len] = ktx_ref[0]
    kt_ref[:, s_len:] = ktc_ref[0]
    for h in range(DIFF_HEADS):
        keep = head_masks[h].astype(F32).astype(BF16)
        va_ref[h, :s_len, :] = px_ref[0, :, w:2 * w] * keep + (1.0 - keep)
        va_ref[h, s_len:, :] = pc_ref[0, :, w:2 * w] * keep + (1.0 - keep)

    def attend(q, k0, q_next=None):
        def scores(qq, u):
            return _dot(qq * unit_masks[u], kt_ref[:, k0:])

        def softmax_pv(sc, u):
            m = jnp.max(sc, axis=-1, keepdims=True)
            return _dot(jnp.exp2(sc - m).astype(BF16), va_ref[u // 2, k0:, :])

        units = []
        nxt = scores(q, 0) if q_next is None else sc_ref[...]
        for u in range(2 * DIFF_HEADS):
            cur = nxt
            if u + 1 < 2 * DIFF_HEADS:
                nxt = scores(q, u + 1)
            elif q_next is not None:
                sc_ref[...] = scores(q_next, 0)
            units.append(softmax_pv(cur, u))
        o = jnp.zeros((q.shape[0], w), F32)
        for h in range(DIFF_HEADS):
            r1, r2 = units[2 * h], units[2 * h + 1]
            oh = r1 / pltpu.roll(r1, w // 2, 1) - lam * (r2 / pltpu.roll(r2, w // 2, 1))
            o = jnp.where(head_masks[h], oh, o)
        hi, lo = _split_bf16(o * o)
        ms = (_dot(hi, head_ones) + _dot(lo, head_ones)) * (1.0 / DIFF_V)
        return (o * lax.rsqrt(ms + RMS_EPS) * gn).astype(BF16)

    n_blocks = s_len // DIFF_QBLOCK

    def q_block(n):
        return px_ref[0, pl.ds(pl.multiple_of(n * DIFF_QBLOCK, DIFF_QBLOCK), DIFF_QBLOCK), 0:w]

    def body(n, carry):
        out = attend(q_block(n), 0, q_block(jnp.minimum(n + 1, n_blocks - 1)))
        ox_ref[0, pl.ds(pl.multiple_of(n * DIFF_QBLOCK, DIFF_QBLOCK), DIFF_QBLOCK), :] = out
        return carry

    sc_ref[...] = _dot(q_block(0) * unit_masks[0], kt_ref[...])
    lax.fori_loop(0, n_blocks, body, 0)
    if ctx_out:
        oc_ref[0] = attend(pc_ref[0, :, 0:w], s_len)


def _diff_call(pdx, ktx, pdc, ktc, lam, gn, lambda_init, ctx_out):
    b, s, _ = pdx.shape
    n_ctx = pdc.shape[1]
    in_specs = [pl.BlockSpec((1, s, PD_W), lambda i: (i, 0, 0)),
                pl.BlockSpec((1, DIFF_QK_W, s), lambda i: (i, 0, 0)),
                pl.BlockSpec((1, n_ctx, PD_W), lambda i: (i, 0, 0)),
                pl.BlockSpec((1, DIFF_QK_W, n_ctx), lambda i: (0, 0, i)),
                _const_spec(lam.shape, 1), _const_spec((1, DIFF_WIDTH), 1)]
    out_specs = [pl.BlockSpec((1, s, DIFF_WIDTH), lambda i: (i, 0, 0))]
    out_shape = [jax.ShapeDtypeStruct((b, s, DIFF_WIDTH), BF16)]
    if ctx_out:
        out_specs.append(pl.BlockSpec((1, n_ctx, DIFF_WIDTH), lambda i: (i, 0, 0)))
        out_shape.append(jax.ShapeDtypeStruct((b, n_ctx, DIFF_WIDTH), BF16))
    res = pl.pallas_call(
        functools.partial(_diff_kernel, ctx_out=ctx_out, lambda_init=lambda_init),
        grid=(b,),
        in_specs=in_specs, out_specs=out_specs, out_shape=out_shape,
        scratch_shapes=[pltpu.VMEM((DIFF_QK_W, s + n_ctx), BF16),
                        pltpu.VMEM((DIFF_HEADS, s + n_ctx, DIFF_WIDTH), BF16),
                        pltpu.VMEM((DIFF_QBLOCK, s + n_ctx), F32)],
        compiler_params=_params(1),
        name="diff",
    )(pdx, ktx, pdc, ktc, lam, gn.reshape(1, DIFF_WIDTH))
    return (res[0], res[1]) if ctx_out else (res[0], None)


def _prep_w_in(w):
    parts, off = [], 0
    for sz in IN_SIZES:
        parts.append(w[..., off:off + sz].astype(BF16))
        off += sz
    gq, gk, gv, gf, gb, og, sq, sk, sv, dq, dk, dv = parts
    pad = jnp.zeros(w.shape[:-1] + (PG_W - (2 * GLA_QK_W + 2 * GLA_WIDTH + 2 * GLA_GATE_RANK),), BF16)
    v0, v1 = sv[..., :SWA_HD], sv[..., SWA_HD:]
    cols = [gq, gk, gv, og, gf, gb, pad, sq, sk, v0, v0, v1, v1, dq, dk, dv]
    return jnp.concatenate(cols, axis=-1)


def _prep_gate(w_gate, b_gate):
    wg = jnp.zeros((2, LANE, GLA_QK_W), F32)
    for d in range(2):
        wg = wg.at[d, GLA_GATE_RANK * d:GLA_GATE_RANK * (d + 1), :].set(w_gate[d])
    return wg, b_gate.reshape(2, 1, GLA_QK_W)


def _axial_angles(rows, head_dim):
    half = head_dim // 2
    row = jnp.repeat(jnp.arange(rows, dtype=F32), GRID_W)
    col = jnp.tile(jnp.arange(GRID_W, dtype=F32), rows)
    inv_freq = 1.0 / (ROPE_BASE ** (jnp.arange(0, half, 2, dtype=F32) / half))

    def axis_angles(pos):
        a = pos[:, None] * inv_freq[None, :]
        return jnp.concatenate([a, a], axis=-1)

    return jnp.concatenate([axis_angles(row), axis_angles(col)], axis=-1)


def _rope_table(seq):
    rows = seq // GRID_W
    blocks = []
    for head_dim, qscale in ((SWA_HD, SWA_QSCALE), (DIFF_QK, DIFF_QSCALE)):
        ang = _axial_angles(rows, head_dim)
        quarter = head_dim // 4
        sign = jnp.where((jnp.arange(head_dim) % (2 * quarter)) < quarter, -1.0, 1.0).astype(F32)
        reps = LANE // head_dim
        cos = jnp.tile(jnp.cos(ang), (1, reps))
        sin = jnp.tile(jnp.sin(ang) * sign[None, :], (1, reps))
        blocks += [cos * qscale, sin * qscale, cos, sin]
    return jnp.concatenate(blocks, axis=1)


def kernel(x, c, ctx, c_ctx, w_mod, b_mod, g_ffn1, w_ffn1_in, w_ffn1_out, g_mix, w_in, w_out, w_gla_gate,
           b_gla_gate, g_gla_norm, swa_sink, diff_lambda, g_diff_norm, g_ffn2, w_ffn2_in, w_ffn2_out, g_final):
    b, s, d = x.shape
    n_ctx = ctx.shape[1]
    depth = w_mod.shape[0]

    cs = jnp.zeros((MOD_ROWS, d), F32).at[:b].set(c).at[b].set(c_ctx)
    mod = _mod_call(cs, w_mod, b_mod).reshape(depth, MOD_ROWS, N_MOD, d)
    tab = _rope_table(s)
    s2, lv = _gla_constants()

    w1i, w1o = w_ffn1_in.astype(BF16), w_ffn1_out.astype(BF16)
    w2i, w2o = w_ffn2_in.astype(BF16), w_ffn2_out.astype(BF16)
    wi = _prep_w_in(w_in)
    wo = w_out.astype(BF16)

    hx = x
    hc = ctx.reshape(1, b * n_ctx, d)
    for l in range(depth):
        ctx_out = l < depth - 1
        last = l == depth - 1
        mod_x = mod[l, :b]
        mod_c = mod[l, b:b + 1]
        lambda_init = 0.8 - 0.6 * math.exp(-0.3 * l)
        wg, bg = _prep_gate(w_gla_gate[l], b_gla_gate[l])
        sink = swa_sink[l] * LOG2E

        hx = _ffn_call(hx, mod_x, g_ffn1[l], w1i, w1o, l, 0)
        hc = _ffn_call(hc, mod_c, g_ffn1[l], w1i, w1o, l, 0)
        pgx, psx, pdx, ksx, ktx = _inproj_call(hx, mod_x, g_mix[l], wi, l, tab)
        pgc, psc, pdc, ksc, ktc = _inproj_call(hc, mod_c, g_mix[l], wi, l)
        pgc, psc, pdc = (a.reshape(b, n_ctx, a.shape[-1]) for a in (pgc, psc, pdc))

        gla_x, gla_c = _gla_call(pgx, pgc, wg, bg, g_gla_norm[l], s2, lv, ctx_out)
        swa_x, swa_c = _swa_call(psx, ksx, psc, ksc, sink, ctx_out)
        dif_x, dif_c = _diff_call(pdx, ktx, pdc, ktc, diff_lambda[l], g_diff_norm[l], lambda_init, ctx_out)

        hx = _ffn_call(hx, mod_x, g_ffn2[l], w2i, w2o, l, 6, mix=(gla_x, swa_x, dif_x, wo),
                       g_final=g_final if last else None)
        if ctx_out:
            flat = lambda a: a.reshape(1, b * n_ctx, a.shape[-1])
            hc = _ffn_call(hc, mod_c, g_ffn2[l], w2i, w2o, l, 6, mix=(flat(gla_c), flat(swa_c), flat(dif_c), wo))
    return hx
```

```python
import functools
import math

import numpy as np
import jax
import jax.numpy as jnp
from jax import lax
from jax.experimental import pallas as pl
from jax.experimental.pallas import tpu as pltpu

F32 = jnp.float32
BF16 = jnp.bfloat16

D_MODEL = 1024
DEPTH = 2
GRID_W = 64
N_MOD = 9
D_FF = 2816
RMS_EPS = 1e-6
ROPE_BASE = 10000.0

GLA_HEADS = 4
GLA_DK = 32
GLA_DV = 64
GLA_GATE_RANK = 16
GLA_TAU = 16.0
SWA_HEADS = 8
SWA_KV_HEADS = 2
SWA_HD = 64
SWA_WINDOW = 128
SWA_BLOCK = 128
DIFF_HEADS = 4
DIFF_QK = 32
DIFF_V = 64
DIFF_QBLOCK = 512

GLA_QK_W = GLA_HEADS * GLA_DK
GLA_WIDTH = GLA_HEADS * GLA_DV
SWA_WIDTH = SWA_HEADS * SWA_HD
SWA_KV_W = SWA_KV_HEADS * SWA_HD
DIFF_QK_W = DIFF_HEADS * 2 * DIFF_QK
DIFF_WIDTH = DIFF_HEADS * DIFF_V
MIX_WIDTH = GLA_WIDTH + SWA_WIDTH + DIFF_WIDTH
IN_SIZES = (GLA_QK_W, GLA_QK_W, GLA_WIDTH, GLA_GATE_RANK, GLA_GATE_RANK, GLA_WIDTH,
            SWA_WIDTH, SWA_KV_W, SWA_KV_W, DIFF_QK_W, DIFF_QK_W, DIFF_WIDTH)

LANE = 128
VMEM_LIMIT = 56 * 1024 * 1024
LOG2E = math.log2(math.e)

PG_W = 896
PS_COLS = 896
PS_W = 768
PD_COLS = 768
PD_W = 512
P_W = PG_W + PS_COLS + PD_COLS
GLA_CHUNK = 64
GLA_LEVELS = 6
GLA_GROUP = 4
GLA_GROUP_X = 8
GLA_SCAN_UNROLL = 4
SWA_PIPE_DEPTH = 2
MOD_ROWS = 16
MOD_TN = 1152
FFN_TM = 512
FFN_TF = D_FF // 2


def _dot(a, b):
    return jnp.dot(a, b, preferred_element_type=F32)


def _dot_nt(a, b):
    return lax.dot_general(a, b, (((1,), (1,)), ((), ())), preferred_element_type=F32)


def _dot_tn(a, b):
    return lax.dot_general(a, b, (((0,), (0,)), ((), ())), preferred_element_type=F32)


def _split_bf16(x):
    hi = x.astype(BF16)
    lo = (x - hi.astype(F32)).astype(BF16)
    return hi, lo


def _rms(x, g):
    return x * lax.rsqrt(jnp.mean(x * x, axis=-1, keepdims=True) + RMS_EPS) * g


def _silu(x):
    return x * (1.0 / (1.0 + jnp.exp(-x)))


def _params(n_grid):
    return pltpu.CompilerParams(dimension_semantics=("arbitrary",) * n_grid, vmem_limit_bytes=VMEM_LIMIT)


def _const_spec(shape, n_grid, single=False):
    zeros = (0,) * len(shape)
    index_map = {1: lambda a: zeros, 2: lambda a, b: zeros}[n_grid]
    if single:
        return pl.BlockSpec(shape, index_map, pipeline_mode=pl.Buffered(1))
    return pl.BlockSpec(shape, index_map)


def _layer_spec(shape, layer):
    return pl.BlockSpec((None,) + tuple(shape), lambda b, i: (layer, 0, 0), pipeline_mode=pl.Buffered(1))


def _mod_kernel(a_ref, w_ref, b_ref, o_ref):
    a = _silu(a_ref[...]).astype(BF16)
    o_ref[0] = _dot(a, w_ref[0].astype(BF16)) + b_ref[0]


def _mod_call(cs, w_mod, b_mod):
    n_layers, d, n = w_mod.shape
    return pl.pallas_call(
        _mod_kernel,
        grid=(n_layers, n // MOD_TN),
        in_specs=[pl.BlockSpec((MOD_ROWS, d), lambda l, j: (0, 0)),
                  pl.BlockSpec((1, d, MOD_TN), lambda l, j: (l, 0, j)),
                  pl.BlockSpec((1, 1, MOD_TN), lambda l, j: (l, 0, j))],
        out_specs=pl.BlockSpec((1, MOD_ROWS, MOD_TN), lambda l, j: (l, 0, j)),
        out_shape=jax.ShapeDtypeStruct((n_layers, MOD_ROWS, n), F32),
        compiler_params=_params(2),
        name="mod",
    )(cs, w_mod, b_mod.reshape(n_layers, 1, n))


def _ffn_kernel(*refs, mod_off, mix, final):
    refs = list(refs)
    h_ref, mod_ref, g_ref, win_ref, wout_ref = refs[:5]
    o_ref = refs[-1]
    mod = mod_ref[0]
    if mix:
        yg_ref, ys_ref, yd_ref, wmix_ref = refs[5:9]
    if final:
        gfin_ref = refs[-2]
    shift, scale, gate = mod[mod_off:mod_off + 1], mod[mod_off + 1:mod_off + 2], mod[mod_off + 2:mod_off + 3]
    rows = h_ref.shape[1] // 2
    halves = [slice(0, rows), slice(rows, 2 * rows)]
    n_chunks = D_FF // FFN_TF

    def prologue(rs):
        x = h_ref[0, rs, :]
        if mix:
            a = GLA_WIDTH
            b = GLA_WIDTH + SWA_WIDTH
            proj = (_dot(yg_ref[0, rs, :], wmix_ref[0:a, :]) + _dot(ys_ref[0, rs, :], wmix_ref[a:b, :])
                    + _dot(yd_ref[0, rs, :], wmix_ref[b:, :]))
            x = x + mod[5:6] * proj
        return x, (_rms(x, g_ref[...]) * (1.0 + scale) + shift).astype(BF16)

    def chunk(y, c):
        gt = _dot(y, win_ref[:, c * FFN_TF:(c + 1) * FFN_TF])
        up = _dot(y, win_ref[:, D_FF + c * FFN_TF:D_FF + (c + 1) * FFN_TF])
        act = (_silu(gt) * up).astype(BF16)
        return _dot(act, wout_ref[c * FFN_TF:(c + 1) * FFN_TF, :])

    def epilogue(rs, x, acc):
        out = x + (0.5 * gate) * acc
        if final:
            out = _rms(out, gfin_ref[...])
        o_ref[0, rs, :] = out

    xa, ya = prologue(halves[0])
    acc_a = chunk(ya, 0)
    xb, yb = prologue(halves[1])
    for c in range(1, n_chunks):
        acc_a = acc_a + chunk(ya, c)
    acc_b = chunk(yb, 0)
    epilogue(halves[0], xa, acc_a)
    for c in range(1, n_chunks):
        acc_b = acc_b + chunk(yb, c)
    epilogue(halves[1], xb, acc_b)


def _ffn_call(h, mod, g, w_in, w_out, layer, mod_off, mix=None, g_final=None):
    bx, t, d = h.shape
    tm = min(FFN_TM, t)
    final = g_final is not None
    tile = lambda width: pl.BlockSpec((1, tm, width), lambda b, i: (b, i, 0))
    in_specs = [tile(d),
                pl.BlockSpec((1, N_MOD, d), lambda b, i: (b, 0, 0)),
                _const_spec((1, d), 2),
                _layer_spec((d, 2 * D_FF), layer),
                _layer_spec((D_FF, d), layer)]
    args = [h, mod, g.reshape(1, d), w_in, w_out]
    if mix is not None:
        in_specs += [tile(GLA_WIDTH), tile(SWA_WIDTH), tile(DIFF_WIDTH), _layer_spec((MIX_WIDTH, d), layer)]
        args += list(mix)
    if final:
        in_specs.append(_const_spec((1, d), 2))
        args.append(g_final.reshape(1, d))
    return pl.pallas_call(
        functools.partial(_ffn_kernel, mod_off=mod_off, mix=mix is not None, final=final),
        grid=(bx, t // tm),
        in_specs=in_specs,
        out_specs=pl.BlockSpec((1, tm, d), lambda b, i: (b, i, 0)),
        out_shape=jax.ShapeDtypeStruct(h.shape, F32),
        compiler_params=_params(2),
        name="ffn",
    )(*args)


SWA_QSCALE = SWA_HD ** -0.5 * LOG2E
DIFF_QSCALE = DIFF_QK ** -0.5 * LOG2E


def _rot_half(blk, quarter, first):
    return jnp.where(first, pltpu.roll(blk, LANE - quarter, 1), pltpu.roll(blk, quarter, 1))


def _inproj_kernel(*refs, rope):
    if rope:
        h_ref, mod_ref, g_ref, w_ref, tab_ref, pg_ref, ps_ref, pd_ref, kts_ref, kt_ref = refs
    else:
        h_ref, mod_ref, g_ref, w_ref, pg_ref, ps_ref, pd_ref, kts_ref, kt_ref = refs
    x = h_ref[0]
    mod = mod_ref[0]
    y = (_rms(x, g_ref[...]) * (1.0 + mod[4:5]) + mod[3:4]).astype(BF16)
    p = _dot(y, w_ref[...])
    pg_ref[0] = p[:, :PG_W]
    swa = [p[:, PG_W + i * LANE:PG_W + (i + 1) * LANE] for i in range(PS_COLS // LANE)]
    dif = [p[:, PG_W + PS_COLS + i * LANE:PG_W + PS_COLS + (i + 1) * LANE] for i in range(PD_COLS // LANE)]
    if rope:
        tab = [tab_ref[:, i * LANE:(i + 1) * LANE] for i in range(8)]
        lane = lax.broadcasted_iota(jnp.int32, (1, LANE), 1)
        first_s = (lane % (SWA_HD // 2)) < (SWA_HD // 4)
        first_d = (lane % (DIFF_QK // 2)) < (DIFF_QK // 4)
        for i in range(5):
            c, s = (tab[0], tab[1]) if i < 4 else (tab[2], tab[3])
            swa[i] = swa[i] * c + _rot_half(swa[i], SWA_HD // 4, first_s) * s
        for i in range(4):
            c, s = (tab[4], tab[5]) if i < 2 else (tab[6], tab[7])
            dif[i] = dif[i] * c + _rot_half(dif[i], DIFF_QK // 4, first_d) * s
    else:
        for i in range(4):
            swa[i] = swa[i] * SWA_QSCALE
        for i in range(2):
            dif[i] = dif[i] * DIFF_QSCALE
    ps_ref[0] = jnp.concatenate(swa[0:4] + swa[5:7], axis=1).astype(BF16)
    kt = swa[4].T
    k0, k1 = kt[:SWA_HD], kt[SWA_HD:]
    kts_ref[0] = jnp.concatenate([k0, k0, k1, k1], axis=0).astype(BF16)
    pd_ref[0] = jnp.concatenate(dif[0:2] + dif[4:6], axis=1).astype(BF16)
    kt_ref[0] = jnp.concatenate(dif[2:4], axis=1).T.astype(BF16)


def _inproj_call(h, mod, g, w, layer, tab=None):
    bx, t, d = h.shape
    tm = min(FFN_TM, t)
    rope = tab is not None
    in_specs = [pl.BlockSpec((1, tm, d), lambda j, b: (b, j, 0)),
                pl.BlockSpec((1, N_MOD, d), lambda j, b: (b, 0, 0)),
                _const_spec((1, d), 2),
                _layer_spec((d, P_W), layer)]
    args = [h, mod, g.reshape(1, d), w]
    if rope:
        in_specs.append(pl.BlockSpec((tm, 8 * LANE), lambda j, b: (j, 0)))
        args.append(tab)
    out_map = lambda j, b: (b, j, 0)
    return pl.pallas_call(
        functools.partial(_inproj_kernel, rope=rope),
        grid=(t // tm, bx),
        in_specs=in_specs,
        out_specs=[pl.BlockSpec((1, tm, PG_W), out_map), pl.BlockSpec((1, tm, PS_W), out_map),
                   pl.BlockSpec((1, tm, PD_W), out_map),
                   pl.BlockSpec((1, 2 * LANE, tm), lambda j, b: (b, 0, j)),
                   pl.BlockSpec((1, DIFF_QK_W, tm), lambda j, b: (b, 0, j))],
        out_shape=[jax.ShapeDtypeStruct((bx, t, PG_W), F32), jax.ShapeDtypeStruct((bx, t, PS_W), BF16),
                   jax.ShapeDtypeStruct((bx, t, PD_W), BF16),
                   jax.ShapeDtypeStruct((bx, 2 * LANE, t), BF16),
                   jax.ShapeDtypeStruct((bx, DIFF_QK_W, t), BF16)],
        compiler_params=_params(2),
        name="inproj",
    )(*args)


GLA_SROWS = (GLA_LEVELS + 2) * GLA_CHUNK
GLA_LV_DIAG = GLA_LEVELS
GLA_LV_NONE = GLA_LEVELS + 1


def _gla_constants():
    c = GLA_CHUNK
    r = np.arange(c)[:, None]
    t = np.arange(c)[None, :]
    blocks = []
    for lv in range(GLA_LEVELS):
        half = 1 << lv
        mid = (r // (2 * half)) * (2 * half) + half
        second = r >= mid
        blocks.append(np.where(second, (t >= mid) & (t <= r), (t > r) & (t < mid)))
    blocks.append(t <= r)
    blocks.append(t > r)
    fwd = np.concatenate(blocks, axis=0).astype(np.float32)
    bwd = np.concatenate([b[::-1, ::-1] for b in blocks], axis=0).astype(np.float32)
    s2 = np.stack([np.concatenate([m, m], axis=1) for m in (fwd, bwd)])
    i = np.arange(c)[:, None]
    j = np.arange(c)[None, :]
    x = np.bitwise_xor(i, j)
    lvl = np.where(j > i, GLA_LV_NONE,
                   np.where(i == j, GLA_LV_DIAG, np.floor(np.log2(np.maximum(x, 1))).astype(np.int64)))
    lv_f = np.tile(lvl, (2, 2))
    lv_b = np.tile(lvl[::-1, ::-1], (2, 2))
    return jnp.asarray(s2, BF16), jnp.asarray(np.stack([lv_f, lv_b]), jnp.int32)


def _gla_kernel(*refs, ctx_out):
    if ctx_out:
        (pgx_ref, pgc_ref, wg_ref, bg_ref, gn_ref, s2_ref, lv_ref, ox_ref, oc_ref,
         oi_ref, qc_ref, upd_ref, dec_ref, st_ref) = refs
    else:
        (pgx_ref, pgc_ref, wg_ref, bg_ref, gn_ref, s2_ref, lv_ref, ox_ref,
         oi_ref, qc_ref, upd_ref, dec_ref, st_ref) = refs
        oc_ref = None
    c = GLA_CHUNK
    grp = GLA_GROUP
    hv = GLA_HEADS * GLA_DV
    n_ctx = pgc_ref.shape[1]
    s_len = pgx_ref.shape[1]
    row = lax.broadcasted_iota(jnp.int32, (hv, GLA_QK_W), 0)
    lane = lax.broadcasted_iota(jnp.int32, (hv, GLA_QK_W), 1)
    head_qk = (row // GLA_DV) == (lane // GLA_DK)
    head_qk_b = head_qk.astype(F32).astype(BF16)
    row2 = lax.broadcasted_iota(jnp.int32, (hv, hv), 0)
    col2 = lax.broadcasted_iota(jnp.int32, (hv, hv), 1)
    head_v_b = ((row2 // GLA_DV) == (col2 // GLA_DV)).astype(F32).astype(BF16)
    gn = gn_ref[...]
    wh, wl = _split_bf16(jnp.concatenate([wg_ref[0], wg_ref[1]], axis=1))
    w3 = jnp.concatenate([wh, wh, wl], axis=0)
    bias = jnp.concatenate([bg_ref[0], bg_ref[1]], axis=1)

    lane1 = lax.broadcasted_iota(jnp.int32, (1, GLA_QK_W), 1)
    pair_mask = [((lane1 // (2 * GLA_DK)) == p).astype(F32).astype(BF16) for p in range(2)]
    parity_mask = [(((lane1 // GLA_DK) % 2) == hh).astype(F32).astype(BF16) for hh in range(2)]
    lane_v = lax.broadcasted_iota(jnp.int32, (1, hv), 1)
    hv_mask = [((lane_v // GLA_DV) == h).astype(F32).astype(BF16) for h in range(GLA_HEADS)]

    def pair_rows(a):
        a = a.astype(BF16)
        return jnp.concatenate([a * pair_mask[0], a * pair_mask[1]], axis=0)

    def parity_rows(a):
        a = a.astype(BF16)
        return jnp.concatenate([a * parity_mask[0], a * parity_mask[1]], axis=0)

    def local(src_ref, src_row, dst_row, with_out, grp):
        blks = [src_ref[0, pl.ds(pl.multiple_of(src_row + g * c, c), c), :] for g in range(grp)]
        qs = [b[:, 0:128] * (GLA_DK ** -0.5) for b in blks]
        ks = [b[:, 128:256] for b in blks]
        vs = [b[:, 256:512].astype(BF16) for b in blks]
        zh, zl = _split_bf16(jnp.concatenate([b[:, 768:896] for b in blks], axis=0))
        z = _dot(jnp.concatenate([zh, zl, zh], axis=1), w3) + bias
        gate = (jnp.minimum(z, 0.0) - jnp.log(1.0 + jnp.exp(-jnp.abs(z)))) * (1.0 / GLA_TAU)
        gh, gl = _split_bf16(gate)
        f_all = []
        for d in range(2):
            cols = slice(d * GLA_QK_W, (d + 1) * GLA_QK_W)
            g2 = jnp.concatenate([jnp.concatenate([gh[g * c:(g + 1) * c, cols], gl[g * c:(g + 1) * c, cols]], axis=0)
                                  for g in range(grp)], axis=1)
            f_all.append(jnp.exp(_dot(s2_ref[d], g2)))

        def fac(d, g, block):
            return f_all[d][block * c:(block + 1) * c, g * GLA_QK_W:(g + 1) * GLA_QK_W]

        if with_out:
            vbds = [[jnp.concatenate([v * hv_mask[2 * p], v * hv_mask[2 * p + 1]], axis=0) for p in range(2)]
                    for v in vs]
            diag = [_dot_nt(pair_rows(q), parity_rows(k)) for q, k in zip(qs, ks)]
        for d in range(2):
            for g in range(grp):
                rows = pl.ds(pl.multiple_of(dst_row + g * c, c), c)
                ci = (dst_row + g * c) // c
                upd = _dot_tn(vs[g], (ks[g] * fac(d, g, GLA_LEVELS + 1)).astype(BF16))
                upd_ref[d, ci] = jnp.where(head_qk, upd, 0.0)
                fcum = fac(d, g, GLA_LEVELS)
                last = c - 1 if d == 0 else 0
                dec_ref[d, ci] = jnp.broadcast_to(fcum[last:last + 1], (8, GLA_QK_W))
                if with_out:
                    qc_ref[d, rows, :] = (qs[g] * fcum).astype(BF16)
            if with_out:
                lv = lv_ref[d]
                atts = [jnp.where(lv == GLA_LV_DIAG, dg, 0.0) for dg in diag]
                for level in range(GLA_LEVELS):
                    for g in range(grp):
                        fl = fac(d, g, level)
                        s = _dot_nt(pair_rows(qs[g] * fl), parity_rows(ks[g] * fl))
                        atts[g] = jnp.where(lv == level, s, atts[g])
                for g in range(grp):
                    rows = pl.ds(pl.multiple_of(dst_row + g * c, c), c)
                    att = atts[g].astype(BF16)
                    oi_ref[d, rows, :] = _dot(att[:c], vbds[g][0]) + _dot(att[c:], vbds[g][1])

    def local_pass(src_ref, dst0, with_out, grp):
        def body(i, carry):
            local(src_ref, i * (grp * c), dst0 + i * (grp * c), with_out, grp)
            return carry
        lax.fori_loop(0, src_ref.shape[1] // (grp * c), body, 0)

    def scan_pass(first, n, with_out):
        def body(i, carry):
            for d in range(2):
                st = st_ref[d]
                for j in range(GLA_SCAN_UNROLL):
                    step = i * GLA_SCAN_UNROLL + j
                    ci = first + (step if d == 0 else n - 1 - step)
                    if with_out:
                        rows = pl.ds(pl.multiple_of(ci * c, c), c)
                        oi_ref[d, rows, :] = oi_ref[d, rows, :] + _dot_nt(qc_ref[d, rows, :], st.astype(BF16))
                    st = dec_ref[d, ci][0:1] * st + upd_ref[d, ci]
                st_ref[d] = st
            return carry
        lax.fori_loop(0, n // GLA_SCAN_UNROLL, body, 0)

    def finish_pass(src_ref, src0, out_ref, n_rows):
        tile = grp * c

        def body(i, carry):
            r = pl.multiple_of(i * tile, tile)
            rs = pl.ds(pl.multiple_of(src0 + r, tile), tile)
            o = oi_ref[0, rs, :] + oi_ref[1, rs, :]
            hi, lo = _split_bf16(o * o)
            ms = (_dot(hi, head_v_b) + _dot(lo, head_v_b)) * (1.0 / GLA_DV)
            og = src_ref[0, pl.ds(r, tile), 512:768]
            out_ref[0, pl.ds(r, tile), :] = (o * lax.rsqrt(ms + RMS_EPS) * gn * _silu(og)).astype(BF16)
            return carry
        lax.fori_loop(0, n_rows // tile, body, 0)

    local_pass(pgc_ref, 0, ctx_out, GLA_GROUP)
    local_pass(pgx_ref, n_ctx, True, GLA_GROUP_X)
    st_ref[...] = jnp.zeros(st_ref.shape, F32)
    scan_pass(0, n_ctx // c, ctx_out)
    scan_pass(n_ctx // c, s_len // c, True)
    finish_pass(pgx_ref, n_ctx, ox_ref, s_len)
    if ctx_out:
        finish_pass(pgc_ref, 0, oc_ref, n_ctx)


def _gla_call(pgx, pgc, wg, bg, gn, s2, lv, ctx_out):
    b, s, _ = pgx.shape
    n_ctx = pgc.shape[1]
    hv = GLA_WIDTH
    n_rows = s + n_ctx
    in_specs = [pl.BlockSpec((1, s, PG_W), lambda i: (i, 0, 0)),
                pl.BlockSpec((1, n_ctx, PG_W), lambda i: (i, 0, 0)),
                _const_spec(wg.shape, 1), _const_spec(bg.shape, 1), _const_spec((1, hv), 1),
                _const_spec(s2.shape, 1), _const_spec(lv.shape, 1)]
    out_specs = [pl.BlockSpec((1, s, hv), lambda i: (i, 0, 0))]
    out_shape = [jax.ShapeDtypeStruct((b, s, hv), BF16)]
    if ctx_out:
        out_specs.append(pl.BlockSpec((1, n_ctx, hv), lambda i: (i, 0, 0)))
        out_shape.append(jax.ShapeDtypeStruct((b, n_ctx, hv), BF16))
    scratch = [pltpu.VMEM((2, n_rows, hv), F32),
               pltpu.VMEM((2, n_rows, GLA_QK_W), BF16),
               pltpu.VMEM((2, n_rows // GLA_CHUNK, hv, GLA_QK_W), F32),
               pltpu.VMEM((2, n_rows // GLA_CHUNK, 8, GLA_QK_W), F32),
               pltpu.VMEM((2, hv, GLA_QK_W), F32)]
    res = pl.pallas_call(
        functools.partial(_gla_kernel, ctx_out=ctx_out),
        grid=(b,),
        in_specs=in_specs, out_specs=out_specs, out_shape=out_shape, scratch_shapes=scratch,
        compiler_params=_params(1),
        name="gla",
    )(pgx, pgc, wg, bg, gn.reshape(1, hv), s2, lv)
    return (res[0], res[1]) if ctx_out else (res[0], None)


def _swa_kernel(*refs, ctx_out):
    if ctx_out:
        sink_ref, px_ref, ktx_ref, pc_ref, ktc_ref, ox_ref, oc_ref, sc_ref = refs
    else:
        sink_ref, px_ref, ktx_ref, pc_ref, ktc_ref, ox_ref, sc_ref = refs
    s_len = px_ref.shape[1]
    n_ctx = pc_ref.shape[1]
    blk = SWA_BLOCK
    band = 3 * blk
    grp = SWA_HEADS // SWA_KV_HEADS
    lane = lax.broadcasted_iota(jnp.int32, (1, LANE), 1)
    low = lane < SWA_HD
    half_mask = [low.astype(F32).astype(BF16), (~low).astype(F32).astype(BF16)]
    rel = (lax.broadcasted_iota(jnp.int32, (blk, band), 0)
           - lax.broadcasted_iota(jnp.int32, (blk, band), 1))

    def col(i):
        return slice(i * LANE, (i + 1) * LANE)

    def krows(kv):
        return slice(kv * LANE, (kv + 1) * LANE)

    def sink_col(n_rows, kv, half):
        ha = grp * kv + half
        return jnp.concatenate([jnp.full((n_rows, 1), sink_ref[ha], F32),
                                jnp.full((n_rows, 1), sink_ref[ha + 2], F32)], axis=0)

    def softmax_pv(sc, v, snk):
        m = jnp.maximum(snk, jnp.max(sc, axis=-1, keepdims=True))
        p = jnp.exp2(sc - m).astype(BF16)
        r = _dot(p, jnp.concatenate([v, jnp.ones(v.shape, BF16)], axis=1))
        return r[:, :LANE] / (r[:, LANE:] + jnp.exp2(snk - m))

    def pipeline(tasks, depth, carried=(), prefetch=()):
        queue = list(carried)
        for t in range(len(queue), min(depth, len(tasks))):
            queue.append(tasks[t][0]())
        outs = []
        for t, (_, finish) in enumerate(tasks):
            ahead = t + depth
            if ahead < len(tasks):
                queue.append(tasks[ahead][0]())
            elif ahead - len(tasks) < len(prefetch):
                prefetch[ahead - len(tasks)]()
            outs.append(finish(queue[t]))
        return outs

    units = [(kv, half) for kv in range(SWA_KV_HEADS) for half in range(2)]
    blocks_per_step = 2
    n_steps = s_len // (blk * blocks_per_step)
    depth = SWA_PIPE_DEPTH

    def block_tasks(n):
        r0 = pl.multiple_of(n * blk, blk)
        start = pl.multiple_of(jnp.clip((n - 1) * blk, 0, s_len - band), blk)
        tasks = []
        for kv, half in units:
            def scores(kv=kv, half=half):
                bias = jnp.where(jnp.abs(rel + (r0 - start)) <= SWA_WINDOW, 0.0, -jnp.inf)
                bias = jnp.concatenate([bias, jnp.zeros((blk, n_ctx), F32)], axis=1)
                q = jnp.concatenate([px_ref[0, pl.ds(r0, blk), col(2 * kv)],
                                     px_ref[0, pl.ds(r0, blk), col(2 * kv + 1)]], axis=0) * half_mask[half]
                kt = jnp.concatenate([ktx_ref[0, krows(kv), pl.ds(start, band)],
                                      ktc_ref[0, krows(kv), :]], axis=1)
                return _dot(q, kt) + jnp.concatenate([bias, bias], axis=0)

            def finish(cur, kv=kv, half=half):
                v = jnp.concatenate([px_ref[0, pl.ds(start, band), col(4 + kv)],
                                     pc_ref[0, :, col(4 + kv)]], axis=0)
                return softmax_pv(cur, v, sink_col(blk, kv, half))

            tasks.append((scores, finish))
        return tasks, r0

    def prefetch_into(slot, scores):
        def run():
            sc_ref[slot] = scores()
        return run

    def body(i, carry):
        tasks, rows = [], []
        for sub in range(blocks_per_step):
            t, r0 = block_tasks(i * blocks_per_step + sub)
            tasks += t
            rows.append(r0)
        nxt, _ = block_tasks(jnp.minimum(i + 1, n_steps - 1) * blocks_per_step)
        outs = pipeline(tasks, depth, carried=[sc_ref[k] for k in range(depth)],
                        prefetch=[prefetch_into(k, nxt[k][0]) for k in range(depth)])
        for sub in range(blocks_per_step):
            for kv in range(SWA_KV_HEADS):
                t = (sub * SWA_KV_HEADS + kv) * 2
                o2 = jnp.where(low, outs[t], outs[t + 1]).astype(BF16)
                ox_ref[0, pl.ds(rows[sub], blk), col(2 * kv)] = o2[:blk]
                ox_ref[0, pl.ds(rows[sub], blk), col(2 * kv + 1)] = o2[blk:]
        return carry

    first, _ = block_tasks(jnp.int32(0))
    for k in range(depth):
        sc_ref[k] = first[k][0]()
    lax.fori_loop(0, n_steps, body, 0)

    if ctx_out:
        tasks = []
        for kv, half in units:
            def scores(kv=kv, half=half):
                q = jnp.concatenate([pc_ref[0, :, col(2 * kv)], pc_ref[0, :, col(2 * kv + 1)]], axis=0)
                return _dot(q * half_mask[half], ktc_ref[0, krows(kv), :])

            def finish(cur, kv=kv, half=half):
                return softmax_pv(cur, pc_ref[0, :, col(4 + kv)], sink_col(n_ctx, kv, half))

            tasks.append((scores, finish))
        outs = pipeline(tasks, 1)
        for kv in range(SWA_KV_HEADS):
            o2 = jnp.where(low, outs[2 * kv], outs[2 * kv + 1]).astype(BF16)
            oc_ref[0, :, col(2 * kv)] = o2[:n_ctx]
            oc_ref[0, :, col(2 * kv + 1)] = o2[n_ctx:]


def _swa_call(psx, ktx, psc, ktc, sink, ctx_out):
    b, s, _ = psx.shape
    n_ctx = psc.shape[1]
    in_specs = [pl.BlockSpec(memory_space=pltpu.SMEM),
                pl.BlockSpec((1, s, PS_W), lambda i: (i, 0, 0)),
                pl.BlockSpec((1, 2 * LANE, s), lambda i: (i, 0, 0)),
                pl.BlockSpec((1, n_ctx, PS_W), lambda i: (i, 0, 0)),
                pl.BlockSpec((1, 2 * LANE, n_ctx), lambda i: (0, 0, i))]
    out_specs = [pl.BlockSpec((1, s, SWA_WIDTH), lambda i: (i, 0, 0))]
    out_shape = [jax.ShapeDtypeStruct((b, s, SWA_WIDTH), BF16)]
    if ctx_out:
        out_specs.append(pl.BlockSpec((1, n_ctx, SWA_WIDTH), lambda i: (i, 0, 0)))
        out_shape.append(jax.ShapeDtypeStruct((b, n_ctx, SWA_WIDTH), BF16))
    res = pl.pallas_call(
        functools.partial(_swa_kernel, ctx_out=ctx_out),
        grid=(b,),
        in_specs=in_specs, out_specs=out_specs, out_shape=out_shape,
        scratch_shapes=[pltpu.VMEM((SWA_PIPE_DEPTH, 2 * SWA_BLOCK, 3 * SWA_BLOCK + n_ctx), F32)],
        compiler_params=_params(1),
        name="swa",
    )(sink, psx, ktx, psc, ktc)
    return (res[0], res[1]) if ctx_out else (res[0], None)


def _diff_kernel(*refs, ctx_out, lambda_init):
    if ctx_out:
        px_ref, ktx_ref, pc_ref, ktc_ref, lam_ref, gn_ref, ox_ref, oc_ref, kt_ref, va_ref, sc_ref = refs
    else:
        px_ref, ktx_ref, pc_ref, ktc_ref, lam_ref, gn_ref, ox_ref, kt_ref, va_ref, sc_ref = refs
    s_len = px_ref.shape[1]
    w = DIFF_QK_W
    lv = lam_ref[...]
    lam = (jnp.exp(jnp.sum(lv[0:1] * lv[1:2], axis=-1, keepdims=True))
           - jnp.exp(jnp.sum(lv[2:3] * lv[3:4], axis=-1, keepdims=True)) + lambda_init)
    lane = lax.broadcasted_iota(jnp.int32, (1, w), 1)
    row2 = lax.broadcasted_iota(jnp.int32, (w, w), 0)
    col2 = lax.broadcasted_iota(jnp.int32, (w, w), 1)
    head_ones = ((row2 // DIFF_V) == (col2 // DIFF_V)).astype(F32).astype(BF16)
    gn = gn_ref[...] * (1.0 - lambda_init)
    unit_masks = [((lane >= DIFF_QK * u) & (lane < DIFF_QK * (u + 1))).astype(F32).astype(BF16)
                  for u in range(2 * DIFF_HEADS)]
    head_masks = [(lane >= DIFF_V * h) & (lane < DIFF_V * (h + 1)) for h in range(DIFF_HEADS)]

    kt_ref[:, :s_len] = ktx_ref[0]
    kt_ref[:, s_len:] = ktc_ref[0]
    for side in range(2):
        keep = ((lane >= side * (w // 2)) & (lane < (side + 1) * (w // 2))).astype(F32).astype(BF16)
        va_ref[side, :s_len, :] = px_ref[0, :, w:2 * w] * keep + (1.0 - keep)
        va_ref[side, s_len:, :] = pc_ref[0, :, w:2 * w] * keep + (1.0 - keep)

    def attend(q, k0, q_next=None):
        def scores(qq, u):
            return _dot(qq * unit_masks[u], kt_ref[:, k0:])

        def softmax_pv(sc, u):
            m = jnp.max(sc, axis=-1, keepdims=True)
            return _dot(jnp.exp2(sc - m).astype(BF16), va_ref[(u // 2) // (DIFF_HEADS // 2), k0:, :])

        units = []
        nxt = scores(q, 0) if q_next is None else sc_ref[...]
        for u in range(2 * DIFF_HEADS):
            cur = nxt
            if u + 1 < 2 * DIFF_HEADS:
                nxt = scores(q, u + 1)
            elif q_next is not None:
                sc_ref[...] = scores(q_next, 0)
            units.append(softmax_pv(cur, u))
        o = jnp.zeros((q.shape[0], w), F32)
        for h in range(DIFF_HEADS):
            r1, r2 = units[2 * h], units[2 * h + 1]
            oh = r1 / pltpu.roll(r1, w // 2, 1) - lam * (r2 / pltpu.roll(r2, w // 2, 1))
            o = jnp.where(head_masks[h], oh, o)
        hi, lo = _split_bf16(o * o)
        ms = (_dot(hi, head_ones) + _dot(lo, head_ones)) * (1.0 / DIFF_V)
        return (o * lax.rsqrt(ms + RMS_EPS) * gn).astype(BF16)

    n_blocks = s_len // DIFF_QBLOCK

    def q_block(n):
        return px_ref[0, pl.ds(pl.multiple_of(n * DIFF_QBLOCK, DIFF_QBLOCK), DIFF_QBLOCK), 0:w]

    def body(n, carry):
        out = attend(q_block(n), 0, q_block(jnp.minimum(n + 1, n_blocks - 1)))
        ox_ref[0, pl.ds(pl.multiple_of(n * DIFF_QBLOCK, DIFF_QBLOCK), DIFF_QBLOCK), :] = out
        return carry

    sc_ref[...] = _dot(q_block(0) * unit_masks[0], kt_ref[...])
    lax.fori_loop(0, n_blocks, body, 0)
    if ctx_out:
        oc_ref[0] = attend(pc_ref[0, :, 0:w], s_len)


def _diff_call(pdx, ktx, pdc, ktc, lam, gn, lambda_init, ctx_out):
    b, s, _ = pdx.shape
    n_ctx = pdc.shape[1]
    in_specs = [pl.BlockSpec((1, s, PD_W), lambda i: (i, 0, 0)),
                pl.BlockSpec((1, DIFF_QK_W, s), lambda i: (i, 0, 0)),
                pl.BlockSpec((1, n_ctx, PD_W), lambda i: (i, 0, 0)),
                pl.BlockSpec((1, DIFF_QK_W, n_ctx), lambda i: (0, 0, i)),
                _const_spec(lam.shape, 1), _const_spec((1, DIFF_WIDTH), 1)]
    out_specs = [pl.BlockSpec((1, s, DIFF_WIDTH), lambda i: (i, 0, 0))]
    out_shape = [jax.ShapeDtypeStruct((b, s, DIFF_WIDTH), BF16)]
    if ctx_out:
        out_specs.append(pl.BlockSpec((1, n_ctx, DIFF_WIDTH), lambda i: (i, 0, 0)))
        out_shape.append(jax.ShapeDtypeStruct((b, n_ctx, DIFF_WIDTH), BF16))
    res = pl.pallas_call(
        functools.partial(_diff_kernel, ctx_out=ctx_out, lambda_init=lambda_init),
        grid=(b,),
        in_specs=in_specs, out_specs=out_specs, out_shape=out_shape,
        scratch_shapes=[pltpu.VMEM((DIFF_QK_W, s + n_ctx), BF16),
                        pltpu.VMEM((2, s + n_ctx, DIFF_WIDTH), BF16),
                        pltpu.VMEM((DIFF_QBLOCK, s + n_ctx), F32)],
        compiler_params=_params(1),
        name="diff",
    )(pdx, ktx, pdc, ktc, lam, gn.reshape(1, DIFF_WIDTH))
    return (res[0], res[1]) if ctx_out else (res[0], None)


def _prep_w_in(w):
    w = w.astype(BF16)
    starts = np.concatenate([[0], np.cumsum(IN_SIZES)])
    gq, gk, gv, gf, gb, og, sq, sk, sv, dq, dk, dv = range(len(IN_SIZES))

    def cols(first, last):
        return w[..., starts[first]:starts[last + 1]]

    pad = jnp.zeros(w.shape[:-1] + (PG_W - (2 * GLA_QK_W + 2 * GLA_WIDTH + 2 * GLA_GATE_RANK),), BF16)
    v0 = w[..., starts[sv]:starts[sv] + SWA_HD]
    v1 = w[..., starts[sv] + SWA_HD:starts[sv + 1]]
    return jnp.concatenate([cols(gq, gv), cols(og, og), cols(gf, gb), pad, cols(sq, sk), v0, v0, v1, v1,
                            cols(dq, dv)], axis=-1)


def _prep_gate(w_gate, b_gate):
    wg = jnp.zeros((2, LANE, GLA_QK_W), F32)
    for d in range(2):
        wg = wg.at[d, GLA_GATE_RANK * d:GLA_GATE_RANK * (d + 1), :].set(w_gate[d])
    return wg, b_gate.reshape(2, 1, GLA_QK_W)


def _axial_angles(rows, head_dim):
    half = head_dim // 2
    row = jnp.repeat(jnp.arange(rows, dtype=F32), GRID_W)
    col = jnp.tile(jnp.arange(GRID_W, dtype=F32), rows)
    inv_freq = 1.0 / (ROPE_BASE ** (jnp.arange(0, half, 2, dtype=F32) / half))

    def axis_angles(pos):
        a = pos[:, None] * inv_freq[None, :]
        return jnp.concatenate([a, a], axis=-1)

    return jnp.concatenate([axis_angles(row), axis_angles(col)], axis=-1)


def _rope_table(seq):
    rows = seq // GRID_W
    blocks = []
    for head_dim, qscale in ((SWA_HD, SWA_QSCALE), (DIFF_QK, DIFF_QSCALE)):
        ang = _axial_angles(rows, head_dim)
        quarter = head_dim // 4
        sign = jnp.where((jnp.arange(head_dim) % (2 * quarter)) < quarter, -1.0, 1.0).astype(F32)
        reps = LANE // head_dim
        cos = jnp.tile(jnp.cos(ang), (1, reps))
        sin = jnp.tile(jnp.sin(ang) * sign[None, :], (1, reps))
        blocks += [cos * qscale, sin * qscale, cos, sin]
    return jnp.concatenate(blocks, axis=1)


def kernel(x, c, ctx, c_ctx, w_mod, b_mod, g_ffn1, w_ffn1_in, w_ffn1_out, g_mix, w_in, w_out, w_gla_gate,
           b_gla_gate, g_gla_norm, swa_sink, diff_lambda, g_diff_norm, g_ffn2, w_ffn2_in, w_ffn2_out, g_final):
    b, s, d = x.shape
    n_ctx = ctx.shape[1]
    depth = w_mod.shape[0]

    cs = jnp.zeros((MOD_ROWS, d), F32).at[:b].set(c).at[b].set(c_ctx)
    mod = _mod_call(cs, w_mod, b_mod).reshape(depth, MOD_ROWS, N_MOD, d)
    tab = _rope_table(s)
    s2, lv = _gla_constants()

    w1i, w1o = w_ffn1_in.astype(BF16), w_ffn1_out.astype(BF16)
    w2i, w2o = w_ffn2_in.astype(BF16), w_ffn2_out.astype(BF16)
    wi = _prep_w_in(w_in)
    wo = w_out.astype(BF16)

    hx = x
    hc = ctx.reshape(1, b * n_ctx, d)
    for l in range(depth):
        ctx_out = l < depth - 1
        last = l == depth - 1
        mod_x = mod[l, :b]
        mod_c = mod[l, b:b + 1]
        lambda_init = 0.8 - 0.6 * math.exp(-0.3 * l)
        wg, bg = _prep_gate(w_gla_gate[l], b_gla_gate[l])
        sink = swa_sink[l] * LOG2E

        hx = _ffn_call(hx, mod_x, g_ffn1[l], w1i, w1o, l, 0)
        hc = _ffn_call(hc, mod_c, g_ffn1[l], w1i, w1o, l, 0)
        pgx, psx, pdx, ksx, ktx = _inproj_call(hx, mod_x, g_mix[l], wi, l, tab)
        pgc, psc, pdc, ksc, ktc = _inproj_call(hc, mod_c, g_mix[l], wi, l)
        pgc, psc, pdc = (a.reshape(b, n_ctx, a.shape[-1]) for a in (pgc, psc, pdc))

        gla_x, gla_c = _gla_call(pgx, pgc, wg, bg, g_gla_norm[l], s2, lv, ctx_out)
        swa_x, swa_c = _swa_call(psx, ksx, psc, ksc, sink, ctx_out)
        dif_x, dif_c = _diff_call(pdx, ktx, pdc, ktc, diff_lambda[l], g_diff_norm[l], lambda_init, ctx_out)

        hx = _ffn_call(hx, mod_x, g_ffn2[l], w2i, w2o, l, 6, mix=(gla_x, swa_x, dif_x, wo),
                       g_final=g_final if last else None)
        if ctx_out:
            flat = lambda a: a.reshape(1, b * n_ctx, a.shape[-1])
            hc = _ffn_call(hc, mod_c, g_ffn2[l], w2i, w2o, l, 6, mix=(flat(gla_c), flat(swa_c), flat(dif_c), wo))
    return hx
```

```python
import functools
import math

import numpy as np
import jax
import jax.numpy as jnp
from jax import lax
from jax.experimental import pallas as pl
from jax.experimental.pallas import tpu as pltpu

F32 = jnp.float32
BF16 = jnp.bfloat16

D_MODEL = 1024
DEPTH = 2
GRID_W = 64
N_MOD = 9
D_FF = 2816
RMS_EPS = 1e-6
ROPE_BASE = 10000.0

GLA_HEADS = 4
GLA_DK = 32
GLA_DV = 64
GLA_GATE_RANK = 16
GLA_TAU = 16.0
SWA_HEADS = 8
SWA_KV_HEADS = 2
SWA_HD = 64
SWA_WINDOW = 128
SWA_BLOCK = 128
DIFF_HEADS = 4
DIFF_QK = 32
DIFF_V = 64
DIFF_QBLOCK = 512

GLA_QK_W = GLA_HEADS * GLA_DK
GLA_WIDTH = GLA_HEADS * GLA_DV
SWA_WIDTH = SWA_HEADS * SWA_HD
SWA_KV_W = SWA_KV_HEADS * SWA_HD
DIFF_QK_W = DIFF_HEADS * 2 * DIFF_QK
DIFF_WIDTH = DIFF_HEADS * DIFF_V
MIX_WIDTH = GLA_WIDTH + SWA_WIDTH + DIFF_WIDTH
IN_SIZES = (GLA_QK_W, GLA_QK_W, GLA_WIDTH, GLA_GATE_RANK, GLA_GATE_RANK, GLA_WIDTH,
            SWA_WIDTH, SWA_KV_W, SWA_KV_W, DIFF_QK_W, DIFF_QK_W, DIFF_WIDTH)

LANE = 128
VMEM_LIMIT = 56 * 1024 * 1024
LOG2E = math.log2(math.e)

PG_W = 896
PS_COLS = 896
PS_W = 768
PD_COLS = 768
PD_W = 256
P_W = PG_W + PS_COLS + PD_COLS
GLA_CHUNK = 64
GLA_LEVELS = 6
GLA_GROUP = 4
GLA_GROUP_X = 8
GLA_SCAN_UNROLL = 4
SWA_PIPE_DEPTH = 2
MOD_ROWS = 16
MOD_TN = 1152
FFN_TM = 512
FFN_TF = D_FF // 2


def _dot(a, b):
    return jnp.dot(a, b, preferred_element_type=F32)


def _dot_nt(a, b):
    return lax.dot_general(a, b, (((1,), (1,)), ((), ())), preferred_element_type=F32)


def _dot_tn(a, b):
    return lax.dot_general(a, b, (((0,), (0,)), ((), ())), preferred_element_type=F32)


def _split_bf16(x):
    hi = x.astype(BF16)
    lo = (x - hi.astype(F32)).astype(BF16)
    return hi, lo


def _rms(x, g):
    return x * lax.rsqrt(jnp.mean(x * x, axis=-1, keepdims=True) + RMS_EPS) * g


def _silu(x):
    return x * (1.0 / (1.0 + jnp.exp(-x)))


def _params(n_grid):
    return pltpu.CompilerParams(dimension_semantics=("arbitrary",) * n_grid, vmem_limit_bytes=VMEM_LIMIT)


def _const_spec(shape, n_grid, single=False):
    zeros = (0,) * len(shape)
    index_map = {1: lambda a: zeros, 2: lambda a, b: zeros}[n_grid]
    if single:
        return pl.BlockSpec(shape, index_map, pipeline_mode=pl.Buffered(1))
    return pl.BlockSpec(shape, index_map)


def _layer_spec(shape, layer):
    return pl.BlockSpec((None,) + tuple(shape), lambda b, i: (layer, 0, 0), pipeline_mode=pl.Buffered(1))


def _mod_kernel(a_ref, w_ref, b_ref, o_ref):
    a = _silu(a_ref[...]).astype(BF16)
    o_ref[0] = _dot(a, w_ref[0].astype(BF16)) + b_ref[0]


def _mod_call(cs, w_mod, b_mod):
    n_layers, d, n = w_mod.shape
    return pl.pallas_call(
        _mod_kernel,
        grid=(n_layers, n // MOD_TN),
        in_specs=[pl.BlockSpec((MOD_ROWS, d), lambda l, j: (0, 0)),
                  pl.BlockSpec((1, d, MOD_TN), lambda l, j: (l, 0, j)),
                  pl.BlockSpec((1, 1, MOD_TN), lambda l, j: (l, 0, j))],
        out_specs=pl.BlockSpec((1, MOD_ROWS, MOD_TN), lambda l, j: (l, 0, j)),
        out_shape=jax.ShapeDtypeStruct((n_layers, MOD_ROWS, n), F32),
        compiler_params=_params(2),
        name="mod",
    )(cs, w_mod, b_mod.reshape(n_layers, 1, n))


def _ffn_kernel(*refs, mod_off, mix, final):
    refs = list(refs)
    h_ref, mod_ref, g_ref, win_ref, wout_ref = refs[:5]
    o_ref = refs[-1]
    mod = mod_ref[0]
    if mix:
        yg_ref, ys_ref, yd_ref, wmix_ref = refs[5:9]
    if final:
        gfin_ref = refs[-2]
    shift, scale, gate = mod[mod_off:mod_off + 1], mod[mod_off + 1:mod_off + 2], mod[mod_off + 2:mod_off + 3]
    rows = h_ref.shape[1] // 2
    halves = [slice(0, rows), slice(rows, 2 * rows)]
    n_chunks = D_FF // FFN_TF

    def prologue(rs):
        x = h_ref[0, rs, :]
        if mix:
            a = GLA_WIDTH
            b = GLA_WIDTH + SWA_WIDTH
            proj = (_dot(yg_ref[0, rs, :], wmix_ref[0:a, :]) + _dot(ys_ref[0, rs, :], wmix_ref[a:b, :])
                    + _dot(yd_ref[0, rs, :], wmix_ref[b:, :]))
            x = x + mod[5:6] * proj
        return x, (_rms(x, g_ref[...]) * (1.0 + scale) + shift).astype(BF16)

    def chunk(y, c):
        gt = _dot(y, win_ref[:, c * FFN_TF:(c + 1) * FFN_TF])
        up = _dot(y, win_ref[:, D_FF + c * FFN_TF:D_FF + (c + 1) * FFN_TF])
        act = (_silu(gt) * up).astype(BF16)
        return _dot(act, wout_ref[c * FFN_TF:(c + 1) * FFN_TF, :])

    def epilogue(rs, x, acc):
        out = x + (0.5 * gate) * acc
        if final:
            out = _rms(out, gfin_ref[...])
        o_ref[0, rs, :] = out

    xa, ya = prologue(halves[0])
    acc_a = chunk(ya, 0)
    xb, yb = prologue(halves[1])
    for c in range(1, n_chunks):
        acc_a = acc_a + chunk(ya, c)
    acc_b = chunk(yb, 0)
    epilogue(halves[0], xa, acc_a)
    for c in range(1, n_chunks):
        acc_b = acc_b + chunk(yb, c)
    epilogue(halves[1], xb, acc_b)


def _ffn_call(h, mod, g, w_in, w_out, layer, mod_off, mix=None, g_final=None):
    bx, t, d = h.shape
    tm = min(FFN_TM, t)
    final = g_final is not None
    tile = lambda width: pl.BlockSpec((1, tm, width), lambda b, i: (b, i, 0))
    in_specs = [tile(d),
                pl.BlockSpec((1, N_MOD, d), lambda b, i: (b, 0, 0)),
                _const_spec((1, d), 2),
                _layer_spec((d, 2 * D_FF), layer),
                _layer_spec((D_FF, d), layer)]
    args = [h, mod, g.reshape(1, d), w_in, w_out]
    if mix is not None:
        in_specs += [tile(GLA_WIDTH), tile(SWA_WIDTH), tile(DIFF_WIDTH), _layer_spec((MIX_WIDTH, d), layer)]
        args += list(mix)
    if final:
        in_specs.append(_const_spec((1, d), 2))
        args.append(g_final.reshape(1, d))
    return pl.pallas_call(
        functools.partial(_ffn_kernel, mod_off=mod_off, mix=mix is not None, final=final),
        grid=(bx, t // tm),
        in_specs=in_specs,
        out_specs=pl.BlockSpec((1, tm, d), lambda b, i: (b, i, 0)),
        out_shape=jax.ShapeDtypeStruct(h.shape, F32),
        compiler_params=_params(2),
        name="ffn",
    )(*args)


SWA_QSCALE = SWA_HD ** -0.5 * LOG2E
DIFF_QSCALE = DIFF_QK ** -0.5 * LOG2E


def _rot_half(blk, quarter, first):
    return jnp.where(first, pltpu.roll(blk, LANE - quarter, 1), pltpu.roll(blk, quarter, 1))


def _inproj_kernel(*refs, rope):
    if rope:
        h_ref, mod_ref, g_ref, w_ref, tab_ref, pg_ref, ps_ref, pd_ref, kts_ref, qt_ref, vt_ref = refs
    else:
        h_ref, mod_ref, g_ref, w_ref, pg_ref, ps_ref, pd_ref, kts_ref, qt_ref, vt_ref = refs
    x = h_ref[0]
    mod = mod_ref[0]
    y = (_rms(x, g_ref[...]) * (1.0 + mod[4:5]) + mod[3:4]).astype(BF16)
    p = _dot(y, w_ref[...])
    pg_ref[0] = p[:, :PG_W]
    swa = [p[:, PG_W + i * LANE:PG_W + (i + 1) * LANE] for i in range(PS_COLS // LANE)]
    dif = [p[:, PG_W + PS_COLS + i * LANE:PG_W + PS_COLS + (i + 1) * LANE] for i in range(PD_COLS // LANE)]
    if rope:
        tab = [tab_ref[:, i * LANE:(i + 1) * LANE] for i in range(8)]
        lane = lax.broadcasted_iota(jnp.int32, (1, LANE), 1)
        first_s = (lane % (SWA_HD // 2)) < (SWA_HD // 4)
        first_d = (lane % (DIFF_QK // 2)) < (DIFF_QK // 4)
        for i in range(5):
            c, s = (tab[0], tab[1]) if i < 4 else (tab[2], tab[3])
            swa[i] = swa[i] * c + _rot_half(swa[i], SWA_HD // 4, first_s) * s
        for i in range(4):
            c, s = (tab[4], tab[5]) if i < 2 else (tab[6], tab[7])
            dif[i] = dif[i] * c + _rot_half(dif[i], DIFF_QK // 4, first_d) * s
    else:
        for i in range(4):
            swa[i] = swa[i] * SWA_QSCALE
        for i in range(2):
            dif[i] = dif[i] * DIFF_QSCALE
    ps_ref[0] = jnp.concatenate(swa[0:4] + swa[5:7], axis=1).astype(BF16)
    kt = swa[4].T
    k0, k1 = kt[:SWA_HD], kt[SWA_HD:]
    kts_ref[0] = jnp.concatenate([k0, k0, k1, k1], axis=0).astype(BF16)
    pd_ref[0] = jnp.concatenate(dif[2:4], axis=1).astype(BF16)
    qt_ref[0] = jnp.concatenate(dif[0:2], axis=1).T.astype(BF16)
    vt_ref[0] = jnp.concatenate(dif[4:6], axis=1).T.astype(BF16)


def _inproj_call(h, mod, g, w, layer, tab=None):
    bx, t, d = h.shape
    tm = min(FFN_TM, t)
    rope = tab is not None
    in_specs = [pl.BlockSpec((1, tm, d), lambda j, b: (b, j, 0)),
                pl.BlockSpec((1, N_MOD, d), lambda j, b: (b, 0, 0)),
                _const_spec((1, d), 2),
                _layer_spec((d, P_W), layer)]
    args = [h, mod, g.reshape(1, d), w]
    if rope:
        in_specs.append(pl.BlockSpec((tm, 8 * LANE), lambda j, b: (j, 0)))
        args.append(tab)
    out_map = lambda j, b: (b, j, 0)
    return pl.pallas_call(
        functools.partial(_inproj_kernel, rope=rope),
        grid=(t // tm, bx),
        in_specs=in_specs,
        out_specs=[pl.BlockSpec((1, tm, PG_W), out_map), pl.BlockSpec((1, tm, PS_W), out_map),
                   pl.BlockSpec((1, tm, PD_W), out_map),
                   pl.BlockSpec((1, 2 * LANE, tm), lambda j, b: (b, 0, j)),
                   pl.BlockSpec((1, DIFF_QK_W, tm), lambda j, b: (b, 0, j)),
                   pl.BlockSpec((1, DIFF_WIDTH, tm), lambda j, b: (b, 0, j))],
        out_shape=[jax.ShapeDtypeStruct((bx, t, PG_W), F32), jax.ShapeDtypeStruct((bx, t, PS_W), BF16),
                   jax.ShapeDtypeStruct((bx, t, PD_W), BF16),
                   jax.ShapeDtypeStruct((bx, 2 * LANE, t), BF16),
                   jax.ShapeDtypeStruct((bx, DIFF_QK_W, t), BF16),
                   jax.ShapeDtypeStruct((bx, DIFF_WIDTH, t), BF16)],
        compiler_params=_params(2),
        name="inproj",
    )(*args)


GLA_SROWS = (GLA_LEVELS + 2) * GLA_CHUNK
GLA_LV_DIAG = GLA_LEVELS
GLA_LV_NONE = GLA_LEVELS + 1


def _gla_constants():
    c = GLA_CHUNK
    r = np.arange(c)[:, None]
    t = np.arange(c)[None, :]
    blocks = []
    for lv in range(GLA_LEVELS):
        half = 1 << lv
        mid = (r // (2 * half)) * (2 * half) + half
        second = r >= mid
        blocks.append(np.where(second, (t >= mid) & (t <= r), (t > r) & (t < mid)))
    blocks.append(t <= r)
    blocks.append(t > r)
    fwd = np.concatenate(blocks, axis=0).astype(np.float32)
    bwd = np.concatenate([b[::-1, ::-1] for b in blocks], axis=0).astype(np.float32)
    s2 = np.stack([np.concatenate([m, m], axis=1) for m in (fwd, bwd)])
    i = np.arange(c)[:, None]
    j = np.arange(c)[None, :]
    x = np.bitwise_xor(i, j)
    lvl = np.where(j > i, GLA_LV_NONE,
                   np.where(i == j, GLA_LV_DIAG, np.floor(np.log2(np.maximum(x, 1))).astype(np.int64)))
    lv_f = np.tile(lvl, (2, 2))
    lv_b = np.tile(lvl[::-1, ::-1], (2, 2))
    return jnp.asarray(s2, BF16), jnp.asarray(np.stack([lv_f, lv_b]), jnp.int32)


def _gla_kernel(*refs, ctx_out):
    if ctx_out:
        (pgx_ref, pgc_ref, wg_ref, bg_ref, gn_ref, s2_ref, lv_ref, ox_ref, oc_ref,
         oi_ref, qc_ref, upd_ref, dec_ref, st_ref) = refs
    else:
        (pgx_ref, pgc_ref, wg_ref, bg_ref, gn_ref, s2_ref, lv_ref, ox_ref,
         oi_ref, qc_ref, upd_ref, dec_ref, st_ref) = refs
        oc_ref = None
    c = GLA_CHUNK
    grp = GLA_GROUP
    hv = GLA_HEADS * GLA_DV
    n_ctx = pgc_ref.shape[1]
    s_len = pgx_ref.shape[1]
    row = lax.broadcasted_iota(jnp.int32, (hv, GLA_QK_W), 0)
    lane = lax.broadcasted_iota(jnp.int32, (hv, GLA_QK_W), 1)
    head_qk = (row // GLA_DV) == (lane // GLA_DK)
    head_qk_b = head_qk.astype(F32).astype(BF16)
    row2 = lax.broadcasted_iota(jnp.int32, (hv, hv), 0)
    col2 = lax.broadcasted_iota(jnp.int32, (hv, hv), 1)
    head_v_b = ((row2 // GLA_DV) == (col2 // GLA_DV)).astype(F32).astype(BF16)
    gn = gn_ref[...]
    wh, wl = _split_bf16(jnp.concatenate([wg_ref[0], wg_ref[1]], axis=1))
    w3 = jnp.concatenate([wh, wh, wl], axis=0)
    bias = jnp.concatenate([bg_ref[0], bg_ref[1]], axis=1)

    lane1 = lax.broadcasted_iota(jnp.int32, (1, GLA_QK_W), 1)
    pair_mask = [((lane1 // (2 * GLA_DK)) == p).astype(F32).astype(BF16) for p in range(2)]
    parity_mask = [(((lane1 // GLA_DK) % 2) == hh).astype(F32).astype(BF16) for hh in range(2)]
    lane_v = lax.broadcasted_iota(jnp.int32, (1, hv), 1)
    hv_mask = [((lane_v // GLA_DV) == h).astype(F32).astype(BF16) for h in range(GLA_HEADS)]

    def pair_rows(a):
        a = a.astype(BF16)
        return jnp.concatenate([a * pair_mask[0], a * pair_mask[1]], axis=0)

    def parity_rows(a):
        a = a.astype(BF16)
        return jnp.concatenate([a * parity_mask[0], a * parity_mask[1]], axis=0)

    def local(src_ref, src_row, dst_row, with_out, grp):
        blks = [src_ref[0, pl.ds(pl.multiple_of(src_row + g * c, c), c), :] for g in range(grp)]
        qs = [b[:, 0:128] * (GLA_DK ** -0.5) for b in blks]
        ks = [b[:, 128:256] for b in blks]
        vs = [b[:, 256:512].astype(BF16) for b in blks]
        zh, zl = _split_bf16(jnp.concatenate([b[:, 768:896] for b in blks], axis=0))
        z = _dot(jnp.concatenate([zh, zl, zh], axis=1), w3) + bias
        gate = (jnp.minimum(z, 0.0) - jnp.log(1.0 + jnp.exp(-jnp.abs(z)))) * (1.0 / GLA_TAU)
        gh, gl = _split_bf16(gate)
        f_all = []
        for d in range(2):
            cols = slice(d * GLA_QK_W, (d + 1) * GLA_QK_W)
            g2 = jnp.concatenate([jnp.concatenate([gh[g * c:(g + 1) * c, cols], gl[g * c:(g + 1) * c, cols]], axis=0)
                                  for g in range(grp)], axis=1)
            f_all.append(jnp.exp(_dot(s2_ref[d], g2)))

        def fac(d, g, block):
            return f_all[d][block * c:(block + 1) * c, g * GLA_QK_W:(g + 1) * GLA_QK_W]

        if with_out:
            vbds = [[jnp.concatenate([v * hv_mask[2 * p], v * hv_mask[2 * p + 1]], axis=0) for p in range(2)]
                    for v in vs]
            diag = [_dot_nt(pair_rows(q), parity_rows(k)) for q, k in zip(qs, ks)]
        for d in range(2):
            for g in range(grp):
                rows = pl.ds(pl.multiple_of(dst_row + g * c, c), c)
                ci = (dst_row + g * c) // c
                upd = _dot_tn(vs[g], (ks[g] * fac(d, g, GLA_LEVELS + 1)).astype(BF16))
                upd_ref[d, ci] = jnp.where(head_qk, upd, 0.0)
                fcum = fac(d, g, GLA_LEVELS)
                last = c - 1 if d == 0 else 0
                dec_ref[d, ci] = jnp.broadcast_to(fcum[last:last + 1], (8, GLA_QK_W))
                if with_out:
                    qc_ref[d, rows, :] = (qs[g] * fcum).astype(BF16)
            if with_out:
                lv = lv_ref[d]
                atts = [jnp.where(lv == GLA_LV_DIAG, dg, 0.0) for dg in diag]
                for level in range(GLA_LEVELS):
                    for g in range(grp):
                        fl = fac(d, g, level)
                        s = _dot_nt(pair_rows(qs[g] * fl), parity_rows(ks[g] * fl))
                        atts[g] = jnp.where(lv == level, s, atts[g])
                for g in range(grp):
                    rows = pl.ds(pl.multiple_of(dst_row + g * c, c), c)
                    att = atts[g].astype(BF16)
                    oi_ref[d, rows, :] = _dot(att[:c], vbds[g][0]) + _dot(att[c:], vbds[g][1])

    def local_pass(src_ref, dst0, with_out, grp):
        def body(i, carry):
            local(src_ref, i * (grp * c), dst0 + i * (grp * c), with_out, grp)
            return carry
        lax.fori_loop(0, src_ref.shape[1] // (grp * c), body, 0)

    def scan_pass(first, n, with_out):
        def body(i, carry):
            for d in range(2):
                st = st_ref[d]
                for j in range(GLA_SCAN_UNROLL):
                    step = i * GLA_SCAN_UNROLL + j
                    ci = first + (step if d == 0 else n - 1 - step)
                    if with_out:
                        rows = pl.ds(pl.multiple_of(ci * c, c), c)
                        oi_ref[d, rows, :] = oi_ref[d, rows, :] + _dot_nt(qc_ref[d, rows, :], st.astype(BF16))
                    st = dec_ref[d, ci][0:1] * st + upd_ref[d, ci]
                st_ref[d] = st
            return carry
        lax.fori_loop(0, n // GLA_SCAN_UNROLL, body, 0)

    def finish_pass(src_ref, src0, out_ref, n_rows):
        tile = grp * c

        def body(i, carry):
            r = pl.multiple_of(i * tile, tile)
            rs = pl.ds(pl.multiple_of(src0 + r, tile), tile)
            o = oi_ref[0, rs, :] + oi_ref[1, rs, :]
            hi, lo = _split_bf16(o * o)
            ms = (_dot(hi, head_v_b) + _dot(lo, head_v_b)) * (1.0 / GLA_DV)
            og = src_ref[0, pl.ds(r, tile), 512:768]
            out_ref[0, pl.ds(r, tile), :] = (o * lax.rsqrt(ms + RMS_EPS) * gn * _silu(og)).astype(BF16)
            return carry
        lax.fori_loop(0, n_rows // tile, body, 0)

    local_pass(pgc_ref, 0, ctx_out, GLA_GROUP)
    local_pass(pgx_ref, n_ctx, True, GLA_GROUP_X)
    st_ref[...] = jnp.zeros(st_ref.shape, F32)
    scan_pass(0, n_ctx // c, ctx_out)
    scan_pass(n_ctx // c, s_len // c, True)
    finish_pass(pgx_ref, n_ctx, ox_ref, s_len)
    if ctx_out:
        finish_pass(pgc_ref, 0, oc_ref, n_ctx)


def _gla_call(pgx, pgc, wg, bg, gn, s2, lv, ctx_out):
    b, s, _ = pgx.shape
    n_ctx = pgc.shape[1]
    hv = GLA_WIDTH
    n_rows = s + n_ctx
    in_specs = [pl.BlockSpec((1, s, PG_W), lambda i: (i, 0, 0)),
                pl.BlockSpec((1, n_ctx, PG_W), lambda i: (i, 0, 0)),
                _const_spec(wg.shape, 1), _const_spec(bg.shape, 1), _const_spec((1, hv), 1),
                _const_spec(s2.shape, 1), _const_spec(lv.shape, 1)]
    out_specs = [pl.BlockSpec((1, s, hv), lambda i: (i, 0, 0))]
    out_shape = [jax.ShapeDtypeStruct((b, s, hv), BF16)]
    if ctx_out:
        out_specs.append(pl.BlockSpec((1, n_ctx, hv), lambda i: (i, 0, 0)))
        out_shape.append(jax.ShapeDtypeStruct((b, n_ctx, hv), BF16))
    scratch = [pltpu.VMEM((2, n_rows, hv), F32),
               pltpu.VMEM((2, n_rows, GLA_QK_W), BF16),
               pltpu.VMEM((2, n_rows // GLA_CHUNK, hv, GLA_QK_W), F32),
               pltpu.VMEM((2, n_rows // GLA_CHUNK, 8, GLA_QK_W), F32),
               pltpu.VMEM((2, hv, GLA_QK_W), F32)]
    res = pl.pallas_call(
        functools.partial(_gla_kernel, ctx_out=ctx_out),
        grid=(b,),
        in_specs=in_specs, out_specs=out_specs, out_shape=out_shape, scratch_shapes=scratch,
        compiler_params=_params(1),
        name="gla",
    )(pgx, pgc, wg, bg, gn.reshape(1, hv), s2, lv)
    return (res[0], res[1]) if ctx_out else (res[0], None)


def _swa_kernel(*refs, ctx_out):
    if ctx_out:
        sink_ref, px_ref, ktx_ref, pc_ref, ktc_ref, ox_ref, oc_ref, sc_ref = refs
    else:
        sink_ref, px_ref, ktx_ref, pc_ref, ktc_ref, ox_ref, sc_ref = refs
    s_len = px_ref.shape[1]
    n_ctx = pc_ref.shape[1]
    blk = SWA_BLOCK
    band = 3 * blk
    grp = SWA_HEADS // SWA_KV_HEADS
    lane = lax.broadcasted_iota(jnp.int32, (1, LANE), 1)
    low = lane < SWA_HD
    half_mask = [low.astype(F32).astype(BF16), (~low).astype(F32).astype(BF16)]
    rel = (lax.broadcasted_iota(jnp.int32, (blk, band), 0)
           - lax.broadcasted_iota(jnp.int32, (blk, band), 1))

    def col(i):
        return slice(i * LANE, (i + 1) * LANE)

    def krows(kv):
        return slice(kv * LANE, (kv + 1) * LANE)

    def sink_col(n_rows, kv, half):
        ha = grp * kv + half
        return jnp.concatenate([jnp.full((n_rows, 1), sink_ref[ha], F32),
                                jnp.full((n_rows, 1), sink_ref[ha + 2], F32)], axis=0)

    def softmax_pv(sc, v, snk):
        m = jnp.maximum(snk, jnp.max(sc, axis=-1, keepdims=True))
        p = jnp.exp2(sc - m).astype(BF16)
        r = _dot(p, jnp.concatenate([v, jnp.ones(v.shape, BF16)], axis=1))
        return r[:, :LANE] / (r[:, LANE:] + jnp.exp2(snk - m))

    def pipeline(tasks, depth, carried=(), prefetch=()):
        queue = list(carried)
        for t in range(len(queue), min(depth, len(tasks))):
            queue.append(tasks[t][0]())
        outs = []
        for t, (_, finish) in enumerate(tasks):
            ahead = t + depth
            if ahead < len(tasks):
                queue.append(tasks[ahead][0]())
            elif ahead - len(tasks) < len(prefetch):
                prefetch[ahead - len(tasks)]()
            outs.append(finish(queue[t]))
        return outs

    units = [(kv, half) for kv in range(SWA_KV_HEADS) for half in range(2)]
    blocks_per_step = 2
    n_steps = s_len // (blk * blocks_per_step)
    depth = SWA_PIPE_DEPTH

    def block_tasks(n):
        r0 = pl.multiple_of(n * blk, blk)
        start = pl.multiple_of(jnp.clip((n - 1) * blk, 0, s_len - band), blk)
        tasks = []
        for kv, half in units:
            def scores(kv=kv, half=half):
                bias = jnp.where(jnp.abs(rel + (r0 - start)) <= SWA_WINDOW, 0.0, -jnp.inf)
                bias = jnp.concatenate([bias, jnp.zeros((blk, n_ctx), F32)], axis=1)
                q = jnp.concatenate([px_ref[0, pl.ds(r0, blk), col(2 * kv)],
                                     px_ref[0, pl.ds(r0, blk), col(2 * kv + 1)]], axis=0) * half_mask[half]
                kt = jnp.concatenate([ktx_ref[0, krows(kv), pl.ds(start, band)],
                                      ktc_ref[0, krows(kv), :]], axis=1)
                return _dot(q, kt) + jnp.concatenate([bias, bias], axis=0)

            def finish(cur, kv=kv, half=half):
                v = jnp.concatenate([px_ref[0, pl.ds(start, band), col(4 + kv)],
                                     pc_ref[0, :, col(4 + kv)]], axis=0)
                return softmax_pv(cur, v, sink_col(blk, kv, half))

            tasks.append((scores, finish))
        return tasks, r0

    def prefetch_into(slot, scores):
        def run():
            sc_ref[slot] = scores()
        return run

    def body(i, carry):
        tasks, rows = [], []
        for sub in range(blocks_per_step):
            t, r0 = block_tasks(i * blocks_per_step + sub)
            tasks += t
            rows.append(r0)
        nxt, _ = block_tasks(jnp.minimum(i + 1, n_steps - 1) * blocks_per_step)
        outs = pipeline(tasks, depth, carried=[sc_ref[k] for k in range(depth)],
                        prefetch=[prefetch_into(k, nxt[k][0]) for k in range(depth)])
        for sub in range(blocks_per_step):
            for kv in range(SWA_KV_HEADS):
                t = (sub * SWA_KV_HEADS + kv) * 2
                o2 = jnp.where(low, outs[t], outs[t + 1]).astype(BF16)
                ox_ref[0, pl.ds(rows[sub], blk), col(2 * kv)] = o2[:blk]
                ox_ref[0, pl.ds(rows[sub], blk), col(2 * kv + 1)] = o2[blk:]
        return carry

    first, _ = block_tasks(jnp.int32(0))
    for k in range(depth):
        sc_ref[k] = first[k][0]()
    lax.fori_loop(0, n_steps, body, 0)

    if ctx_out:
        tasks = []
        for kv, half in units:
            def scores(kv=kv, half=half):
                q = jnp.concatenate([pc_ref[0, :, col(2 * kv)], pc_ref[0, :, col(2 * kv + 1)]], axis=0)
                return _dot(q * half_mask[half], ktc_ref[0, krows(kv), :])

            def finish(cur, kv=kv, half=half):
                return softmax_pv(cur, pc_ref[0, :, col(4 + kv)], sink_col(n_ctx, kv, half))

            tasks.append((scores, finish))
        outs = pipeline(tasks, 1)
        for kv in range(SWA_KV_HEADS):
            o2 = jnp.where(low, outs[2 * kv], outs[2 * kv + 1]).astype(BF16)
            oc_ref[0, :, col(2 * kv)] = o2[:n_ctx]
            oc_ref[0, :, col(2 * kv + 1)] = o2[n_ctx:]


def _swa_call(psx, ktx, psc, ktc, sink, ctx_out):
    b, s, _ = psx.shape
    n_ctx = psc.shape[1]
    in_specs = [pl.BlockSpec(memory_space=pltpu.SMEM),
                pl.BlockSpec((1, s, PS_W), lambda i: (i, 0, 0)),
                pl.BlockSpec((1, 2 * LANE, s), lambda i: (i, 0, 0)),
                pl.BlockSpec((1, n_ctx, PS_W), lambda i: (i, 0, 0)),
                pl.BlockSpec((1, 2 * LANE, n_ctx), lambda i: (0, 0, i))]
    out_specs = [pl.BlockSpec((1, s, SWA_WIDTH), lambda i: (i, 0, 0))]
    out_shape = [jax.ShapeDtypeStruct((b, s, SWA_WIDTH), BF16)]
    if ctx_out:
        out_specs.append(pl.BlockSpec((1, n_ctx, SWA_WIDTH), lambda i: (i, 0, 0)))
        out_shape.append(jax.ShapeDtypeStruct((b, n_ctx, SWA_WIDTH), BF16))
    res = pl.pallas_call(
        functools.partial(_swa_kernel, ctx_out=ctx_out),
        grid=(b,),
        in_specs=in_specs, out_specs=out_specs, out_shape=out_shape,
        scratch_shapes=[pltpu.VMEM((SWA_PIPE_DEPTH, 2 * SWA_BLOCK, 3 * SWA_BLOCK + n_ctx), F32)],
        compiler_params=_params(1),
        name="swa",
    )(sink, psx, ktx, psc, ktc)
    return (res[0], res[1]) if ctx_out else (res[0], None)


DIFF_ONES_ROWS = 16
DIFF_VROWS = DIFF_V + DIFF_ONES_ROWS


def _diff_kernel(*refs, ctx_out, lambda_init):
    if ctx_out:
        (kx_ref, qtx_ref, vtx_ref, kc_ref, qtc_ref, vtc_ref, lam_ref, gn_ref, ox_ref, oc_ref,
         k_ref, vt_ref, sc_ref) = refs
    else:
        (kx_ref, qtx_ref, vtx_ref, kc_ref, qtc_ref, vtc_ref, lam_ref, gn_ref, ox_ref,
         k_ref, vt_ref, sc_ref) = refs
    s_len = kx_ref.shape[1]
    n_ctx = kc_ref.shape[1]
    w = DIFF_QK_W
    n_units = 2 * DIFF_HEADS
    lv = lam_ref[...]
    lam = (jnp.exp(jnp.sum(lv[0:1] * lv[1:2], axis=-1, keepdims=True))
           - jnp.exp(jnp.sum(lv[2:3] * lv[3:4], axis=-1, keepdims=True)) + lambda_init)
    row2 = lax.broadcasted_iota(jnp.int32, (w, w), 0)
    col2 = lax.broadcasted_iota(jnp.int32, (w, w), 1)
    head_ones = ((row2 // DIFF_V) == (col2 // DIFF_V)).astype(F32).astype(BF16)
    gn = gn_ref[...] * (1.0 - lambda_init)

    def unit_masks(n_q):
        row = lax.broadcasted_iota(jnp.int32, (w, n_q), 0)
        return [((row // DIFF_QK) == u).astype(F32).astype(BF16) for u in range(n_units)]

    k_ref[:s_len, :] = kx_ref[0]
    k_ref[s_len:, :] = kc_ref[0]
    for h in range(DIFF_HEADS):
        vt_ref[h, :DIFF_V, :s_len] = vtx_ref[0, h * DIFF_V:(h + 1) * DIFF_V, :]
        vt_ref[h, :DIFF_V, s_len:] = vtc_ref[0, h * DIFF_V:(h + 1) * DIFF_V, :]
        vt_ref[h, DIFF_V:, :] = jnp.ones((DIFF_ONES_ROWS, s_len + n_ctx), BF16)

    def attend(qt, masks, k0, qt_next=None):
        def scores(qq, u):
            return _dot(k_ref[k0:, :], qq * masks[u])

        def softmax_pv(sc, u):
            m = jnp.max(sc, axis=0, keepdims=True)
            return _dot(vt_ref[u // 2, :, k0:], jnp.exp2(sc - m).astype(BF16))

        units = []
        nxt = scores(qt, 0) if qt_next is None else sc_ref[...]
        for u in range(n_units):
            cur = nxt
            if u + 1 < n_units:
                nxt = scores(qt, u + 1)
            elif qt_next is not None:
                sc_ref[...] = scores(qt_next, 0)
            units.append(softmax_pv(cur, u))
        heads = []
        for h in range(DIFF_HEADS):
            r1, r2 = units[2 * h], units[2 * h + 1]
            heads.append(r1[:DIFF_V] / r1[DIFF_V:DIFF_V + 1] - lam * (r2[:DIFF_V] / r2[DIFF_V:DIFF_V + 1]))
        o = jnp.concatenate(heads, axis=0).T
        hi, lo = _split_bf16(o * o)
        ms = (_dot(hi, head_ones) + _dot(lo, head_ones)) * (1.0 / DIFF_V)
        return (o * lax.rsqrt(ms + RMS_EPS) * gn).astype(BF16)

    n_blocks = s_len // DIFF_QBLOCK
    masks_x = unit_masks(DIFF_QBLOCK)

    def q_block(n):
        return qtx_ref[0, :, pl.ds(pl.multiple_of(n * DIFF_QBLOCK, DIFF_QBLOCK), DIFF_QBLOCK)]

    def body(n, carry):
        out = attend(q_block(n), masks_x, 0, q_block(jnp.minimum(n + 1, n_blocks - 1)))
        ox_ref[0, pl.ds(pl.multiple_of(n * DIFF_QBLOCK, DIFF_QBLOCK), DIFF_QBLOCK), :] = out
        return carry

    sc_ref[...] = _dot(k_ref[...], q_block(0) * masks_x[0])
    lax.fori_loop(0, n_blocks, body, 0)
    if ctx_out:
        oc_ref[0] = attend(qtc_ref[0], unit_masks(n_ctx), s_len)


def _diff_call(kx, qtx, vtx, kc, qtc, vtc, lam, gn, lambda_init, ctx_out):
    b, s, _ = kx.shape
    n_ctx = kc.shape[1]
    ctx_cols = lambda i: (0, 0, i)
    in_specs = [pl.BlockSpec((1, s, PD_W), lambda i: (i, 0, 0)),
                pl.BlockSpec((1, DIFF_QK_W, s), lambda i: (i, 0, 0)),
                pl.BlockSpec((1, DIFF_WIDTH, s), lambda i: (i, 0, 0)),
                pl.BlockSpec((1, n_ctx, PD_W), lambda i: (i, 0, 0)),
                pl.BlockSpec((1, DIFF_QK_W, n_ctx), ctx_cols),
                pl.BlockSpec((1, DIFF_WIDTH, n_ctx), ctx_cols),
                _const_spec(lam.shape, 1), _const_spec((1, DIFF_WIDTH), 1)]
    out_specs = [pl.BlockSpec((1, s, DIFF_WIDTH), lambda i: (i, 0, 0))]
    out_shape = [jax.ShapeDtypeStruct((b, s, DIFF_WIDTH), BF16)]
    if ctx_out:
        out_specs.append(pl.BlockSpec((1, n_ctx, DIFF_WIDTH), lambda i: (i, 0, 0)))
        out_shape.append(jax.ShapeDtypeStruct((b, n_ctx, DIFF_WIDTH), BF16))
    res = pl.pallas_call(
        functools.partial(_diff_kernel, ctx_out=ctx_out, lambda_init=lambda_init),
        grid=(b,),
        in_specs=in_specs, out_specs=out_specs, out_shape=out_shape,
        scratch_shapes=[pltpu.VMEM((s + n_ctx, DIFF_QK_W), BF16),
                        pltpu.VMEM((DIFF_HEADS, DIFF_VROWS, s + n_ctx), BF16),
                        pltpu.VMEM((s + n_ctx, DIFF_QBLOCK), F32)],
        compiler_params=_params(1),
        name="diff",
    )(kx, qtx, vtx, kc, qtc, vtc, lam, gn.reshape(1, DIFF_WIDTH))
    return (res[0], res[1]) if ctx_out else (res[0], None)


def _prep_w_in(w):
    w = w.astype(BF16)
    starts = np.concatenate([[0], np.cumsum(IN_SIZES)])
    gq, gk, gv, gf, gb, og, sq, sk, sv, dq, dk, dv = range(len(IN_SIZES))

    def cols(first, last):
        return w[..., starts[first]:starts[last + 1]]

    pad = jnp.zeros(w.shape[:-1] + (PG_W - (2 * GLA_QK_W + 2 * GLA_WIDTH + 2 * GLA_GATE_RANK),), BF16)
    v0 = w[..., starts[sv]:starts[sv] + SWA_HD]
    v1 = w[..., starts[sv] + SWA_HD:starts[sv + 1]]
    return jnp.concatenate([cols(gq, gv), cols(og, og), cols(gf, gb), pad, cols(sq, sk), v0, v0, v1, v1,
                            cols(dq, dv)], axis=-1)


def _prep_gate(w_gate, b_gate):
    wg = jnp.zeros((2, LANE, GLA_QK_W), F32)
    for d in range(2):
        wg = wg.at[d, GLA_GATE_RANK * d:GLA_GATE_RANK * (d + 1), :].set(w_gate[d])
    return wg, b_gate.reshape(2, 1, GLA_QK_W)


def _axial_angles(rows, head_dim):
    half = head_dim // 2
    row = jnp.repeat(jnp.arange(rows, dtype=F32), GRID_W)
    col = jnp.tile(jnp.arange(GRID_W, dtype=F32), rows)
    inv_freq = 1.0 / (ROPE_BASE ** (jnp.arange(0, half, 2, dtype=F32) / half))

    def axis_angles(pos):
        a = pos[:, None] * inv_freq[None, :]
        return jnp.concatenate([a, a], axis=-1)

    return jnp.concatenate([axis_angles(row), axis_angles(col)], axis=-1)


def _rope_table(seq):
    rows = seq // GRID_W
    blocks = []
    for head_dim, qscale in ((SWA_HD, SWA_QSCALE), (DIFF_QK, DIFF_QSCALE)):
        ang = _axial_angles(rows, head_dim)
        quarter = head_dim // 4
        sign = jnp.where((jnp.arange(head_dim) % (2 * quarter)) < quarter, -1.0, 1.0).astype(F32)
        reps = LANE // head_dim
        cos = jnp.tile(jnp.cos(ang), (1, reps))
        sin = jnp.tile(jnp.sin(ang) * sign[None, :], (1, reps))
        blocks += [cos * qscale, sin * qscale, cos, sin]
    return jnp.concatenate(blocks, axis=1)


def kernel(x, c, ctx, c_ctx, w_mod, b_mod, g_ffn1, w_ffn1_in, w_ffn1_out, g_mix, w_in, w_out, w_gla_gate,
           b_gla_gate, g_gla_norm, swa_sink, diff_lambda, g_diff_norm, g_ffn2, w_ffn2_in, w_ffn2_out, g_final):
    b, s, d = x.shape
    n_ctx = ctx.shape[1]
    depth = w_mod.shape[0]

    cs = jnp.zeros((MOD_ROWS, d), F32).at[:b].set(c).at[b].set(c_ctx)
    mod = _mod_call(cs, w_mod, b_mod).reshape(depth, MOD_ROWS, N_MOD, d)
    tab = _rope_table(s)
    s2, lv = _gla_constants()

    w1i, w1o = w_ffn1_in.astype(BF16), w_ffn1_out.astype(BF16)
    w2i, w2o = w_ffn2_in.astype(BF16), w_ffn2_out.astype(BF16)
    wi = _prep_w_in(w_in)
    wo = w_out.astype(BF16)

    hx = x
    hc = ctx.reshape(1, b * n_ctx, d)
    for l in range(depth):
        ctx_out = l < depth - 1
        last = l == depth - 1
        mod_x = mod[l, :b]
        mod_c = mod[l, b:b + 1]
        lambda_init = 0.8 - 0.6 * math.exp(-0.3 * l)
        wg, bg = _prep_gate(w_gla_gate[l], b_gla_gate[l])
        sink = swa_sink[l] * LOG2E

        hx = _ffn_call(hx, mod_x, g_ffn1[l], w1i, w1o, l, 0)
        hc = _ffn_call(hc, mod_c, g_ffn1[l], w1i, w1o, l, 0)
        pgx, psx, pdx, ksx, qtx, vtx = _inproj_call(hx, mod_x, g_mix[l], wi, l, tab)
        pgc, psc, pdc, ksc, qtc, vtc = _inproj_call(hc, mod_c, g_mix[l], wi, l)
        pgc, psc, pdc = (a.reshape(b, n_ctx, a.shape[-1]) for a in (pgc, psc, pdc))

        gla_x, gla_c = _gla_call(pgx, pgc, wg, bg, g_gla_norm[l], s2, lv, ctx_out)
        swa_x, swa_c = _swa_call(psx, ksx, psc, ksc, sink, ctx_out)
        dif_x, dif_c = _diff_call(pdx, qtx, vtx, pdc, qtc, vtc, diff_lambda[l], g_diff_norm[l], lambda_init,
                                  ctx_out)

        hx = _ffn_call(hx, mod_x, g_ffn2[l], w2i, w2o, l, 6, mix=(gla_x, swa_x, dif_x, wo),
                       g_final=g_final if last else None)
        if ctx_out:
            flat = lambda a: a.reshape(1, b * n_ctx, a.shape[-1])
            hc = _ffn_call(hc, mod_c, g_ffn2[l], w2i, w2o, l, 6, mix=(flat(gla_c), flat(swa_c), flat(dif_c), wo))
    return hx
```

```python
import functools
import math

import numpy as np
import jax
import jax.numpy as jnp
from jax import lax
from jax.experimental import pallas as pl
from jax.experimental.pallas import tpu as pltpu

F32 = jnp.float32
BF16 = jnp.bfloat16

D_MODEL = 1024
DEPTH = 2
GRID_W = 64
N_MOD = 9
D_FF = 2816
RMS_EPS = 1e-6
ROPE_BASE = 10000.0

GLA_HEADS = 4
GLA_DK = 32
GLA_DV = 64
GLA_GATE_RANK = 16
GLA_TAU = 16.0
SWA_HEADS = 8
SWA_KV_HEADS = 2
SWA_HD = 64
SWA_WINDOW = 128
SWA_BLOCK = 128
DIFF_HEADS = 4
DIFF_QK = 32
DIFF_V = 64
DIFF_QBLOCK = 512

GLA_QK_W = GLA_HEADS * GLA_DK
GLA_WIDTH = GLA_HEADS * GLA_DV
SWA_WIDTH = SWA_HEADS * SWA_HD
SWA_KV_W = SWA_KV_HEADS * SWA_HD
DIFF_QK_W = DIFF_HEADS * 2 * DIFF_QK
DIFF_WIDTH = DIFF_HEADS * DIFF_V
MIX_WIDTH = GLA_WIDTH + SWA_WIDTH + DIFF_WIDTH
IN_SIZES = (GLA_QK_W, GLA_QK_W, GLA_WIDTH, GLA_GATE_RANK, GLA_GATE_RANK, GLA_WIDTH,
            SWA_WIDTH, SWA_KV_W, SWA_KV_W, DIFF_QK_W, DIFF_QK_W, DIFF_WIDTH)

LANE = 128
VMEM_LIMIT = 56 * 1024 * 1024
LOG2E = math.log2(math.e)

PG_W = 896
PS_COLS = 896
PS_W = 768
PD_COLS = 768
PD_W = 512
P_W = PG_W + PS_COLS + PD_COLS
GLA_CHUNK = 64
GLA_LEVELS = 6
GLA_GROUP = 4
GLA_GROUP_X = 8
GLA_SCAN_UNROLL = 4
SWA_PIPE_DEPTH = 2
MOD_ROWS = 16
MOD_TN = 1152
FFN_TM = 512
FFN_TF = D_FF


def _dot(a, b):
    return jnp.dot(a, b, preferred_element_type=F32)


def _dot_nt(a, b):
    return lax.dot_general(a, b, (((1,), (1,)), ((), ())), preferred_element_type=F32)


def _dot_tn(a, b):
    return lax.dot_general(a, b, (((0,), (0,)), ((), ())), preferred_element_type=F32)


def _split_bf16(x):
    hi = x.astype(BF16)
    lo = (x - hi.astype(F32)).astype(BF16)
    return hi, lo


def _rms(x, g):
    return x * lax.rsqrt(jnp.mean(x * x, axis=-1, keepdims=True) + RMS_EPS) * g


def _silu(x):
    return x * (1.0 / (1.0 + jnp.exp(-x)))


def _params(n_grid):
    return pltpu.CompilerParams(dimension_semantics=("arbitrary",) * n_grid, vmem_limit_bytes=VMEM_LIMIT)


def _const_spec(shape, n_grid, single=False):
    zeros = (0,) * len(shape)
    index_map = {1: lambda a: zeros, 2: lambda a, b: zeros}[n_grid]
    if single:
        return pl.BlockSpec(shape, index_map, pipeline_mode=pl.Buffered(1))
    return pl.BlockSpec(shape, index_map)


def _layer_spec(shape, layer):
    return pl.BlockSpec((None,) + tuple(shape), lambda b, i: (layer, 0, 0), pipeline_mode=pl.Buffered(1))


def _mod_kernel(a_ref, w_ref, b_ref, o_ref):
    a = _silu(a_ref[...]).astype(BF16)
    o_ref[0] = _dot(a, w_ref[0].astype(BF16)) + b_ref[0]


def _mod_call(cs, w_mod, b_mod):
    n_layers, d, n = w_mod.shape
    return pl.pallas_call(
        _mod_kernel,
        grid=(n_layers, n // MOD_TN),
        in_specs=[pl.BlockSpec((MOD_ROWS, d), lambda l, j: (0, 0)),
                  pl.BlockSpec((1, d, MOD_TN), lambda l, j: (l, 0, j)),
                  pl.BlockSpec((1, 1, MOD_TN), lambda l, j: (l, 0, j))],
        out_specs=pl.BlockSpec((1, MOD_ROWS, MOD_TN), lambda l, j: (l, 0, j)),
        out_shape=jax.ShapeDtypeStruct((n_layers, MOD_ROWS, n), F32),
        compiler_params=_params(2),
        name="mod",
    )(cs, w_mod, b_mod.reshape(n_layers, 1, n))


def _ffn_kernel(*refs, mod_off, mix, final):
    refs = list(refs)
    h_ref, mod_ref, g_ref, win_ref, wout_ref = refs[:5]
    o_ref = refs[-1]
    mod = mod_ref[0]
    if mix:
        yg_ref, ys_ref, yd_ref, wmix_ref = refs[5:9]
    if final:
        gfin_ref = refs[-2]
    shift, scale, gate = mod[mod_off:mod_off + 1], mod[mod_off + 1:mod_off + 2], mod[mod_off + 2:mod_off + 3]
    rows = h_ref.shape[1] // 2
    halves = [slice(0, rows), slice(rows, 2 * rows)]
    n_chunks = D_FF // FFN_TF

    def prologue(rs):
        x = h_ref[0, rs, :]
        if mix:
            a = GLA_WIDTH
            b = GLA_WIDTH + SWA_WIDTH
            proj = (_dot(yg_ref[0, rs, :], wmix_ref[0:a, :]) + _dot(ys_ref[0, rs, :], wmix_ref[a:b, :])
                    + _dot(yd_ref[0, rs, :], wmix_ref[b:, :]))
            x = x + mod[5:6] * proj
        return x, (_rms(x, g_ref[...]) * (1.0 + scale) + shift).astype(BF16)

    def chunk(y, c):
        gt = _dot(y, win_ref[:, c * FFN_TF:(c + 1) * FFN_TF])
        up = _dot(y, win_ref[:, D_FF + c * FFN_TF:D_FF + (c + 1) * FFN_TF])
        act = (_silu(gt) * up).astype(BF16)
        return _dot(act, wout_ref[c * FFN_TF:(c + 1) * FFN_TF, :])

    def epilogue(rs, x, acc):
        out = x + (0.5 * gate) * acc
        if final:
            out = _rms(out, gfin_ref[...])
        o_ref[0, rs, :] = out

    xa, ya = prologue(halves[0])
    acc_a = chunk(ya, 0)
    xb, yb = prologue(halves[1])
    for c in range(1, n_chunks):
        acc_a = acc_a + chunk(ya, c)
    acc_b = chunk(yb, 0)
    epilogue(halves[0], xa, acc_a)
    for c in range(1, n_chunks):
        acc_b = acc_b + chunk(yb, c)
    epilogue(halves[1], xb, acc_b)


def _ffn_call(h, mod, g, w_in, w_out, layer, mod_off, mix=None, g_final=None):
    bx, t, d = h.shape
    tm = min(FFN_TM, t)
    final = g_final is not None
    tile = lambda width: pl.BlockSpec((1, tm, width), lambda b, i: (b, i, 0))
    in_specs = [tile(d),
                pl.BlockSpec((1, N_MOD, d), lambda b, i: (b, 0, 0)),
                _const_spec((1, d), 2),
                _layer_spec((d, 2 * D_FF), layer),
                _layer_spec((D_FF, d), layer)]
    args = [h, mod, g.reshape(1, d), w_in, w_out]
    if mix is not None:
        in_specs += [tile(GLA_WIDTH), tile(SWA_WIDTH), tile(DIFF_WIDTH), _layer_spec((MIX_WIDTH, d), layer)]
        args += list(mix)
    if final:
        in_specs.append(_const_spec((1, d), 2))
        args.append(g_final.reshape(1, d))
    return pl.pallas_call(
        functools.partial(_ffn_kernel, mod_off=mod_off, mix=mix is not None, final=final),
        grid=(bx, t // tm),
        in_specs=in_specs,
        out_specs=pl.BlockSpec((1, tm, d), lambda b, i: (b, i, 0)),
        out_shape=jax.ShapeDtypeStruct(h.shape, F32),
        compiler_params=_params(2),
        name="ffn",
    )(*args)


SWA_QSCALE = SWA_HD ** -0.5 * LOG2E
DIFF_QSCALE = DIFF_QK ** -0.5 * LOG2E


def _rot_half(blk, quarter, first):
    return jnp.where(first, pltpu.roll(blk, LANE - quarter, 1), pltpu.roll(blk, quarter, 1))


def _inproj_kernel(*refs, rope):
    if rope:
        h_ref, mod_ref, g_ref, w_ref, tab_ref, pg_ref, ps_ref, pd_ref, kts_ref, kt_ref = refs
    else:
        h_ref, mod_ref, g_ref, w_ref, pg_ref, ps_ref, pd_ref, kts_ref, kt_ref = refs
    x = h_ref[0]
    mod = mod_ref[0]
    y = (_rms(x, g_ref[...]) * (1.0 + mod[4:5]) + mod[3:4]).astype(BF16)
    p = _dot(y, w_ref[...])
    pg_ref[0] = p[:, :PG_W]
    swa = [p[:, PG_W + i * LANE:PG_W + (i + 1) * LANE] for i in range(PS_COLS // LANE)]
    dif = [p[:, PG_W + PS_COLS + i * LANE:PG_W + PS_COLS + (i + 1) * LANE] for i in range(PD_COLS // LANE)]
    if rope:
        tab = [tab_ref[:, i * LANE:(i + 1) * LANE] for i in range(8)]
        lane = lax.broadcasted_iota(jnp.int32, (1, LANE), 1)
        first_s = (lane % (SWA_HD // 2)) < (SWA_HD // 4)
        first_d = (lane % (DIFF_QK // 2)) < (DIFF_QK // 4)
        for i in range(5):
            c, s = (tab[0], tab[1]) if i < 4 else (tab[2], tab[3])
            swa[i] = swa[i] * c + _rot_half(swa[i], SWA_HD // 4, first_s) * s
        for i in range(4):
            c, s = (tab[4], tab[5]) if i < 2 else (tab[6], tab[7])
            dif[i] = dif[i] * c + _rot_half(dif[i], DIFF_QK // 4, first_d) * s
    else:
        for i in range(4):
            swa[i] = swa[i] * SWA_QSCALE
        for i in range(2):
            dif[i] = dif[i] * DIFF_QSCALE
    ps_ref[0] = jnp.concatenate(swa[0:4] + swa[5:7], axis=1).astype(BF16)
    kt = swa[4].T
    k0, k1 = kt[:SWA_HD], kt[SWA_HD:]
    kts_ref[0] = jnp.concatenate([k0, k0, k1, k1], axis=0).astype(BF16)
    pd_ref[0] = jnp.concatenate(dif[0:2] + dif[4:6], axis=1).astype(BF16)
    kt_ref[0] = jnp.concatenate(dif[2:4], axis=1).T.astype(BF16)


def _inproj_call(h, mod, g, w, layer, tab=None):
    bx, t, d = h.shape
    tm = min(FFN_TM, t)
    rope = tab is not None
    in_specs = [pl.BlockSpec((1, tm, d), lambda j, b: (b, j, 0)),
                pl.BlockSpec((1, N_MOD, d), lambda j, b: (b, 0, 0)),
                _const_spec((1, d), 2),
                _layer_spec((d, P_W), layer)]
    args = [h, mod, g.reshape(1, d), w]
    if rope:
        in_specs.append(pl.BlockSpec((tm, 8 * LANE), lambda j, b: (j, 0)))
        args.append(tab)
    out_map = lambda j, b: (b, j, 0)
    return pl.pallas_call(
        functools.partial(_inproj_kernel, rope=rope),
        grid=(t // tm, bx),
        in_specs=in_specs,
        out_specs=[pl.BlockSpec((1, tm, PG_W), out_map), pl.BlockSpec((1, tm, PS_W), out_map),
                   pl.BlockSpec((1, tm, PD_W), out_map),
                   pl.BlockSpec((1, 2 * LANE, tm), lambda j, b: (b, 0, j)),
                   pl.BlockSpec((1, DIFF_QK_W, tm), lambda j, b: (b, 0, j))],
        out_shape=[jax.ShapeDtypeStruct((bx, t, PG_W), F32), jax.ShapeDtypeStruct((bx, t, PS_W), BF16),
                   jax.ShapeDtypeStruct((bx, t, PD_W), BF16),
                   jax.ShapeDtypeStruct((bx, 2 * LANE, t), BF16),
                   jax.ShapeDtypeStruct((bx, DIFF_QK_W, t), BF16)],
        compiler_params=_params(2),
        name="inproj",
    )(*args)


GLA_SROWS = (GLA_LEVELS + 2) * GLA_CHUNK
GLA_LV_DIAG = GLA_LEVELS
GLA_LV_NONE = GLA_LEVELS + 1


def _gla_constants():
    c = GLA_CHUNK
    r = np.arange(c)[:, None]
    t = np.arange(c)[None, :]
    blocks = []
    for lv in range(GLA_LEVELS):
        half = 1 << lv
        mid = (r // (2 * half)) * (2 * half) + half
        second = r >= mid
        blocks.append(np.where(second, (t >= mid) & (t <= r), (t > r) & (t < mid)))
    blocks.append(t <= r)
    blocks.append(t > r)
    fwd = np.concatenate(blocks, axis=0).astype(np.float32)
    bwd = np.concatenate([b[::-1, ::-1] for b in blocks], axis=0).astype(np.float32)
    s2 = np.stack([np.concatenate([m, m], axis=1) for m in (fwd, bwd)])
    i = np.arange(c)[:, None]
    j = np.arange(c)[None, :]
    x = np.bitwise_xor(i, j)
    lvl = np.where(j > i, GLA_LV_NONE,
                   np.where(i == j, GLA_LV_DIAG, np.floor(np.log2(np.maximum(x, 1))).astype(np.int64)))
    lv_f = np.tile(lvl, (2, 2))
    lv_b = np.tile(lvl[::-1, ::-1], (2, 2))
    return jnp.asarray(s2, BF16), jnp.asarray(np.stack([lv_f, lv_b]), jnp.int32)


def _gla_kernel(*refs, ctx_out):
    if ctx_out:
        (pgx_ref, pgc_ref, wg_ref, bg_ref, gn_ref, s2_ref, lv_ref, ox_ref, oc_ref,
         oi_ref, qc_ref, upd_ref, dec_ref, st_ref) = refs
    else:
        (pgx_ref, pgc_ref, wg_ref, bg_ref, gn_ref, s2_ref, lv_ref, ox_ref,
         oi_ref, qc_ref, upd_ref, dec_ref, st_ref) = refs
        oc_ref = None
    c = GLA_CHUNK
    grp = GLA_GROUP
    hv = GLA_HEADS * GLA_DV
    n_ctx = pgc_ref.shape[1]
    s_len = pgx_ref.shape[1]
    row = lax.broadcasted_iota(jnp.int32, (hv, GLA_QK_W), 0)
    lane = lax.broadcasted_iota(jnp.int32, (hv, GLA_QK_W), 1)
    head_qk = (row // GLA_DV) == (lane // GLA_DK)
    head_qk_b = head_qk.astype(F32).astype(BF16)
    row2 = lax.broadcasted_iota(jnp.int32, (hv, hv), 0)
    col2 = lax.broadcasted_iota(jnp.int32, (hv, hv), 1)
    head_v_b = ((row2 // GLA_DV) == (col2 // GLA_DV)).astype(F32).astype(BF16)
    gn = gn_ref[...]
    wh, wl = _split_bf16(jnp.concatenate([wg_ref[0], wg_ref[1]], axis=1))
    w3 = jnp.concatenate([wh, wh, wl], axis=0)
    bias = jnp.concatenate([bg_ref[0], bg_ref[1]], axis=1)

    lane1 = lax.broadcasted_iota(jnp.int32, (1, GLA_QK_W), 1)
    pair_mask = [((lane1 // (2 * GLA_DK)) == p).astype(F32).astype(BF16) for p in range(2)]
    parity_mask = [(((lane1 // GLA_DK) % 2) == hh).astype(F32).astype(BF16) for hh in range(2)]
    lane_v = lax.broadcasted_iota(jnp.int32, (1, hv), 1)
    hv_mask = [((lane_v // GLA_DV) == h).astype(F32).astype(BF16) for h in range(GLA_HEADS)]

    def pair_rows(a):
        a = a.astype(BF16)
        return jnp.concatenate([a * pair_mask[0], a * pair_mask[1]], axis=0)

    def parity_rows(a):
        a = a.astype(BF16)
        return jnp.concatenate([a * parity_mask[0], a * parity_mask[1]], axis=0)

    def local(src_ref, src_row, dst_row, with_out, grp):
        blks = [src_ref[0, pl.ds(pl.multiple_of(src_row + g * c, c), c), :] for g in range(grp)]
        qs = [b[:, 0:128] * (GLA_DK ** -0.5) for b in blks]
        ks = [b[:, 128:256] for b in blks]
        vs = [b[:, 256:512].astype(BF16) for b in blks]
        zh, zl = _split_bf16(jnp.concatenate([b[:, 768:896] for b in blks], axis=0))
        z = _dot(jnp.concatenate([zh, zl, zh], axis=1), w3) + bias
        gate = (jnp.minimum(z, 0.0) - jnp.log(1.0 + jnp.exp(-jnp.abs(z)))) * (1.0 / GLA_TAU)
        gh, gl = _split_bf16(gate)
        f_all = []
        for d in range(2):
            cols = slice(d * GLA_QK_W, (d + 1) * GLA_QK_W)
            g2 = jnp.concatenate([jnp.concatenate([gh[g * c:(g + 1) * c, cols], gl[g * c:(g + 1) * c, cols]], axis=0)
                                  for g in range(grp)], axis=1)
            f_all.append(jnp.exp(_dot(s2_ref[d], g2)))

        def fac(d, g, block):
            return f_all[d][block * c:(block + 1) * c, g * GLA_QK_W:(g + 1) * GLA_QK_W]

        if with_out:
            vbds = [[jnp.concatenate([v * hv_mask[2 * p], v * hv_mask[2 * p + 1]], axis=0) for p in range(2)]
                    for v in vs]
            diag = [_dot_nt(pair_rows(q), parity_rows(k)) for q, k in zip(qs, ks)]
        for d in range(2):
            for g in range(grp):
                rows = pl.ds(pl.multiple_of(dst_row + g * c, c), c)
                ci = (dst_row + g * c) // c
                upd = _dot_tn(vs[g], (ks[g] * fac(d, g, GLA_LEVELS + 1)).astype(BF16))
                upd_ref[d, ci] = jnp.where(head_qk, upd, 0.0)
                fcum = fac(d, g, GLA_LEVELS)
                last = c - 1 if d == 0 else 0
                dec_ref[d, ci] = jnp.broadcast_to(fcum[last:last + 1], (8, GLA_QK_W))
                if with_out:
                    qc_ref[d, rows, :] = (qs[g] * fcum).astype(BF16)
            if with_out:
                lv = lv_ref[d]
                atts = [jnp.where(lv == GLA_LV_DIAG, dg, 0.0) for dg in diag]
                for level in range(GLA_LEVELS):
                    for g in range(grp):
                        fl = fac(d, g, level)
                        s = _dot_nt(pair_rows(qs[g] * fl), parity_rows(ks[g] * fl))
                        atts[g] = jnp.where(lv == level, s, atts[g])
                for g in range(grp):
                    rows = pl.ds(pl.multiple_of(dst_row + g * c, c), c)
                    att = atts[g].astype(BF16)
                    oi_ref[d, rows, :] = _dot(att[:c], vbds[g][0]) + _dot(att[c:], vbds[g][1])

    def local_pass(src_ref, dst0, with_out, grp):
        def body(i, carry):
            local(src_ref, i * (grp * c), dst0 + i * (grp * c), with_out, grp)
            return carry
        lax.fori_loop(0, src_ref.shape[1] // (grp * c), body, 0)

    def scan_pass(first, n, with_out):
        def body(i, carry):
            for d in range(2):
                st = st_ref[d]
                for j in range(GLA_SCAN_UNROLL):
                    step = i * GLA_SCAN_UNROLL + j
                    ci = first + (step if d == 0 else n - 1 - step)
                    if with_out:
                        rows = pl.ds(pl.multiple_of(ci * c, c), c)
                        oi_ref[d, rows, :] = oi_ref[d, rows, :] + _dot_nt(qc_ref[d, rows, :], st.astype(BF16))
                    st = dec_ref[d, ci][0:1] * st + upd_ref[d, ci]
                st_ref[d] = st
            return carry
        lax.fori_loop(0, n // GLA_SCAN_UNROLL, body, 0)

    def finish_pass(src_ref, src0, out_ref, n_rows):
        tile = grp * c

        def body(i, carry):
            r = pl.multiple_of(i * tile, tile)
            rs = pl.ds(pl.multiple_of(src0 + r, tile), tile)
            o = oi_ref[0, rs, :] + oi_ref[1, rs, :]
            hi, lo = _split_bf16(o * o)
            ms = (_dot(hi, head_v_b) + _dot(lo, head_v_b)) * (1.0 / GLA_DV)
            og = src_ref[0, pl.ds(r, tile), 512:768]
            out_ref[0, pl.ds(r, tile), :] = (o * lax.rsqrt(ms + RMS_EPS) * gn * _silu(og)).astype(BF16)
            return carry
        lax.fori_loop(0, n_rows // tile, body, 0)

    local_pass(pgc_ref, 0, ctx_out, GLA_GROUP)
    local_pass(pgx_ref, n_ctx, True, GLA_GROUP_X)
    st_ref[...] = jnp.zeros(st_ref.shape, F32)
    scan_pass(0, n_ctx // c, ctx_out)
    scan_pass(n_ctx // c, s_len // c, True)
    finish_pass(pgx_ref, n_ctx, ox_ref, s_len)
    if ctx_out:
        finish_pass(pgc_ref, 0, oc_ref, n_ctx)


def _gla_call(pgx, pgc, wg, bg, gn, s2, lv, ctx_out):
    b, s, _ = pgx.shape
    n_ctx = pgc.shape[1]
    hv = GLA_WIDTH
    n_rows = s + n_ctx
    in_specs = [pl.BlockSpec((1, s, PG_W), lambda i: (i, 0, 0)),
                pl.BlockSpec((1, n_ctx, PG_W), lambda i: (i, 0, 0)),
                _const_spec(wg.shape, 1), _const_spec(bg.shape, 1), _const_spec((1, hv), 1),
                _const_spec(s2.shape, 1), _const_spec(lv.shape, 1)]
    out_specs = [pl.BlockSpec((1, s, hv), lambda i: (i, 0, 0))]
    out_shape = [jax.ShapeDtypeStruct((b, s, hv), BF16)]
    if ctx_out:
        out_specs.append(pl.BlockSpec((1, n_ctx, hv), lambda i: (i, 0, 0)))
        out_shape.append(jax.ShapeDtypeStruct((b, n_ctx, hv), BF16))
    scratch = [pltpu.VMEM((2, n_rows, hv), F32),
               pltpu.VMEM((2, n_rows, GLA_QK_W), BF16),
               pltpu.VMEM((2, n_rows // GLA_CHUNK, hv, GLA_QK_W), F32),
               pltpu.VMEM((2, n_rows // GLA_CHUNK, 8, GLA_QK_W), F32),
               pltpu.VMEM((2, hv, GLA_QK_W), F32)]
    res = pl.pallas_call(
        functools.partial(_gla_kernel, ctx_out=ctx_out),
        grid=(b,),
        in_specs=in_specs, out_specs=out_specs, out_shape=out_shape, scratch_shapes=scratch,
        compiler_params=_params(1),
        name="gla",
    )(pgx, pgc, wg, bg, gn.reshape(1, hv), s2, lv)
    return (res[0], res[1]) if ctx_out else (res[0], None)


def _swa_kernel(*refs, ctx_out):
    if ctx_out:
        sink_ref, px_ref, ktx_ref, pc_ref, ktc_ref, ox_ref, oc_ref, sc_ref = refs
    else:
        sink_ref, px_ref, ktx_ref, pc_ref, ktc_ref, ox_ref, sc_ref = refs
    s_len = px_ref.shape[1]
    n_ctx = pc_ref.shape[1]
    blk = SWA_BLOCK
    band = 3 * blk
    grp = SWA_HEADS // SWA_KV_HEADS
    lane = lax.broadcasted_iota(jnp.int32, (1, LANE), 1)
    low = lane < SWA_HD
    half_mask = [low.astype(F32).astype(BF16), (~low).astype(F32).astype(BF16)]
    rel = (lax.broadcasted_iota(jnp.int32, (blk, band), 0)
           - lax.broadcasted_iota(jnp.int32, (blk, band), 1))

    def col(i):
        return slice(i * LANE, (i + 1) * LANE)

    def krows(kv):
        return slice(kv * LANE, (kv + 1) * LANE)

    def sink_col(n_rows, kv, half):
        ha = grp * kv + half
        return jnp.concatenate([jnp.full((n_rows, 1), sink_ref[ha], F32),
                                jnp.full((n_rows, 1), sink_ref[ha + 2], F32)], axis=0)

    def softmax_pv(sc, v, snk):
        m = jnp.maximum(snk, jnp.max(sc, axis=-1, keepdims=True))
        p = jnp.exp2(sc - m).astype(BF16)
        r = _dot(p, jnp.concatenate([v, jnp.ones(v.shape, BF16)], axis=1))
        return r[:, :LANE] / (r[:, LANE:] + jnp.exp2(snk - m))

    def pipeline(tasks, depth, carried=(), prefetch=()):
        queue = list(carried)
        for t in range(len(queue), min(depth, len(tasks))):
            queue.append(tasks[t][0]())
        outs = []
        for t, (_, finish) in enumerate(tasks):
            ahead = t + depth
            if ahead < len(tasks):
                queue.append(tasks[ahead][0]())
            elif ahead - len(tasks) < len(prefetch):
                prefetch[ahead - len(tasks)]()
            outs.append(finish(queue[t]))
        return outs

    units = [(kv, half) for kv in range(SWA_KV_HEADS) for half in range(2)]
    blocks_per_step = 2
    n_steps = s_len // (blk * blocks_per_step)
    depth = SWA_PIPE_DEPTH

    def block_tasks(n):
        r0 = pl.multiple_of(n * blk, blk)
        start = pl.multiple_of(jnp.clip((n - 1) * blk, 0, s_len - band), blk)
        tasks = []
        for kv, half in units:
            def scores(kv=kv, half=half):
                bias = jnp.where(jnp.abs(rel + (r0 - start)) <= SWA_WINDOW, 0.0, -jnp.inf)
                bias = jnp.concatenate([bias, jnp.zeros((blk, n_ctx), F32)], axis=1)
                q = jnp.concatenate([px_ref[0, pl.ds(r0, blk), col(2 * kv)],
                                     px_ref[0, pl.ds(r0, blk), col(2 * kv + 1)]], axis=0) * half_mask[half]
                kt = jnp.concatenate([ktx_ref[0, krows(kv), pl.ds(start, band)],
                                      ktc_ref[0, krows(kv), :]], axis=1)
                return _dot(q, kt) + jnp.concatenate([bias, bias], axis=0)

            def finish(cur, kv=kv, half=half):
                v = jnp.concatenate([px_ref[0, pl.ds(start, band), col(4 + kv)],
                                     pc_ref[0, :, col(4 + kv)]], axis=0)
                return softmax_pv(cur, v, sink_col(blk, kv, half))

            tasks.append((scores, finish))
        return tasks, r0

    def prefetch_into(slot, scores):
        def run():
            sc_ref[slot] = scores()
        return run

    def body(i, carry):
        tasks, rows = [], []
        for sub in range(blocks_per_step):
            t, r0 = block_tasks(i * blocks_per_step + sub)
            tasks += t
            rows.append(r0)
        nxt, _ = block_tasks(jnp.minimum(i + 1, n_steps - 1) * blocks_per_step)
        outs = pipeline(tasks, depth, carried=[sc_ref[k] for k in range(depth)],
                        prefetch=[prefetch_into(k, nxt[k][0]) for k in range(depth)])
        for sub in range(blocks_per_step):
            for kv in range(SWA_KV_HEADS):
                t = (sub * SWA_KV_HEADS + kv) * 2
                o2 = jnp.where(low, outs[t], outs[t + 1]).astype(BF16)
                ox_ref[0, pl.ds(rows[sub], blk), col(2 * kv)] = o2[:blk]
                ox_ref[0, pl.ds(rows[sub], blk), col(2 * kv + 1)] = o2[blk:]
        return carry

    first, _ = block_tasks(jnp.int32(0))
    for k in range(depth):
        sc_ref[k] = first[k][0]()
    lax.fori_loop(0, n_steps, body, 0)

    if ctx_out:
        tasks = []
        for kv, half in units:
            def scores(kv=kv, half=half):
                q = jnp.concatenate([pc_ref[0, :, col(2 * kv)], pc_ref[0, :, col(2 * kv + 1)]], axis=0)
                return _dot(q * half_mask[half], ktc_ref[0, krows(kv), :])

            def finish(cur, kv=kv, half=half):
                return softmax_pv(cur, pc_ref[0, :, col(4 + kv)], sink_col(n_ctx, kv, half))

            tasks.append((scores, finish))
        outs = pipeline(tasks, 1)
        for kv in range(SWA_KV_HEADS):
            o2 = jnp.where(low, outs[2 * kv], outs[2 * kv + 1]).astype(BF16)
            oc_ref[0, :, col(2 * kv)] = o2[:n_ctx]
            oc_ref[0, :, col(2 * kv + 1)] = o2[n_ctx:]


def _swa_call(psx, ktx, psc, ktc, sink, ctx_out):
    b, s, _ = psx.shape
    n_ctx = psc.shape[1]
    in_specs = [pl.BlockSpec(memory_space=pltpu.SMEM),
                pl.BlockSpec((1, s, PS_W), lambda i: (i, 0, 0)),
                pl.BlockSpec((1, 2 * LANE, s), lambda i: (i, 0, 0)),
                pl.BlockSpec((1, n_ctx, PS_W), lambda i: (i, 0, 0)),
                pl.BlockSpec((1, 2 * LANE, n_ctx), lambda i: (0, 0, i))]
    out_specs = [pl.BlockSpec((1, s, SWA_WIDTH), lambda i: (i, 0, 0))]
    out_shape = [jax.ShapeDtypeStruct((b, s, SWA_WIDTH), BF16)]
    if ctx_out:
        out_specs.append(pl.BlockSpec((1, n_ctx, SWA_WIDTH), lambda i: (i, 0, 0)))
        out_shape.append(jax.ShapeDtypeStruct((b, n_ctx, SWA_WIDTH), BF16))
    res = pl.pallas_call(
        functools.partial(_swa_kernel, ctx_out=ctx_out),
        grid=(b,),
        in_specs=in_specs, out_specs=out_specs, out_shape=out_shape,
        scratch_shapes=[pltpu.VMEM((SWA_PIPE_DEPTH, 2 * SWA_BLOCK, 3 * SWA_BLOCK + n_ctx), F32)],
        compiler_params=_params(1),
        name="swa",
    )(sink, psx, ktx, psc, ktc)
    return (res[0], res[1]) if ctx_out else (res[0], None)


def _diff_kernel(*refs, ctx_out, lambda_init):
    if ctx_out:
        px_ref, ktx_ref, pc_ref, ktc_ref, lam_ref, gn_ref, ox_ref, oc_ref, kt_ref, va_ref, sc_ref = refs
    else:
        px_ref, ktx_ref, pc_ref, ktc_ref, lam_ref, gn_ref, ox_ref, kt_ref, va_ref, sc_ref = refs
    s_len = px_ref.shape[1]
    w = DIFF_QK_W
    lv = lam_ref[...]
    lam = (jnp.exp(jnp.sum(lv[0:1] * lv[1:2], axis=-1, keepdims=True))
           - jnp.exp(jnp.sum(lv[2:3] * lv[3:4], axis=-1, keepdims=True)) + lambda_init)
    lane = lax.broadcasted_iota(jnp.int32, (1, w), 1)
    row2 = lax.broadcasted_iota(jnp.int32, (w, w), 0)
    col2 = lax.broadcasted_iota(jnp.int32, (w, w), 1)
    head_ones = ((row2 // DIFF_V) == (col2 // DIFF_V)).astype(F32).astype(BF16)
    gn = gn_ref[...] * (1.0 - lambda_init)
    unit_masks = [((lane >= DIFF_QK * u) & (lane < DIFF_QK * (u + 1))).astype(F32).astype(BF16)
                  for u in range(2 * DIFF_HEADS)]
    head_masks = [(lane >= DIFF_V * h) & (lane < DIFF_V * (h + 1)) for h in range(DIFF_HEADS)]

    kt_ref[:, :s_len] = ktx_ref[0]
    kt_ref[:, s_len:] = ktc_ref[0]
    for side in range(2):
        keep = ((lane >= side * (w // 2)) & (lane < (side + 1) * (w // 2))).astype(F32).astype(BF16)
        va_ref[side, :s_len, :] = px_ref[0, :, w:2 * w] * keep + (1.0 - keep)
        va_ref[side, s_len:, :] = pc_ref[0, :, w:2 * w] * keep + (1.0 - keep)

    def attend(q, k0, q_next=None):
        def scores(qq, u):
            return _dot(qq * unit_masks[u], kt_ref[:, k0:])

        def softmax_pv(sc, u):
            m = jnp.max(sc, axis=-1, keepdims=True)
            return _dot(jnp.exp2(sc - m).astype(BF16), va_ref[(u // 2) // (DIFF_HEADS // 2), k0:, :])

        units = []
        nxt = scores(q, 0) if q_next is None else sc_ref[...]
        for u in range(2 * DIFF_HEADS):
            cur = nxt
            if u + 1 < 2 * DIFF_HEADS:
                nxt = scores(q, u + 1)
            elif q_next is not None:
                sc_ref[...] = scores(q_next, 0)
            units.append(softmax_pv(cur, u))
        o = jnp.zeros((q.shape[0], w), F32)
        for h in range(DIFF_HEADS):
            r1, r2 = units[2 * h], units[2 * h + 1]
            oh = r1 / pltpu.roll(r1, w // 2, 1) - lam * (r2 / pltpu.roll(r2, w // 2, 1))
            o = jnp.where(head_masks[h], oh, o)
        hi, lo = _split_bf16(o * o)
        ms = (_dot(hi, head_ones) + _dot(lo, head_ones)) * (1.0 / DIFF_V)
        return (o * lax.rsqrt(ms + RMS_EPS) * gn).astype(BF16)

    n_blocks = s_len // DIFF_QBLOCK

    def q_block(n):
        return px_ref[0, pl.ds(pl.multiple_of(n * DIFF_QBLOCK, DIFF_QBLOCK), DIFF_QBLOCK), 0:w]

    def body(n, carry):
        out = attend(q_block(n), 0, q_block(jnp.minimum(n + 1, n_blocks - 1)))
        ox_ref[0, pl.ds(pl.multiple_of(n * DIFF_QBLOCK, DIFF_QBLOCK), DIFF_QBLOCK), :] = out
        return carry

    sc_ref[...] = _dot(q_block(0) * unit_masks[0], kt_ref[...])
    lax.fori_loop(0, n_blocks, body, 0)
    if ctx_out:
        oc_ref[0] = attend(pc_ref[0, :, 0:w], s_len)


def _diff_call(pdx, ktx, pdc, ktc, lam, gn, lambda_init, ctx_out):
    b, s, _ = pdx.shape
    n_ctx = pdc.shape[1]
    in_specs = [pl.BlockSpec((1, s, PD_W), lambda i: (i, 0, 0)),
                pl.BlockSpec((1, DIFF_QK_W, s), lambda i: (i, 0, 0)),
                pl.BlockSpec((1, n_ctx, PD_W), lambda i: (i, 0, 0)),
                pl.BlockSpec((1, DIFF_QK_W, n_ctx), lambda i: (0, 0, i)),
                _const_spec(lam.shape, 1), _const_spec((1, DIFF_WIDTH), 1)]
    out_specs = [pl.BlockSpec((1, s, DIFF_WIDTH), lambda i: (i, 0, 0))]
    out_shape = [jax.ShapeDtypeStruct((b, s, DIFF_WIDTH), BF16)]
    if ctx_out:
        out_specs.append(pl.BlockSpec((1, n_ctx, DIFF_WIDTH), lambda i: (i, 0, 0)))
        out_shape.append(jax.ShapeDtypeStruct((b, n_ctx, DIFF_WIDTH), BF16))
    res = pl.pallas_call(
        functools.partial(_diff_kernel, ctx_out=ctx_out, lambda_init=lambda_init),
        grid=(b,),
        in_specs=in_specs, out_specs=out_specs, out_shape=out_shape,
        scratch_shapes=[pltpu.VMEM((DIFF_QK_W, s + n_ctx), BF16),
                        pltpu.VMEM((2, s + n_ctx, DIFF_WIDTH), BF16),
                        pltpu.VMEM((DIFF_QBLOCK, s + n_ctx), F32)],
        compiler_params=_params(1),
        name="diff",
    )(pdx, ktx, pdc, ktc, lam, gn.reshape(1, DIFF_WIDTH))
    return (res[0], res[1]) if ctx_out else (res[0], None)


def _prep_w_in(w):
    w = w.astype(BF16)
    starts = np.concatenate([[0], np.cumsum(IN_SIZES)])
    gq, gk, gv, gf, gb, og, sq, sk, sv, dq, dk, dv = range(len(IN_SIZES))

    def cols(first, last):
        return w[..., starts[first]:starts[last + 1]]

    pad = jnp.zeros(w.shape[:-1] + (PG_W - (2 * GLA_QK_W + 2 * GLA_WIDTH + 2 * GLA_GATE_RANK),), BF16)
    v0 = w[..., starts[sv]:starts[sv] + SWA_HD]
    v1 = w[..., starts[sv] + SWA_HD:starts[sv + 1]]
    return jnp.concatenate([cols(gq, gv), cols(og, og), cols(gf, gb), pad, cols(sq, sk), v0, v0, v1, v1,
                            cols(dq, dv)], axis=-1)


def _prep_gate(w_gate, b_gate):
    wg = jnp.zeros((2, LANE, GLA_QK_W), F32)
    for d in range(2):
        wg = wg.at[d, GLA_GATE_RANK * d:GLA_GATE_RANK * (d + 1), :].set(w_gate[d])
    return wg, b_gate.reshape(2, 1, GLA_QK_W)


def _axial_angles(rows, head_dim):
    half = head_dim // 2
    row = np.repeat(np.arange(rows, dtype=np.float64), GRID_W)
    col = np.tile(np.arange(GRID_W, dtype=np.float64), rows)
    inv_freq = 1.0 / (ROPE_BASE ** (np.arange(0, half, 2, dtype=np.float64) / half))

    def axis_angles(pos):
        a = pos[:, None] * inv_freq[None, :]
        return np.concatenate([a, a], axis=-1)

    return np.concatenate([axis_angles(row), axis_angles(col)], axis=-1)


def _rope_table(seq):
    rows = seq // GRID_W
    blocks = []
    for head_dim, qscale in ((SWA_HD, SWA_QSCALE), (DIFF_QK, DIFF_QSCALE)):
        ang = _axial_angles(rows, head_dim)
        quarter = head_dim // 4
        sign = np.where((np.arange(head_dim) % (2 * quarter)) < quarter, -1.0, 1.0)
        reps = LANE // head_dim
        cos = np.tile(np.cos(ang), (1, reps))
        sin = np.tile(np.sin(ang) * sign[None, :], (1, reps))
        blocks += [cos * qscale, sin * qscale, cos, sin]
    return jnp.asarray(np.concatenate(blocks, axis=1), F32)


def kernel(x, c, ctx, c_ctx, w_mod, b_mod, g_ffn1, w_ffn1_in, w_ffn1_out, g_mix, w_in, w_out, w_gla_gate,
           b_gla_gate, g_gla_norm, swa_sink, diff_lambda, g_diff_norm, g_ffn2, w_ffn2_in, w_ffn2_out, g_final):
    b, s, d = x.shape
    n_ctx = ctx.shape[1]
    depth = w_mod.shape[0]

    cs = jnp.zeros((MOD_ROWS, d), F32).at[:b].set(c).at[b].set(c_ctx)
    mod = _mod_call(cs, w_mod, b_mod).reshape(depth, MOD_ROWS, N_MOD, d)
    tab = _rope_table(s)
    s2, lv = _gla_constants()

    w1i, w1o = w_ffn1_in.astype(BF16), w_ffn1_out.astype(BF16)
    w2i, w2o = w_ffn2_in.astype(BF16), w_ffn2_out.astype(BF16)
    wi = _prep_w_in(w_in)
    wo = w_out.astype(BF16)

    hx = x
    hc = ctx.reshape(1, b * n_ctx, d)
    for l in range(depth):
        ctx_out = l < depth - 1
        last = l == depth - 1
        mod_x = mod[l, :b]
        mod_c = mod[l, b:b + 1]
        lambda_init = 0.8 - 0.6 * math.exp(-0.3 * l)
        wg, bg = _prep_gate(w_gla_gate[l], b_gla_gate[l])
        sink = swa_sink[l] * LOG2E

        hx = _ffn_call(hx, mod_x, g_ffn1[l], w1i, w1o, l, 0)
        hc = _ffn_call(hc, mod_c, g_ffn1[l], w1i, w1o, l, 0)
        pgx, psx, pdx, ksx, ktx = _inproj_call(hx, mod_x, g_mix[l], wi, l, tab)
        pgc, psc, pdc, ksc, ktc = _inproj_call(hc, mod_c, g_mix[l], wi, l)
        pgc, psc, pdc = (a.reshape(b, n_ctx, a.shape[-1]) for a in (pgc, psc, pdc))

        gla_x, gla_c = _gla_call(pgx, pgc, wg, bg, g_gla_norm[l], s2, lv, ctx_out)
        swa_x, swa_c = _swa_call(psx, ksx, psc, ksc, sink, ctx_out)
        dif_x, dif_c = _diff_call(pdx, ktx, pdc, ktc, diff_lambda[l], g_diff_norm[l], lambda_init, ctx_out)

        hx = _ffn_call(hx, mod_x, g_ffn2[l], w2i, w2o, l, 6, mix=(gla_x, swa_x, dif_x, wo),
                       g_final=g_final if last else None)
        if ctx_out:
            flat = lambda a: a.reshape(1, b * n_ctx, a.shape[-1])
            hc = _ffn_call(hc, mod_c, g_ffn2[l], w2i, w2o, l, 6, mix=(flat(gla_c), flat(swa_c), flat(dif_c), wo))
    return hx
```

```python
import functools
import math

import numpy as np
import jax
import jax.numpy as jnp
from jax import lax
from jax.experimental import pallas as pl
from jax.experimental.pallas import tpu as pltpu

F32 = jnp.float32
BF16 = jnp.bfloat16

D_MODEL = 1024
DEPTH = 2
GRID_W = 64
N_MOD = 9
D_FF = 2816
RMS_EPS = 1e-6
ROPE_BASE = 10000.0

GLA_HEADS = 4
GLA_DK = 32
GLA_DV = 64
GLA_GATE_RANK = 16
GLA_TAU = 16.0
SWA_HEADS = 8
SWA_KV_HEADS = 2
SWA_HD = 64
SWA_WINDOW = 128
SWA_BLOCK = 128
DIFF_HEADS = 4
DIFF_QK = 32
DIFF_V = 64
DIFF_QBLOCK = 512

GLA_QK_W = GLA_HEADS * GLA_DK
GLA_WIDTH = GLA_HEADS * GLA_DV
SWA_WIDTH = SWA_HEADS * SWA_HD
SWA_KV_W = SWA_KV_HEADS * SWA_HD
DIFF_QK_W = DIFF_HEADS * 2 * DIFF_QK
DIFF_WIDTH = DIFF_HEADS * DIFF_V
MIX_WIDTH = GLA_WIDTH + SWA_WIDTH + DIFF_WIDTH
IN_SIZES = (GLA_QK_W, GLA_QK_W, GLA_WIDTH, GLA_GATE_RANK, GLA_GATE_RANK, GLA_WIDTH,
            SWA_WIDTH, SWA_KV_W, SWA_KV_W, DIFF_QK_W, DIFF_QK_W, DIFF_WIDTH)

LANE = 128
VMEM_LIMIT = 56 * 1024 * 1024
LOG2E = math.log2(math.e)

PG_W = 896
PS_COLS = 896
PS_W = 768
PD_COLS = 768
PD_W = 512
P_W = PG_W + PS_COLS + PD_COLS
GLA_CHUNK = 64
GLA_LEVELS = 6
GLA_GROUP = 4
GLA_GROUP_X = 8
GLA_SCAN_UNROLL = 4
SWA_PIPE_DEPTH = 2
MOD_ROWS = 16
MOD_TN = 1152
FFN_TM = 512
FFN_STEP = 1024
FFN_ROWS = 256


def _dot(a, b):
    return jnp.dot(a, b, preferred_element_type=F32)


def _dot_nt(a, b):
    return lax.dot_general(a, b, (((1,), (1,)), ((), ())), preferred_element_type=F32)


def _dot_tn(a, b):
    return lax.dot_general(a, b, (((0,), (0,)), ((), ())), preferred_element_type=F32)


def _split_bf16(x):
    hi = x.astype(BF16)
    lo = (x - hi.astype(F32)).astype(BF16)
    return hi, lo


def _rms(x, g):
    return x * lax.rsqrt(jnp.mean(x * x, axis=-1, keepdims=True) + RMS_EPS) * g


def _silu(x):
    return x * (1.0 / (1.0 + jnp.exp(-x)))


def _params(n_grid):
    return pltpu.CompilerParams(dimension_semantics=("arbitrary",) * n_grid, vmem_limit_bytes=VMEM_LIMIT)


def _const_spec(shape, n_grid, single=False):
    zeros = (0,) * len(shape)
    index_map = {1: lambda a: zeros, 2: lambda a, b: zeros}[n_grid]
    if single:
        return pl.BlockSpec(shape, index_map, pipeline_mode=pl.Buffered(1))
    return pl.BlockSpec(shape, index_map)


def _layer_spec(shape, layer):
    return pl.BlockSpec((None,) + tuple(shape), lambda b, i: (layer, 0, 0), pipeline_mode=pl.Buffered(1))


def _mod_kernel(a_ref, w_ref, b_ref, o_ref):
    a = _silu(a_ref[...]).astype(BF16)
    o_ref[0] = _dot(a, w_ref[0].astype(BF16)) + b_ref[0]


def _mod_call(cs, w_mod, b_mod):
    n_layers, d, n = w_mod.shape
    return pl.pallas_call(
        _mod_kernel,
        grid=(n_layers, n // MOD_TN),
        in_specs=[pl.BlockSpec((MOD_ROWS, d), lambda l, j: (0, 0)),
                  pl.BlockSpec((1, d, MOD_TN), lambda l, j: (l, 0, j)),
                  pl.BlockSpec((1, 1, MOD_TN), lambda l, j: (l, 0, j))],
        out_specs=pl.BlockSpec((1, MOD_ROWS, MOD_TN), lambda l, j: (l, 0, j)),
        out_shape=jax.ShapeDtypeStruct((n_layers, MOD_ROWS, n), F32),
        compiler_params=_params(2),
        name="mod",
    )(cs, w_mod, b_mod.reshape(n_layers, 1, n))


def _ffn_kernel(*refs, mod_off, mix, final):
    refs = list(refs)
    h_ref, mod_ref, g_ref, win_ref, wout_ref = refs[:5]
    o_ref = refs[-1]
    mod = mod_ref[0]
    if mix:
        yg_ref, ys_ref, yd_ref, wmix_ref = refs[5:9]
    if final:
        gfin_ref = refs[-2]
    shift, scale, gate = mod[mod_off:mod_off + 1], mod[mod_off + 1:mod_off + 2], mod[mod_off + 2:mod_off + 3]
    parts = [slice(r, r + FFN_ROWS) for r in range(0, h_ref.shape[1], FFN_ROWS)]

    def prologue(rs):
        x = h_ref[0, rs, :]
        if mix:
            a = GLA_WIDTH
            b = GLA_WIDTH + SWA_WIDTH
            proj = (_dot(yg_ref[0, rs, :], wmix_ref[0:a, :]) + _dot(ys_ref[0, rs, :], wmix_ref[a:b, :])
                    + _dot(yd_ref[0, rs, :], wmix_ref[b:, :]))
            x = x + mod[5:6] * proj
        return x, (_rms(x, g_ref[...]) * (1.0 + scale) + shift).astype(BF16)

    def swiglu(y):
        gt = _dot(y, win_ref[:, :D_FF])
        up = _dot(y, win_ref[:, D_FF:])
        return _dot((_silu(gt) * up).astype(BF16), wout_ref[...])

    def epilogue(rs, x, acc):
        out = x + (0.5 * gate) * acc
        if final:
            out = _rms(out, gfin_ref[...])
        o_ref[0, rs, :] = out

    x, y = prologue(parts[0])
    for r, rs in enumerate(parts):
        acc = swiglu(y)
        x_cur = x
        if r + 1 < len(parts):
            x, y = prologue(parts[r + 1])
        epilogue(rs, x_cur, acc)


def _ffn_call(h, mod, g, w_in, w_out, layer, mod_off, mix=None, g_final=None):
    bx, t, d = h.shape
    tm = min(FFN_STEP, t)
    final = g_final is not None
    tile = lambda width: pl.BlockSpec((1, tm, width), lambda b, i: (b, i, 0))
    in_specs = [tile(d),
                pl.BlockSpec((1, N_MOD, d), lambda b, i: (b, 0, 0)),
                _const_spec((1, d), 2),
                _layer_spec((d, 2 * D_FF), layer),
                _layer_spec((D_FF, d), layer)]
    args = [h, mod, g.reshape(1, d), w_in, w_out]
    if mix is not None:
        in_specs += [tile(GLA_WIDTH), tile(SWA_WIDTH), tile(DIFF_WIDTH), _layer_spec((MIX_WIDTH, d), layer)]
        args += list(mix)
    if final:
        in_specs.append(_const_spec((1, d), 2))
        args.append(g_final.reshape(1, d))
    return pl.pallas_call(
        functools.partial(_ffn_kernel, mod_off=mod_off, mix=mix is not None, final=final),
        grid=(bx, t // tm),
        in_specs=in_specs,
        out_specs=pl.BlockSpec((1, tm, d), lambda b, i: (b, i, 0)),
        out_shape=jax.ShapeDtypeStruct(h.shape, F32),
        compiler_params=_params(2),
        name="ffn",
    )(*args)


SWA_QSCALE = SWA_HD ** -0.5 * LOG2E
DIFF_QSCALE = DIFF_QK ** -0.5 * LOG2E


def _rot_half(blk, quarter, first):
    return jnp.where(first, pltpu.roll(blk, LANE - quarter, 1), pltpu.roll(blk, quarter, 1))


def _inproj_kernel(*refs, rope):
    if rope:
        h_ref, mod_ref, g_ref, w_ref, tab_ref, pg_ref, ps_ref, pd_ref, kts_ref, kt_ref = refs
    else:
        h_ref, mod_ref, g_ref, w_ref, pg_ref, ps_ref, pd_ref, kts_ref, kt_ref = refs
    mod = mod_ref[0]
    rows = h_ref.shape[1] // 2
    halves = [slice(0, rows), slice(rows, 2 * rows)]

    def project(rs):
        y = (_rms(h_ref[0, rs, :], g_ref[...]) * (1.0 + mod[4:5]) + mod[3:4]).astype(BF16)
        return _dot(y, w_ref[...])

    def emit(rs, p):
        pg_ref[0, rs, :] = p[:, :PG_W]
        swa = [p[:, PG_W + i * LANE:PG_W + (i + 1) * LANE] for i in range(PS_COLS // LANE)]
        dif = [p[:, PG_W + PS_COLS + i * LANE:PG_W + PS_COLS + (i + 1) * LANE] for i in range(PD_COLS // LANE)]
        if rope:
            tab = [tab_ref[rs, i * LANE:(i + 1) * LANE] for i in range(8)]
            lane = lax.broadcasted_iota(jnp.int32, (1, LANE), 1)
            first_s = (lane % (SWA_HD // 2)) < (SWA_HD // 4)
            first_d = (lane % (DIFF_QK // 2)) < (DIFF_QK // 4)
            for i in range(5):
                c, s = (tab[0], tab[1]) if i < 4 else (tab[2], tab[3])
                swa[i] = swa[i] * c + _rot_half(swa[i], SWA_HD // 4, first_s) * s
            for i in range(4):
                c, s = (tab[4], tab[5]) if i < 2 else (tab[6], tab[7])
                dif[i] = dif[i] * c + _rot_half(dif[i], DIFF_QK // 4, first_d) * s
        else:
            for i in range(4):
                swa[i] = swa[i] * SWA_QSCALE
            for i in range(2):
                dif[i] = dif[i] * DIFF_QSCALE
        ps_ref[0, rs, :] = jnp.concatenate(swa[0:4] + swa[5:7], axis=1).astype(BF16)
        kt = swa[4].T
        k0, k1 = kt[:SWA_HD], kt[SWA_HD:]
        kts_ref[0, :, rs] = jnp.concatenate([k0, k0, k1, k1], axis=0).astype(BF16)
        pd_ref[0, rs, :] = jnp.concatenate(dif[0:2] + dif[4:6], axis=1).astype(BF16)
        kt_ref[0, :, rs] = jnp.concatenate(dif[2:4], axis=1).T.astype(BF16)

    pa = project(halves[0])
    pb = project(halves[1])
    emit(halves[0], pa)
    emit(halves[1], pb)


def _inproj_call(h, mod, g, w, layer, tab=None):
    bx, t, d = h.shape
    tm = min(FFN_TM, t)
    rope = tab is not None
    in_specs = [pl.BlockSpec((1, tm, d), lambda j, b: (b, j, 0)),
                pl.BlockSpec((1, N_MOD, d), lambda j, b: (b, 0, 0)),
                _const_spec((1, d), 2),
                _layer_spec((d, P_W), layer)]
    args = [h, mod, g.reshape(1, d), w]
    if rope:
        in_specs.append(pl.BlockSpec((tm, 8 * LANE), lambda j, b: (j, 0)))
        args.append(tab)
    out_map = lambda j, b: (b, j, 0)
    return pl.pallas_call(
        functools.partial(_inproj_kernel, rope=rope),
        grid=(t // tm, bx),
        in_specs=in_specs,
        out_specs=[pl.BlockSpec((1, tm, PG_W), out_map), pl.BlockSpec((1, tm, PS_W), out_map),
                   pl.BlockSpec((1, tm, PD_W), out_map),
                   pl.BlockSpec((1, 2 * LANE, tm), lambda j, b: (b, 0, j)),
                   pl.BlockSpec((1, DIFF_QK_W, tm), lambda j, b: (b, 0, j))],
        out_shape=[jax.ShapeDtypeStruct((bx, t, PG_W), F32), jax.ShapeDtypeStruct((bx, t, PS_W), BF16),
                   jax.ShapeDtypeStruct((bx, t, PD_W), BF16),
                   jax.ShapeDtypeStruct((bx, 2 * LANE, t), BF16),
                   jax.ShapeDtypeStruct((bx, DIFF_QK_W, t), BF16)],
        compiler_params=_params(2),
        name="inproj",
    )(*args)


GLA_SROWS = (GLA_LEVELS + 2) * GLA_CHUNK
GLA_LV_DIAG = GLA_LEVELS
GLA_LV_NONE = GLA_LEVELS + 1


def _gla_constants():
    c = GLA_CHUNK
    r = np.arange(c)[:, None]
    t = np.arange(c)[None, :]
    blocks = []
    for lv in range(GLA_LEVELS):
        half = 1 << lv
        mid = (r // (2 * half)) * (2 * half) + half
        second = r >= mid
        blocks.append(np.where(second, (t >= mid) & (t <= r), (t > r) & (t < mid)))
    blocks.append(t <= r)
    blocks.append(t > r)
    fwd = np.concatenate(blocks, axis=0).astype(np.float32)
    bwd = np.concatenate([b[::-1, ::-1] for b in blocks], axis=0).astype(np.float32)
    s2 = np.stack([np.concatenate([m, m], axis=1) for m in (fwd, bwd)])
    i = np.arange(c)[:, None]
    j = np.arange(c)[None, :]
    x = np.bitwise_xor(i, j)
    lvl = np.where(j > i, GLA_LV_NONE,
                   np.where(i == j, GLA_LV_DIAG, np.floor(np.log2(np.maximum(x, 1))).astype(np.int64)))
    lv_f = np.tile(lvl, (2, 2))
    lv_b = np.tile(lvl[::-1, ::-1], (2, 2))
    return jnp.asarray(s2, BF16), jnp.asarray(np.stack([lv_f, lv_b]), jnp.int32)


def _gla_kernel(*refs, ctx_out):
    if ctx_out:
        (pgx_ref, pgc_ref, wg_ref, bg_ref, gn_ref, s2_ref, lv_ref, ox_ref, oc_ref,
         oi_ref, qc_ref, upd_ref, dec_ref, st_ref, f_ref) = refs
    else:
        (pgx_ref, pgc_ref, wg_ref, bg_ref, gn_ref, s2_ref, lv_ref, ox_ref,
         oi_ref, qc_ref, upd_ref, dec_ref, st_ref, f_ref) = refs
        oc_ref = None
    c = GLA_CHUNK
    grp = GLA_GROUP
    hv = GLA_HEADS * GLA_DV
    n_ctx = pgc_ref.shape[1]
    s_len = pgx_ref.shape[1]
    row = lax.broadcasted_iota(jnp.int32, (hv, GLA_QK_W), 0)
    lane = lax.broadcasted_iota(jnp.int32, (hv, GLA_QK_W), 1)
    head_qk = (row // GLA_DV) == (lane // GLA_DK)
    head_qk_b = head_qk.astype(F32).astype(BF16)
    row2 = lax.broadcasted_iota(jnp.int32, (hv, hv), 0)
    col2 = lax.broadcasted_iota(jnp.int32, (hv, hv), 1)
    head_v_b = ((row2 // GLA_DV) == (col2 // GLA_DV)).astype(F32).astype(BF16)
    gn = gn_ref[...]
    wh, wl = _split_bf16(jnp.concatenate([wg_ref[0], wg_ref[1]], axis=1))
    w3 = jnp.concatenate([wh, wh, wl], axis=0)
    bias = jnp.concatenate([bg_ref[0], bg_ref[1]], axis=1)

    lane1 = lax.broadcasted_iota(jnp.int32, (1, GLA_QK_W), 1)
    pair_mask = [((lane1 // (2 * GLA_DK)) == p).astype(F32).astype(BF16) for p in range(2)]
    parity_mask = [(((lane1 // GLA_DK) % 2) == hh).astype(F32).astype(BF16) for hh in range(2)]
    lane_v = lax.broadcasted_iota(jnp.int32, (1, hv), 1)
    hv_mask = [((lane_v // GLA_DV) == h).astype(F32).astype(BF16) for h in range(GLA_HEADS)]

    def pair_rows(a):
        a = a.astype(BF16)
        return jnp.concatenate([a * pair_mask[0], a * pair_mask[1]], axis=0)

    def parity_rows(a):
        a = a.astype(BF16)
        return jnp.concatenate([a * parity_mask[0], a * parity_mask[1]], axis=0)

    def local(src_ref, src_row, dst_row, with_out, grp):
        def src_rows(g):
            return pl.ds(pl.multiple_of(src_row + g * c, c), c)

        def q_of(g):
            return src_ref[0, src_rows(g), 0:128] * (GLA_DK ** -0.5)

        def k_of(g):
            return src_ref[0, src_rows(g), 128:256]

        def v_of(g):
            return src_ref[0, src_rows(g), 256:512].astype(BF16)

        zh, zl = _split_bf16(jnp.concatenate([src_ref[0, src_rows(g), 768:896] for g in range(grp)], axis=0))
        z = _dot(jnp.concatenate([zh, zl, zh], axis=1), w3) + bias
        gate = (jnp.minimum(z, 0.0) - jnp.log(1.0 + jnp.exp(-jnp.abs(z)))) * (1.0 / GLA_TAU)
        gh, gl = _split_bf16(gate)
        for d in range(2):
            cols = slice(d * GLA_QK_W, (d + 1) * GLA_QK_W)
            g2 = jnp.concatenate([jnp.concatenate([gh[g * c:(g + 1) * c, cols], gl[g * c:(g + 1) * c, cols]], axis=0)
                                  for g in range(grp)], axis=1)
            f_ref[d, :, 0:grp * GLA_QK_W] = jnp.exp(_dot(s2_ref[d], g2))

        def fac(d, g, block):
            return f_ref[d, block * c:(block + 1) * c, g * GLA_QK_W:(g + 1) * GLA_QK_W]

        def vbd_of(g, p):
            v = v_of(g)
            return jnp.concatenate([v * hv_mask[2 * p], v * hv_mask[2 * p + 1]], axis=0)

        if with_out:
            diag = [_dot_nt(pair_rows(q_of(g)), parity_rows(k_of(g))) for g in range(grp)]
        for d in range(2):
            for g in range(grp):
                rows = pl.ds(pl.multiple_of(dst_row + g * c, c), c)
                ci = (dst_row + g * c) // c
                upd = _dot_tn(v_of(g), (k_of(g) * fac(d, g, GLA_LEVELS + 1)).astype(BF16))
                upd_ref[d, ci] = jnp.where(head_qk, upd, 0.0)
                fcum = fac(d, g, GLA_LEVELS)
                last = c - 1 if d == 0 else 0
                dec_ref[d, ci] = jnp.broadcast_to(fcum[last:last + 1], (8, GLA_QK_W))
                if with_out:
                    qc_ref[d, rows, :] = (q_of(g) * fcum).astype(BF16)
            if with_out:
                lv = lv_ref[d]
                atts = [jnp.where(lv == GLA_LV_DIAG, dg, 0.0) for dg in diag]
                for level in range(GLA_LEVELS):
                    for g in range(grp):
                        fl = fac(d, g, level)
                        s = _dot_nt(pair_rows(q_of(g) * fl), parity_rows(k_of(g) * fl))
                        atts[g] = jnp.where(lv == level, s, atts[g])
                for g in range(grp):
                    rows = pl.ds(pl.multiple_of(dst_row + g * c, c), c)
                    att = atts[g].astype(BF16)
                    oi_ref[d, rows, :] = _dot(att[:c], vbd_of(g, 0)) + _dot(att[c:], vbd_of(g, 1))

    def local_pass(src_ref, dst0, with_out, grp):
        def body(i, carry):
            local(src_ref, i * (grp * c), dst0 + i * (grp * c), with_out, grp)
            return carry
        lax.fori_loop(0, src_ref.shape[1] // (grp * c), body, 0)

    def scan_pass(first, n, with_out):
        def body(i, carry):
            for d in range(2):
                st = st_ref[d]
                for j in range(GLA_SCAN_UNROLL):
                    step = i * GLA_SCAN_UNROLL + j
                    ci = first + (step if d == 0 else n - 1 - step)
                    if with_out:
                        rows = pl.ds(pl.multiple_of(ci * c, c), c)
                        oi_ref[d, rows, :] = oi_ref[d, rows, :] + _dot_nt(qc_ref[d, rows, :], st.astype(BF16))
                    st = dec_ref[d, ci][0:1] * st + upd_ref[d, ci]
                st_ref[d] = st
            return carry
        lax.fori_loop(0, n // GLA_SCAN_UNROLL, body, 0)

    def finish_pass(src_ref, src0, out_ref, n_rows):
        tile = grp * c

        def body(i, carry):
            r = pl.multiple_of(i * tile, tile)
            rs = pl.ds(pl.multiple_of(src0 + r, tile), tile)
            o = oi_ref[0, rs, :] + oi_ref[1, rs, :]
            hi, lo = _split_bf16(o * o)
            ms = (_dot(hi, head_v_b) + _dot(lo, head_v_b)) * (1.0 / GLA_DV)
            og = src_ref[0, pl.ds(r, tile), 512:768]
            out_ref[0, pl.ds(r, tile), :] = (o * lax.rsqrt(ms + RMS_EPS) * gn * _silu(og)).astype(BF16)
            return carry
        lax.fori_loop(0, n_rows // tile, body, 0)

    local_pass(pgc_ref, 0, ctx_out, GLA_GROUP)
    local_pass(pgx_ref, n_ctx, True, GLA_GROUP_X)
    st_ref[...] = jnp.zeros(st_ref.shape, F32)
    scan_pass(0, n_ctx // c, ctx_out)
    scan_pass(n_ctx // c, s_len // c, True)
    finish_pass(pgx_ref, n_ctx, ox_ref, s_len)
    if ctx_out:
        finish_pass(pgc_ref, 0, oc_ref, n_ctx)


def _gla_call(pgx, pgc, wg, bg, gn, s2, lv, ctx_out):
    b, s, _ = pgx.shape
    n_ctx = pgc.shape[1]
    hv = GLA_WIDTH
    n_rows = s + n_ctx
    in_specs = [pl.BlockSpec((1, s, PG_W), lambda i: (i, 0, 0)),
                pl.BlockSpec((1, n_ctx, PG_W), lambda i: (i, 0, 0)),
                _const_spec(wg.shape, 1), _const_spec(bg.shape, 1), _const_spec((1, hv), 1),
                _const_spec(s2.shape, 1), _const_spec(lv.shape, 1)]
    out_specs = [pl.BlockSpec((1, s, hv), lambda i: (i, 0, 0))]
    out_shape = [jax.ShapeDtypeStruct((b, s, hv), BF16)]
    if ctx_out:
        out_specs.append(pl.BlockSpec((1, n_ctx, hv), lambda i: (i, 0, 0)))
        out_shape.append(jax.ShapeDtypeStruct((b, n_ctx, hv), BF16))
    scratch = [pltpu.VMEM((2, n_rows, hv), F32),
               pltpu.VMEM((2, n_rows, GLA_QK_W), BF16),
               pltpu.VMEM((2, n_rows // GLA_CHUNK, hv, GLA_QK_W), F32),
               pltpu.VMEM((2, n_rows // GLA_CHUNK, 8, GLA_QK_W), F32),
               pltpu.VMEM((2, hv, GLA_QK_W), F32),
               pltpu.VMEM((2, GLA_SROWS, max(GLA_GROUP, GLA_GROUP_X) * GLA_QK_W), F32)]
    res = pl.pallas_call(
        functools.partial(_gla_kernel, ctx_out=ctx_out),
        grid=(b,),
        in_specs=in_specs, out_specs=out_specs, out_shape=out_shape, scratch_shapes=scratch,
        compiler_params=_params(1),
        name="gla",
    )(pgx, pgc, wg, bg, gn.reshape(1, hv), s2, lv)
    return (res[0], res[1]) if ctx_out else (res[0], None)


def _swa_kernel(*refs, ctx_out):
    if ctx_out:
        sink_ref, px_ref, ktx_ref, pc_ref, ktc_ref, ox_ref, oc_ref, sc_ref = refs
    else:
        sink_ref, px_ref, ktx_ref, pc_ref, ktc_ref, ox_ref, sc_ref = refs
    s_len = px_ref.shape[1]
    n_ctx = pc_ref.shape[1]
    blk = SWA_BLOCK
    band = 3 * blk
    grp = SWA_HEADS // SWA_KV_HEADS
    lane = lax.broadcasted_iota(jnp.int32, (1, LANE), 1)
    low = lane < SWA_HD
    half_mask = [low.astype(F32).astype(BF16), (~low).astype(F32).astype(BF16)]
    rel = (lax.broadcasted_iota(jnp.int32, (blk, band), 0)
           - lax.broadcasted_iota(jnp.int32, (blk, band), 1))

    def col(i):
        return slice(i * LANE, (i + 1) * LANE)

    def krows(kv):
        return slice(kv * LANE, (kv + 1) * LANE)

    def sink_col(n_rows, kv, half):
        ha = grp * kv + half
        return jnp.concatenate([jnp.full((n_rows, 1), sink_ref[ha], F32),
                                jnp.full((n_rows, 1), sink_ref[ha + 2], F32)], axis=0)

    def softmax_pv(sc, v, snk):
        m = jnp.maximum(snk, jnp.max(sc, axis=-1, keepdims=True))
        p = jnp.exp2(sc - m).astype(BF16)
        r = _dot(p, jnp.concatenate([v, jnp.ones(v.shape, BF16)], axis=1))
        return r[:, :LANE] / (r[:, LANE:] + jnp.exp2(snk - m))

    def pipeline(tasks, depth, carried=(), prefetch=()):
        queue = list(carried)
        for t in range(len(queue), min(depth, len(tasks))):
            queue.append(tasks[t][0]())
        outs = []
        for t, (_, finish) in enumerate(tasks):
            ahead = t + depth
            if ahead < len(tasks):
                queue.append(tasks[ahead][0]())
            elif ahead - len(tasks) < len(prefetch):
                prefetch[ahead - len(tasks)]()
            outs.append(finish(queue[t]))
        return outs

    units = [(kv, half) for kv in range(SWA_KV_HEADS) for half in range(2)]
    blocks_per_step = 2
    n_steps = s_len // (blk * blocks_per_step)
    depth = SWA_PIPE_DEPTH

    def block_tasks(n):
        r0 = pl.multiple_of(n * blk, blk)
        start = pl.multiple_of(jnp.clip((n - 1) * blk, 0, s_len - band), blk)
        tasks = []
        for kv, half in units:
            def scores(kv=kv, half=half):
                bias = jnp.where(jnp.abs(rel + (r0 - start)) <= SWA_WINDOW, 0.0, -jnp.inf)
                bias = jnp.concatenate([bias, jnp.zeros((blk, n_ctx), F32)], axis=1)
                q = jnp.concatenate([px_ref[0, pl.ds(r0, blk), col(2 * kv)],
                                     px_ref[0, pl.ds(r0, blk), col(2 * kv + 1)]], axis=0) * half_mask[half]
                kt = jnp.concatenate([ktx_ref[0, krows(kv), pl.ds(start, band)],
                                      ktc_ref[0, krows(kv), :]], axis=1)
                return _dot(q, kt) + jnp.concatenate([bias, bias], axis=0)

            def finish(cur, kv=kv, half=half):
                v = jnp.concatenate([px_ref[0, pl.ds(start, band), col(4 + kv)],
                                     pc_ref[0, :, col(4 + kv)]], axis=0)
                return softmax_pv(cur, v, sink_col(blk, kv, half))

            tasks.append((scores, finish))
        return tasks, r0

    def prefetch_into(slot, scores):
        def run():
            sc_ref[slot] = scores()
        return run

    def body(i, carry):
        tasks, rows = [], []
        for sub in range(blocks_per_step):
            t, r0 = block_tasks(i * blocks_per_step + sub)
            tasks += t
            rows.append(r0)
        nxt, _ = block_tasks(jnp.minimum(i + 1, n_steps - 1) * blocks_per_step)
        outs = pipeline(tasks, depth, carried=[sc_ref[k] for k in range(depth)],
                        prefetch=[prefetch_into(k, nxt[k][0]) for k in range(depth)])
        for sub in range(blocks_per_step):
            for kv in range(SWA_KV_HEADS):
                t = (sub * SWA_KV_HEADS + kv) * 2
                o2 = jnp.where(low, outs[t], outs[t + 1]).astype(BF16)
                ox_ref[0, pl.ds(rows[sub], blk), col(2 * kv)] = o2[:blk]
                ox_ref[0, pl.ds(rows[sub], blk), col(2 * kv + 1)] = o2[blk:]
        return carry

    first, _ = block_tasks(jnp.int32(0))
    for k in range(depth):
        sc_ref[k] = first[k][0]()
    lax.fori_loop(0, n_steps, body, 0)

    if ctx_out:
        tasks = []
        for kv, half in units:
            def scores(kv=kv, half=half):
                q = jnp.concatenate([pc_ref[0, :, col(2 * kv)], pc_ref[0, :, col(2 * kv + 1)]], axis=0)
                return _dot(q * half_mask[half], ktc_ref[0, krows(kv), :])

            def finish(cur, kv=kv, half=half):
                return softmax_pv(cur, pc_ref[0, :, col(4 + kv)], sink_col(n_ctx, kv, half))

            tasks.append((scores, finish))
        outs = pipeline(tasks, 1)
        for kv in range(SWA_KV_HEADS):
            o2 = jnp.where(low, outs[2 * kv], outs[2 * kv + 1]).astype(BF16)
            oc_ref[0, :, col(2 * kv)] = o2[:n_ctx]
            oc_ref[0, :, col(2 * kv + 1)] = o2[n_ctx:]


def _swa_call(psx, ktx, psc, ktc, sink, ctx_out):
    b, s, _ = psx.shape
    n_ctx = psc.shape[1]
    in_specs = [pl.BlockSpec(memory_space=pltpu.SMEM),
                pl.BlockSpec((1, s, PS_W), lambda i: (i, 0, 0)),
                pl.BlockSpec((1, 2 * LANE, s), lambda i: (i, 0, 0)),
                pl.BlockSpec((1, n_ctx, PS_W), lambda i: (i, 0, 0)),
                pl.BlockSpec((1, 2 * LANE, n_ctx), lambda i: (0, 0, i))]
    out_specs = [pl.BlockSpec((1, s, SWA_WIDTH), lambda i: (i, 0, 0))]
    out_shape = [jax.ShapeDtypeStruct((b, s, SWA_WIDTH), BF16)]
    if ctx_out:
        out_specs.append(pl.BlockSpec((1, n_ctx, SWA_WIDTH), lambda i: (i, 0, 0)))
        out_shape.append(jax.ShapeDtypeStruct((b, n_ctx, SWA_WIDTH), BF16))
    res = pl.pallas_call(
        functools.partial(_swa_kernel, ctx_out=ctx_out),
        grid=(b,),
        in_specs=in_specs, out_specs=out_specs, out_shape=out_shape,
        scratch_shapes=[pltpu.VMEM((SWA_PIPE_DEPTH, 2 * SWA_BLOCK, 3 * SWA_BLOCK + n_ctx), F32)],
        compiler_params=_params(1),
        name="swa",
    )(sink, psx, ktx, psc, ktc)
    return (res[0], res[1]) if ctx_out else (res[0], None)


def _diff_kernel(*refs, ctx_out, lambda_init):
    if ctx_out:
        px_ref, ktx_ref, pc_ref, ktc_ref, lam_ref, gn_ref, ox_ref, oc_ref, kt_ref, va_ref, sc_ref = refs
    else:
        px_ref, ktx_ref, pc_ref, ktc_ref, lam_ref, gn_ref, ox_ref, kt_ref, va_ref, sc_ref = refs
    s_len = px_ref.shape[1]
    w = DIFF_QK_W
    lv = lam_ref[...]
    lam = (jnp.exp(jnp.sum(lv[0:1] * lv[1:2], axis=-1, keepdims=True))
           - jnp.exp(jnp.sum(lv[2:3] * lv[3:4], axis=-1, keepdims=True)) + lambda_init)
    lane = lax.broadcasted_iota(jnp.int32, (1, w), 1)
    row2 = lax.broadcasted_iota(jnp.int32, (w, w), 0)
    col2 = lax.broadcasted_iota(jnp.int32, (w, w), 1)
    head_ones = ((row2 // DIFF_V) == (col2 // DIFF_V)).astype(F32).astype(BF16)
    gn = gn_ref[...] * (1.0 - lambda_init)
    unit_masks = [((lane >= DIFF_QK * u) & (lane < DIFF_QK * (u + 1))).astype(F32).astype(BF16)
                  for u in range(2 * DIFF_HEADS)]
    head_masks = [(lane >= DIFF_V * h) & (lane < DIFF_V * (h + 1)) for h in range(DIFF_HEADS)]

    kt_ref[:, :s_len] = ktx_ref[0]
    kt_ref[:, s_len:] = ktc_ref[0]
    for side in range(2):
        keep = ((lane >= side * (w // 2)) & (lane < (side + 1) * (w // 2))).astype(F32).astype(BF16)
        va_ref[side, :s_len, :] = px_ref[0, :, w:2 * w] * keep + (1.0 - keep)
        va_ref[side, s_len:, :] = pc_ref[0, :, w:2 * w] * keep + (1.0 - keep)

    def attend(q, k0, q_next=None):
        def scores(qq, u):
            return _dot(qq * unit_masks[u], kt_ref[:, k0:])

        def softmax_pv(sc, u):
            m = jnp.max(sc, axis=-1, keepdims=True)
            return _dot(jnp.exp2(sc - m).astype(BF16), va_ref[(u // 2) // (DIFF_HEADS // 2), k0:, :])

        units = []
        nxt = scores(q, 0) if q_next is None else sc_ref[...]
        for u in range(2 * DIFF_HEADS):
            cur = nxt
            if u + 1 < 2 * DIFF_HEADS:
                nxt = scores(q, u + 1)
            elif q_next is not None:
                sc_ref[...] = scores(q_next, 0)
            units.append(softmax_pv(cur, u))
        o = jnp.zeros((q.shape[0], w), F32)
        for h in range(DIFF_HEADS):
            r1, r2 = units[2 * h], units[2 * h + 1]
            oh = r1 / pltpu.roll(r1, w // 2, 1) - lam * (r2 / pltpu.roll(r2, w // 2, 1))
            o = jnp.where(head_masks[h], oh, o)
        hi, lo = _split_bf16(o * o)
        ms = (_dot(hi, head_ones) + _dot(lo, head_ones)) * (1.0 / DIFF_V)
        return (o * lax.rsqrt(ms + RMS_EPS) * gn).astype(BF16)

    n_blocks = s_len // DIFF_QBLOCK

    def q_block(n):
        return px_ref[0, pl.ds(pl.multiple_of(n * DIFF_QBLOCK, DIFF_QBLOCK), DIFF_QBLOCK), 0:w]

    def body(n, carry):
        out = attend(q_block(n), 0, q_block(jnp.minimum(n + 1, n_blocks - 1)))
        ox_ref[0, pl.ds(pl.multiple_of(n * DIFF_QBLOCK, DIFF_QBLOCK), DIFF_QBLOCK), :] = out
        return carry

    sc_ref[...] = _dot(q_block(0) * unit_masks[0], kt_ref[...])
    lax.fori_loop(0, n_blocks, body, 0)
    if ctx_out:
        oc_ref[0] = attend(pc_ref[0, :, 0:w], s_len)


def _diff_call(pdx, ktx, pdc, ktc, lam, gn, lambda_init, ctx_out):
    b, s, _ = pdx.shape
    n_ctx = pdc.shape[1]
    in_specs = [pl.BlockSpec((1, s, PD_W), lambda i: (i, 0, 0)),
                pl.BlockSpec((1, DIFF_QK_W, s), lambda i: (i, 0, 0)),
                pl.BlockSpec((1, n_ctx, PD_W), lambda i: (i, 0, 0)),
                pl.BlockSpec((1, DIFF_QK_W, n_ctx), lambda i: (0, 0, i)),
                _const_spec(lam.shape, 1), _const_spec((1, DIFF_WIDTH), 1)]
    out_specs = [pl.BlockSpec((1, s, DIFF_WIDTH), lambda i: (i, 0, 0))]
    out_shape = [jax.ShapeDtypeStruct((b, s, DIFF_WIDTH), BF16)]
    if ctx_out:
        out_specs.append(pl.BlockSpec((1, n_ctx, DIFF_WIDTH), lambda i: (i, 0, 0)))
        out_shape.append(jax.ShapeDtypeStruct((b, n_ctx, DIFF_WIDTH), BF16))
    res = pl.pallas_call(
        functools.partial(_diff_kernel, ctx_out=ctx_out, lambda_init=lambda_init),
        grid=(b,),
        in_specs=in_specs, out_specs=out_specs, out_shape=out_shape,
        scratch_shapes=[pltpu.VMEM((DIFF_QK_W, s + n_ctx), BF16),
                        pltpu.VMEM((2, s + n_ctx, DIFF_WIDTH), BF16),
                        pltpu.VMEM((DIFF_QBLOCK, s + n_ctx), F32)],
        compiler_params=_params(1),
        name="diff",
    )(pdx, ktx, pdc, ktc, lam, gn.reshape(1, DIFF_WIDTH))
    return (res[0], res[1]) if ctx_out else (res[0], None)


def _prep_w_in(w):
    w = w.astype(BF16)
    starts = np.concatenate([[0], np.cumsum(IN_SIZES)])
    gq, gk, gv, gf, gb, og, sq, sk, sv, dq, dk, dv = range(len(IN_SIZES))

    def cols(first, last):
        return w[..., starts[first]:starts[last + 1]]

    pad = jnp.zeros(w.shape[:-1] + (PG_W - (2 * GLA_QK_W + 2 * GLA_WIDTH + 2 * GLA_GATE_RANK),), BF16)
    v0 = w[..., starts[sv]:starts[sv] + SWA_HD]
    v1 = w[..., starts[sv] + SWA_HD:starts[sv + 1]]
    return jnp.concatenate([cols(gq, gv), cols(og, og), cols(gf, gb), pad, cols(sq, sk), v0, v0, v1, v1,
                            cols(dq, dv)], axis=-1)


def _prep_gate(w_gate, b_gate):
    wg = jnp.zeros((2, LANE, GLA_QK_W), F32)
    for d in range(2):
        wg = wg.at[d, GLA_GATE_RANK * d:GLA_GATE_RANK * (d + 1), :].set(w_gate[d])
    return wg, b_gate.reshape(2, 1, GLA_QK_W)


def _axial_angles(rows, head_dim):
    half = head_dim // 2
    row = np.repeat(np.arange(rows, dtype=np.float64), GRID_W)
    col = np.tile(np.arange(GRID_W, dtype=np.float64), rows)
    inv_freq = 1.0 / (ROPE_BASE ** (np.arange(0, half, 2, dtype=np.float64) / half))

    def axis_angles(pos):
        a = pos[:, None] * inv_freq[None, :]
        return np.concatenate([a, a], axis=-1)

    return np.concatenate([axis_angles(row), axis_angles(col)], axis=-1)


def _rope_table(seq):
    rows = seq // GRID_W
    blocks = []
    for head_dim, qscale in ((SWA_HD, SWA_QSCALE), (DIFF_QK, DIFF_QSCALE)):
        ang = _axial_angles(rows, head_dim)
        quarter = head_dim // 4
        sign = np.where((np.arange(head_dim) % (2 * quarter)) < quarter, -1.0, 1.0)
        reps = LANE // head_dim
        cos = np.tile(np.cos(ang), (1, reps))
        sin = np.tile(np.sin(ang) * sign[None, :], (1, reps))
        blocks += [cos * qscale, sin * qscale, cos, sin]
    return jnp.asarray(np.concatenate(blocks, axis=1), F32)


def kernel(x, c, ctx, c_ctx, w_mod, b_mod, g_ffn1, w_ffn1_in, w_ffn1_out, g_mix, w_in, w_out, w_gla_gate,
           b_gla_gate, g_gla_norm, swa_sink, diff_lambda, g_diff_norm, g_ffn2, w_ffn2_in, w_ffn2_out, g_final):
    b, s, d = x.shape
    n_ctx = ctx.shape[1]
    depth = w_mod.shape[0]

    cs = jnp.zeros((MOD_ROWS, d), F32).at[:b].set(c).at[b].set(c_ctx)
    mod = _mod_call(cs, w_mod, b_mod).reshape(depth, MOD_ROWS, N_MOD, d)
    tab = _rope_table(s)
    s2, lv = _gla_constants()

    w1i, w1o = w_ffn1_in.astype(BF16), w_ffn1_out.astype(BF16)
    w2i, w2o = w_ffn2_in.astype(BF16), w_ffn2_out.astype(BF16)
    wi = _prep_w_in(w_in)
    wo = w_out.astype(BF16)

    hx = x
    hc = ctx.reshape(1, b * n_ctx, d)
    for l in range(depth):
        ctx_out = l < depth - 1
        last = l == depth - 1
        mod_x = mod[l, :b]
        mod_c = mod[l, b:b + 1]
        lambda_init = 0.8 - 0.6 * math.exp(-0.3 * l)
        wg, bg = _prep_gate(w_gla_gate[l], b_gla_gate[l])
        sink = swa_sink[l] * LOG2E

        hx = _ffn_call(hx, mod_x, g_ffn1[l], w1i, w1o, l, 0)
        hc = _ffn_call(hc, mod_c, g_ffn1[l], w1i, w1o, l, 0)
        pgx, psx, pdx, ksx, ktx = _inproj_call(hx, mod_x, g_mix[l], wi, l, tab)
        pgc, psc, pdc, ksc, ktc = _inproj_call(hc, mod_c, g_mix[l], wi, l)
        pgc, psc, pdc = (a.reshape(b, n_ctx, a.shape[-1]) for a in (pgc, psc, pdc))

        gla_x, gla_c = _gla_call(pgx, pgc, wg, bg, g_gla_norm[l], s2, lv, ctx_out)
        swa_x, swa_c = _swa_call(psx, ksx, psc, ksc, sink, ctx_out)
        dif_x, dif_c = _diff_call(pdx, ktx, pdc, ktc, diff_lambda[l], g_diff_norm[l], lambda_init, ctx_out)

        hx = _ffn_call(hx, mod_x, g_ffn2[l], w2i, w2o, l, 6, mix=(gla_x, swa_x, dif_x, wo),
                       g_final=g_final if last else None)
        if ctx_out:
            flat = lambda a: a.reshape(1, b * n_ctx, a.shape[-1])
            hc = _ffn_call(hc, mod_c, g_ffn2[l], w2i, w2o, l, 6, mix=(flat(gla_c), flat(swa_c), flat(dif_c), wo))
    return hx
```

```python
import functools
import math

import numpy as np
import jax
import jax.numpy as jnp
from jax import lax
from jax.experimental import pallas as pl
from jax.experimental.pallas import tpu as pltpu

F32 = jnp.float32
BF16 = jnp.bfloat16

D_MODEL = 1024
DEPTH = 2
GRID_W = 64
N_MOD = 9
D_FF = 2816
RMS_EPS = 1e-6
ROPE_BASE = 10000.0

GLA_HEADS = 4
GLA_DK = 32
GLA_DV = 64
GLA_GATE_RANK = 16
GLA_TAU = 16.0
SWA_HEADS = 8
SWA_KV_HEADS = 2
SWA_HD = 64
SWA_WINDOW = 128
SWA_BLOCK = 128
DIFF_HEADS = 4
DIFF_QK = 32
DIFF_V = 64
DIFF_QBLOCK = 512

GLA_QK_W = GLA_HEADS * GLA_DK
GLA_WIDTH = GLA_HEADS * GLA_DV
SWA_WIDTH = SWA_HEADS * SWA_HD
SWA_KV_W = SWA_KV_HEADS * SWA_HD
DIFF_QK_W = DIFF_HEADS * 2 * DIFF_QK
DIFF_WIDTH = DIFF_HEADS * DIFF_V
MIX_WIDTH = GLA_WIDTH + SWA_WIDTH + DIFF_WIDTH
IN_SIZES = (GLA_QK_W, GLA_QK_W, GLA_WIDTH, GLA_GATE_RANK, GLA_GATE_RANK, GLA_WIDTH,
            SWA_WIDTH, SWA_KV_W, SWA_KV_W, DIFF_QK_W, DIFF_QK_W, DIFF_WIDTH)

LANE = 128
VMEM_LIMIT = 56 * 1024 * 1024
LOG2E = math.log2(math.e)

PG_W = 896
PS_COLS = 896
PS_W = 768
PD_COLS = 768
PD_W = 512
P_W = PG_W + PS_COLS + PD_COLS
GLA_CHUNK = 64
GLA_LEVELS = 6
GLA_GROUP = 4
GLA_GROUP_X = 16
GLA_SCAN_UNROLL = 4
SWA_UNITS_PER_STEP = 8
SWA_PIPE_DEPTH = 2
MOD_ROWS = 16
MOD_TN = 1152
FFN_TM = 512
FFN_STEP = 1024
FFN_ROWS = 256


def _dot(a, b):
    return jnp.dot(a, b, preferred_element_type=F32)


def _dot_nt(a, b):
    return lax.dot_general(a, b, (((1,), (1,)), ((), ())), preferred_element_type=F32)


def _dot_tn(a, b):
    return lax.dot_general(a, b, (((0,), (0,)), ((), ())), preferred_element_type=F32)


def _split_bf16(x):
    hi = x.astype(BF16)
    lo = (x - hi.astype(F32)).astype(BF16)
    return hi, lo


def _rms(x, g):
    return x * lax.rsqrt(jnp.mean(x * x, axis=-1, keepdims=True) + RMS_EPS) * g


def _silu(x):
    return x * (1.0 / (1.0 + jnp.exp(-x)))


def _params(n_grid):
    return pltpu.CompilerParams(dimension_semantics=("arbitrary",) * n_grid, vmem_limit_bytes=VMEM_LIMIT)


def _const_spec(shape, n_grid, single=False):
    zeros = (0,) * len(shape)
    index_map = {1: lambda a: zeros, 2: lambda a, b: zeros}[n_grid]
    if single:
        return pl.BlockSpec(shape, index_map, pipeline_mode=pl.Buffered(1))
    return pl.BlockSpec(shape, index_map)


def _layer_spec(shape, layer):
    return pl.BlockSpec((None,) + tuple(shape), lambda b, i: (layer, 0, 0), pipeline_mode=pl.Buffered(1))


def _mod_kernel(a_ref, w_ref, b_ref, o_ref):
    a = _silu(a_ref[...]).astype(BF16)
    o_ref[0] = _dot(a, w_ref[0].astype(BF16)) + b_ref[0]


def _mod_call(cs, w_mod, b_mod):
    n_layers, d, n = w_mod.shape
    return pl.pallas_call(
        _mod_kernel,
        grid=(n_layers, n // MOD_TN),
        in_specs=[pl.BlockSpec((MOD_ROWS, d), lambda l, j: (0, 0)),
                  pl.BlockSpec((1, d, MOD_TN), lambda l, j: (l, 0, j)),
                  pl.BlockSpec((1, 1, MOD_TN), lambda l, j: (l, 0, j))],
        out_specs=pl.BlockSpec((1, MOD_ROWS, MOD_TN), lambda l, j: (l, 0, j)),
        out_shape=jax.ShapeDtypeStruct((n_layers, MOD_ROWS, n), F32),
        compiler_params=_params(2),
        name="mod",
    )(cs, w_mod, b_mod.reshape(n_layers, 1, n))


def _ffn_kernel(*refs, mod_off, mix, final):
    refs = list(refs)
    h_ref, mod_ref, g_ref, win_ref, wout_ref = refs[:5]
    o_ref = refs[-1]
    mod = mod_ref[0]
    if mix:
        yg_ref, ys_ref, yd_ref, wmix_ref = refs[5:9]
    if final:
        gfin_ref = refs[-2]
    shift, scale, gate = mod[mod_off:mod_off + 1], mod[mod_off + 1:mod_off + 2], mod[mod_off + 2:mod_off + 3]
    parts = [slice(r, r + FFN_ROWS) for r in range(0, h_ref.shape[1], FFN_ROWS)]

    def prologue(rs):
        x = h_ref[0, rs, :]
        if mix:
            a = GLA_WIDTH
            b = GLA_WIDTH + SWA_WIDTH
            proj = (_dot(yg_ref[0, rs, :], wmix_ref[0:a, :]) + _dot(ys_ref[0, rs, :], wmix_ref[a:b, :])
                    + _dot(yd_ref[0, rs, :], wmix_ref[b:, :]))
            x = x + mod[5:6] * proj
        return x, (_rms(x, g_ref[...]) * (1.0 + scale) + shift).astype(BF16)

    def swiglu(y):
        gt = _dot(y, win_ref[:, :D_FF])
        up = _dot(y, win_ref[:, D_FF:])
        return _dot((_silu(gt) * up).astype(BF16), wout_ref[...])

    def epilogue(rs, x, acc):
        out = x + (0.5 * gate) * acc
        if final:
            out = _rms(out, gfin_ref[...])
        o_ref[0, rs, :] = out

    x, y = prologue(parts[0])
    for r, rs in enumerate(parts):
        acc = swiglu(y)
        x_cur = x
        if r + 1 < len(parts):
            x, y = prologue(parts[r + 1])
        epilogue(rs, x_cur, acc)


def _ffn_call(h, mod, g, w_in, w_out, layer, mod_off, mix=None, g_final=None):
    bx, t, d = h.shape
    tm = min(FFN_STEP, t)
    final = g_final is not None
    tile = lambda width: pl.BlockSpec((1, tm, width), lambda b, i: (b, i, 0))
    in_specs = [tile(d),
                pl.BlockSpec((1, N_MOD, d), lambda b, i: (b, 0, 0)),
                _const_spec((1, d), 2),
                _layer_spec((d, 2 * D_FF), layer),
                _layer_spec((D_FF, d), layer)]
    args = [h, mod, g.reshape(1, d), w_in, w_out]
    if mix is not None:
        in_specs += [tile(GLA_WIDTH), tile(SWA_WIDTH), tile(DIFF_WIDTH), _layer_spec((MIX_WIDTH, d), layer)]
        args += list(mix)
    if final:
        in_specs.append(_const_spec((1, d), 2))
        args.append(g_final.reshape(1, d))
    return pl.pallas_call(
        functools.partial(_ffn_kernel, mod_off=mod_off, mix=mix is not None, final=final),
        grid=(bx, t // tm),
        in_specs=in_specs,
        out_specs=pl.BlockSpec((1, tm, d), lambda b, i: (b, i, 0)),
        out_shape=jax.ShapeDtypeStruct(h.shape, F32),
        compiler_params=_params(2),
        name="ffn",
    )(*args)


SWA_QSCALE = SWA_HD ** -0.5 * LOG2E
DIFF_QSCALE = DIFF_QK ** -0.5 * LOG2E


def _rot_half(blk, quarter, first):
    return jnp.where(first, pltpu.roll(blk, LANE - quarter, 1), pltpu.roll(blk, quarter, 1))


def _inproj_kernel(*refs, rope):
    if rope:
        h_ref, mod_ref, g_ref, w_ref, tab_ref, pg_ref, ps_ref, pd_ref, kts_ref, kt_ref = refs
    else:
        h_ref, mod_ref, g_ref, w_ref, pg_ref, ps_ref, pd_ref, kts_ref, kt_ref = refs
    mod = mod_ref[0]
    rows = h_ref.shape[1] // 2
    halves = [slice(0, rows), slice(rows, 2 * rows)]

    def project(rs):
        y = (_rms(h_ref[0, rs, :], g_ref[...]) * (1.0 + mod[4:5]) + mod[3:4]).astype(BF16)
        return _dot(y, w_ref[...])

    def emit(rs, p):
        pg_ref[0, rs, :] = p[:, :PG_W]
        swa = [p[:, PG_W + i * LANE:PG_W + (i + 1) * LANE] for i in range(PS_COLS // LANE)]
        dif = [p[:, PG_W + PS_COLS + i * LANE:PG_W + PS_COLS + (i + 1) * LANE] for i in range(PD_COLS // LANE)]
        if rope:
            tab = [tab_ref[rs, i * LANE:(i + 1) * LANE] for i in range(8)]
            lane = lax.broadcasted_iota(jnp.int32, (1, LANE), 1)
            first_s = (lane % (SWA_HD // 2)) < (SWA_HD // 4)
            first_d = (lane % (DIFF_QK // 2)) < (DIFF_QK // 4)
            for i in range(5):
                c, s = (tab[0], tab[1]) if i < 4 else (tab[2], tab[3])
                swa[i] = swa[i] * c + _rot_half(swa[i], SWA_HD // 4, first_s) * s
            for i in range(4):
                c, s = (tab[4], tab[5]) if i < 2 else (tab[6], tab[7])
                dif[i] = dif[i] * c + _rot_half(dif[i], DIFF_QK // 4, first_d) * s
        else:
            for i in range(4):
                swa[i] = swa[i] * SWA_QSCALE
            for i in range(2):
                dif[i] = dif[i] * DIFF_QSCALE
        ps_ref[0, rs, :] = jnp.concatenate(swa[0:4] + swa[5:7], axis=1).astype(BF16)
        kt = swa[4].T
        k0, k1 = kt[:SWA_HD], kt[SWA_HD:]
        kts_ref[0, :, rs] = jnp.concatenate([k0, k0, k1, k1], axis=0).astype(BF16)
        pd_ref[0, rs, :] = jnp.concatenate(dif[0:2] + dif[4:6], axis=1).astype(BF16)
        kt_ref[0, :, rs] = jnp.concatenate(dif[2:4], axis=1).T.astype(BF16)

    pa = project(halves[0])
    pb = project(halves[1])
    emit(halves[0], pa)
    emit(halves[1], pb)


def _inproj_call(h, mod, g, w, layer, tab=None):
    bx, t, d = h.shape
    tm = min(FFN_TM, t)
    rope = tab is not None
    in_specs = [pl.BlockSpec((1, tm, d), lambda j, b: (b, j, 0)),
                pl.BlockSpec((1, N_MOD, d), lambda j, b: (b, 0, 0)),
                _const_spec((1, d), 2),
                _layer_spec((d, P_W), layer)]
    args = [h, mod, g.reshape(1, d), w]
    if rope:
        in_specs.append(pl.BlockSpec((tm, 8 * LANE), lambda j, b: (j, 0)))
        args.append(tab)
    out_map = lambda j, b: (b, j, 0)
    return pl.pallas_call(
        functools.partial(_inproj_kernel, rope=rope),
        grid=(t // tm, bx),
        in_specs=in_specs,
        out_specs=[pl.BlockSpec((1, tm, PG_W), out_map), pl.BlockSpec((1, tm, PS_W), out_map),
                   pl.BlockSpec((1, tm, PD_W), out_map),
                   pl.BlockSpec((1, 2 * LANE, tm), lambda j, b: (b, 0, j)),
                   pl.BlockSpec((1, DIFF_QK_W, tm), lambda j, b: (b, 0, j))],
        out_shape=[jax.ShapeDtypeStruct((bx, t, PG_W), F32), jax.ShapeDtypeStruct((bx, t, PS_W), BF16),
                   jax.ShapeDtypeStruct((bx, t, PD_W), BF16),
                   jax.ShapeDtypeStruct((bx, 2 * LANE, t), BF16),
                   jax.ShapeDtypeStruct((bx, DIFF_QK_W, t), BF16)],
        compiler_params=_params(2),
        name="inproj",
    )(*args)


GLA_SROWS = (GLA_LEVELS + 2) * GLA_CHUNK
GLA_LV_DIAG = GLA_LEVELS
GLA_LV_NONE = GLA_LEVELS + 1


def _gla_constants():
    c = GLA_CHUNK
    r = np.arange(c)[:, None]
    t = np.arange(c)[None, :]
    blocks = []
    for lv in range(GLA_LEVELS):
        half = 1 << lv
        mid = (r // (2 * half)) * (2 * half) + half
        second = r >= mid
        blocks.append(np.where(second, (t >= mid) & (t <= r), (t > r) & (t < mid)))
    blocks.append(t <= r)
    blocks.append(t > r)
    fwd = np.concatenate(blocks, axis=0).astype(np.float32)
    bwd = np.concatenate([b[::-1, ::-1] for b in blocks], axis=0).astype(np.float32)
    s2 = np.stack([np.concatenate([m, m], axis=1) for m in (fwd, bwd)])
    i = np.arange(c)[:, None]
    j = np.arange(c)[None, :]
    x = np.bitwise_xor(i, j)
    lvl = np.where(j > i, GLA_LV_NONE,
                   np.where(i == j, GLA_LV_DIAG, np.floor(np.log2(np.maximum(x, 1))).astype(np.int64)))
    lv_f = np.tile(lvl, (2, 2))
    lv_b = np.tile(lvl[::-1, ::-1], (2, 2))
    return jnp.asarray(s2, BF16), jnp.asarray(np.stack([lv_f, lv_b]), jnp.int32)


def _gla_kernel(*refs, ctx_out):
    if ctx_out:
        (pgx_ref, pgc_ref, wg_ref, bg_ref, gn_ref, s2_ref, lv_ref, ox_ref, oc_ref,
         oi_ref, qc_ref, upd_ref, dec_ref, st_ref, f_ref) = refs
    else:
        (pgx_ref, pgc_ref, wg_ref, bg_ref, gn_ref, s2_ref, lv_ref, ox_ref,
         oi_ref, qc_ref, upd_ref, dec_ref, st_ref, f_ref) = refs
        oc_ref = None
    c = GLA_CHUNK
    grp = GLA_GROUP
    hv = GLA_HEADS * GLA_DV
    n_ctx = pgc_ref.shape[1]
    s_len = pgx_ref.shape[1]
    row = lax.broadcasted_iota(jnp.int32, (hv, GLA_QK_W), 0)
    lane = lax.broadcasted_iota(jnp.int32, (hv, GLA_QK_W), 1)
    head_qk = (row // GLA_DV) == (lane // GLA_DK)
    row2 = lax.broadcasted_iota(jnp.int32, (hv, hv), 0)
    col2 = lax.broadcasted_iota(jnp.int32, (hv, hv), 1)
    head_v_b = ((row2 // GLA_DV) == (col2 // GLA_DV)).astype(F32).astype(BF16)
    gn = gn_ref[...]
    wh, wl = _split_bf16(jnp.concatenate([wg_ref[0], wg_ref[1]], axis=1))
    w3 = jnp.concatenate([wh, wh, wl], axis=0)
    bias = jnp.concatenate([bg_ref[0], bg_ref[1]], axis=1)

    lane1 = lax.broadcasted_iota(jnp.int32, (1, GLA_QK_W), 1)
    pair_mask = [((lane1 // (2 * GLA_DK)) == p).astype(F32).astype(BF16) for p in range(2)]
    parity_mask = [(((lane1 // GLA_DK) % 2) == hh).astype(F32).astype(BF16) for hh in range(2)]
    lane_v = lax.broadcasted_iota(jnp.int32, (1, hv), 1)
    hv_mask = [((lane_v // GLA_DV) == h).astype(F32).astype(BF16) for h in range(GLA_HEADS)]

    def pair_rows(a):
        a = a.astype(BF16)
        return jnp.concatenate([a * pair_mask[0], a * pair_mask[1]], axis=0)

    def parity_rows(a):
        a = a.astype(BF16)
        return jnp.concatenate([a * parity_mask[0], a * parity_mask[1]], axis=0)

    def local(src_ref, src_row, dst_row, with_out, grp):
        def src_rows(g):
            return pl.ds(pl.multiple_of(src_row + g * c, c), c)

        def q_of(g):
            return src_ref[0, src_rows(g), 0:128] * (GLA_DK ** -0.5)

        def k_of(g):
            return src_ref[0, src_rows(g), 128:256]

        def v_of(g):
            return src_ref[0, src_rows(g), 256:512].astype(BF16)

        zh, zl = _split_bf16(jnp.concatenate([src_ref[0, src_rows(g), 768:896] for g in range(grp)], axis=0))
        z = _dot(jnp.concatenate([zh, zl, zh], axis=1), w3) + bias
        gate = (jnp.minimum(z, 0.0) - jnp.log(1.0 + jnp.exp(-jnp.abs(z)))) * (1.0 / GLA_TAU)
        gh, gl = _split_bf16(gate)
        for d in range(2):
            cols = slice(d * GLA_QK_W, (d + 1) * GLA_QK_W)
            g2 = jnp.concatenate([jnp.concatenate([gh[g * c:(g + 1) * c, cols], gl[g * c:(g + 1) * c, cols]], axis=0)
                                  for g in range(grp)], axis=1)
            f_ref[d, :, 0:grp * GLA_QK_W] = jnp.exp(_dot(s2_ref[d], g2))

        def fac(d, g, block):
            return f_ref[d, block * c:(block + 1) * c, g * GLA_QK_W:(g + 1) * GLA_QK_W]

        def vbd_of(g, p):
            v = v_of(g)
            return jnp.concatenate([v * hv_mask[2 * p], v * hv_mask[2 * p + 1]], axis=0)

        if with_out:
            diag = [_dot_nt(pair_rows(q_of(g)), parity_rows(k_of(g))) for g in range(grp)]
        for d in range(2):
            for g in range(grp):
                rows = pl.ds(pl.multiple_of(dst_row + g * c, c), c)
                ci = (dst_row + g * c) // c
                upd = _dot_tn(v_of(g), (k_of(g) * fac(d, g, GLA_LEVELS + 1)).astype(BF16))
                upd_ref[d, ci] = jnp.where(head_qk, upd, 0.0)
                fcum = fac(d, g, GLA_LEVELS)
                last = c - 1 if d == 0 else 0
                dec_ref[d, ci] = jnp.broadcast_to(fcum[last:last + 1], (8, GLA_QK_W))
                if with_out:
                    qc_ref[d, rows, :] = (q_of(g) * fcum).astype(BF16)
            if with_out:
                lv = lv_ref[d]
                atts = [jnp.where(lv == GLA_LV_DIAG, dg, 0.0) for dg in diag]
                for level in range(GLA_LEVELS):
                    for g in range(grp):
                        fl = fac(d, g, level)
                        s = _dot_nt(pair_rows(q_of(g) * fl), parity_rows(k_of(g) * fl))
                        atts[g] = jnp.where(lv == level, s, atts[g])
                for g in range(grp):
                    rows = pl.ds(pl.multiple_of(dst_row + g * c, c), c)
                    att = atts[g].astype(BF16)
                    oi_ref[d, rows, :] = _dot(att[:c], vbd_of(g, 0)) + _dot(att[c:], vbd_of(g, 1))

    def local_pass(src_ref, dst0, with_out, grp):
        def body(i, carry):
            local(src_ref, i * (grp * c), dst0 + i * (grp * c), with_out, grp)
            return carry
        lax.fori_loop(0, src_ref.shape[1] // (grp * c), body, 0)

    def scan_pass(first, n, with_out):
        def body(i, carry):
            for d in range(2):
                st = st_ref[d]
                for j in range(GLA_SCAN_UNROLL):
                    step = i * GLA_SCAN_UNROLL + j
                    ci = first + (step if d == 0 else n - 1 - step)
                    if with_out:
                        rows = pl.ds(pl.multiple_of(ci * c, c), c)
                        oi_ref[d, rows, :] = oi_ref[d, rows, :] + _dot_nt(qc_ref[d, rows, :], st.astype(BF16))
                    st = dec_ref[d, ci][0:1] * st + upd_ref[d, ci]
                st_ref[d] = st
            return carry
        lax.fori_loop(0, n // GLA_SCAN_UNROLL, body, 0)

    def finish_pass(src_ref, src0, out_ref, n_rows):
        tile = grp * c

        def body(i, carry):
            r = pl.multiple_of(i * tile, tile)
            rs = pl.ds(pl.multiple_of(src0 + r, tile), tile)
            o = oi_ref[0, rs, :] + oi_ref[1, rs, :]
            hi, lo = _split_bf16(o * o)
            ms = (_dot(hi, head_v_b) + _dot(lo, head_v_b)) * (1.0 / GLA_DV)
            og = src_ref[0, pl.ds(r, tile), 512:768]
            out_ref[0, pl.ds(r, tile), :] = (o * lax.rsqrt(ms + RMS_EPS) * gn * _silu(og)).astype(BF16)
            return carry
        lax.fori_loop(0, n_rows // tile, body, 0)

    local_pass(pgc_ref, 0, ctx_out, GLA_GROUP)
    local_pass(pgx_ref, n_ctx, True, GLA_GROUP_X)
    st_ref[...] = jnp.zeros(st_ref.shape, F32)
    scan_pass(0, n_ctx // c, ctx_out)
    scan_pass(n_ctx // c, s_len // c, True)
    finish_pass(pgx_ref, n_ctx, ox_ref, s_len)
    if ctx_out:
        finish_pass(pgc_ref, 0, oc_ref, n_ctx)


def _gla_call(pgx, pgc, wg, bg, gn, s2, lv, ctx_out):
    b, s, _ = pgx.shape
    n_ctx = pgc.shape[1]
    hv = GLA_WIDTH
    n_rows = s + n_ctx
    assert s % (GLA_GROUP_X * GLA_CHUNK) == 0 and n_ctx % (GLA_GROUP * GLA_CHUNK) == 0
    assert (s // GLA_CHUNK) % GLA_SCAN_UNROLL == 0 and (n_ctx // GLA_CHUNK) % GLA_SCAN_UNROLL == 0
    in_specs = [pl.BlockSpec((1, s, PG_W), lambda i: (i, 0, 0)),
                pl.BlockSpec((1, n_ctx, PG_W), lambda i: (i, 0, 0)),
                _const_spec(wg.shape, 1), _const_spec(bg.shape, 1), _const_spec((1, hv), 1),
                _const_spec(s2.shape, 1), _const_spec(lv.shape, 1)]
    out_specs = [pl.BlockSpec((1, s, hv), lambda i: (i, 0, 0))]
    out_shape = [jax.ShapeDtypeStruct((b, s, hv), BF16)]
    if ctx_out:
        out_specs.append(pl.BlockSpec((1, n_ctx, hv), lambda i: (i, 0, 0)))
        out_shape.append(jax.ShapeDtypeStruct((b, n_ctx, hv), BF16))
    scratch = [pltpu.VMEM((2, n_rows, hv), F32),
               pltpu.VMEM((2, n_rows, GLA_QK_W), BF16),
               pltpu.VMEM((2, n_rows // GLA_CHUNK, hv, GLA_QK_W), F32),
               pltpu.VMEM((2, n_rows // GLA_CHUNK, 8, GLA_QK_W), F32),
               pltpu.VMEM((2, hv, GLA_QK_W), F32),
               pltpu.VMEM((2, GLA_SROWS, max(GLA_GROUP, GLA_GROUP_X) * GLA_QK_W), F32)]
    res = pl.pallas_call(
        functools.partial(_gla_kernel, ctx_out=ctx_out),
        grid=(b,),
        in_specs=in_specs, out_specs=out_specs, out_shape=out_shape, scratch_shapes=scratch,
        compiler_params=_params(1),
        name="gla",
    )(pgx, pgc, wg, bg, gn.reshape(1, hv), s2, lv)
    return (res[0], res[1]) if ctx_out else (res[0], None)


def _swa_kernel(*refs, ctx_out):
    if ctx_out:
        sink_ref, px_ref, ktx_ref, pc_ref, ktc_ref, ox_ref, oc_ref, sc_ref = refs
    else:
        sink_ref, px_ref, ktx_ref, pc_ref, ktc_ref, ox_ref, sc_ref = refs
    s_len = px_ref.shape[1]
    n_ctx = pc_ref.shape[1]
    blk = SWA_BLOCK
    band = 3 * blk
    grp = SWA_HEADS // SWA_KV_HEADS
    lane = lax.broadcasted_iota(jnp.int32, (1, LANE), 1)
    low = lane < SWA_HD
    half_mask = [low.astype(F32).astype(BF16), (~low).astype(F32).astype(BF16)]
    rel = (lax.broadcasted_iota(jnp.int32, (blk, band), 0)
           - lax.broadcasted_iota(jnp.int32, (blk, band), 1))

    def col(i):
        return slice(i * LANE, (i + 1) * LANE)

    def krows(kv):
        return slice(kv * LANE, (kv + 1) * LANE)

    def sink_col(n_rows, kv, half):
        ha = grp * kv + half
        return jnp.concatenate([jnp.full((n_rows, 1), sink_ref[ha], F32),
                                jnp.full((n_rows, 1), sink_ref[ha + 2], F32)], axis=0)

    def softmax_pv(sc, v, snk):
        m = jnp.maximum(snk, jnp.max(sc(), axis=-1, keepdims=True))
        p = jnp.exp2(sc() - m).astype(BF16)
        r = _dot(p, jnp.concatenate([v, jnp.ones(v.shape, BF16)], axis=1))
        return r[:, :LANE] / (r[:, LANE:] + jnp.exp2(snk - m))

    def pipeline(tasks, depth, carried=(), prefetch=()):
        for t in range(len(carried), min(depth, len(tasks))):
            sc_ref[t] = tasks[t][0]()
        outs = []
        for t, (_, finish) in enumerate(tasks):
            ahead = t + depth
            if ahead < len(tasks):
                sc_ref[ahead] = tasks[ahead][0]()
            elif ahead - len(tasks) < len(prefetch):
                prefetch[ahead - len(tasks)]()
            outs.append(finish(lambda t=t: sc_ref[t]))
        return outs

    units = [(kv, half) for kv in range(SWA_KV_HEADS) for half in range(2)]
    blocks_per_step = 2
    n_steps = s_len // (blk * blocks_per_step)
    depth = SWA_PIPE_DEPTH

    def block_tasks(n):
        r0 = pl.multiple_of(n * blk, blk)
        start = pl.multiple_of(jnp.clip((n - 1) * blk, 0, s_len - band), blk)
        tasks = []
        for kv, half in units:
            def scores(kv=kv, half=half):
                bias = jnp.where(jnp.abs(rel + (r0 - start)) <= SWA_WINDOW, 0.0, -jnp.inf)
                bias = jnp.concatenate([bias, jnp.zeros((blk, n_ctx), F32)], axis=1)
                q = jnp.concatenate([px_ref[0, pl.ds(r0, blk), col(2 * kv)],
                                     px_ref[0, pl.ds(r0, blk), col(2 * kv + 1)]], axis=0) * half_mask[half]
                kt = jnp.concatenate([ktx_ref[0, krows(kv), pl.ds(start, band)],
                                      ktc_ref[0, krows(kv), :]], axis=1)
                return _dot(q, kt) + jnp.concatenate([bias, bias], axis=0)

            def finish(cur, kv=kv, half=half):
                v = jnp.concatenate([px_ref[0, pl.ds(start, band), col(4 + kv)],
                                     pc_ref[0, :, col(4 + kv)]], axis=0)
                return softmax_pv(cur, v, sink_col(blk, kv, half))

            tasks.append((scores, finish))
        return tasks, r0

    def prefetch_into(slot, scores):
        def run():
            sc_ref[slot] = scores()
        return run

    def body(i, carry):
        tasks, rows = [], []
        for sub in range(blocks_per_step):
            t, r0 = block_tasks(i * blocks_per_step + sub)
            tasks += t
            rows.append(r0)
        nxt, _ = block_tasks(jnp.minimum(i + 1, n_steps - 1) * blocks_per_step)
        outs = pipeline(tasks, depth, carried=range(depth),
                        prefetch=[prefetch_into(k, nxt[k][0]) for k in range(depth)])
        for sub in range(blocks_per_step):
            for kv in range(SWA_KV_HEADS):
                t = (sub * SWA_KV_HEADS + kv) * 2
                o2 = jnp.where(low, outs[t], outs[t + 1]).astype(BF16)
                ox_ref[0, pl.ds(rows[sub], blk), col(2 * kv)] = o2[:blk]
                ox_ref[0, pl.ds(rows[sub], blk), col(2 * kv + 1)] = o2[blk:]
        return carry

    first, _ = block_tasks(jnp.int32(0))
    for k in range(depth):
        sc_ref[k] = first[k][0]()
    lax.fori_loop(0, n_steps, body, 0)

    if ctx_out:
        tasks = []
        for kv, half in units:
            def scores(kv=kv, half=half):
                q = jnp.concatenate([pc_ref[0, :, col(2 * kv)], pc_ref[0, :, col(2 * kv + 1)]], axis=0)
                return _dot(q * half_mask[half], ktc_ref[0, krows(kv), :])

            def finish(cur, kv=kv, half=half):
                return softmax_pv(cur, pc_ref[0, :, col(4 + kv)], sink_col(n_ctx, kv, half))

            tasks.append((scores, finish))
        outs = []
        nxt = tasks[0][0]()
        for t, (_, finish) in enumerate(tasks):
            cur = nxt
            if t + 1 < len(tasks):
                nxt = tasks[t + 1][0]()
            outs.append(finish(lambda cur=cur: cur))
        for kv in range(SWA_KV_HEADS):
            o2 = jnp.where(low, outs[2 * kv], outs[2 * kv + 1]).astype(BF16)
            oc_ref[0, :, col(2 * kv)] = o2[:n_ctx]
            oc_ref[0, :, col(2 * kv + 1)] = o2[n_ctx:]


def _swa_call(psx, ktx, psc, ktc, sink, ctx_out):
    b, s, _ = psx.shape
    n_ctx = psc.shape[1]
    assert s % (2 * SWA_BLOCK) == 0 and s >= 3 * SWA_BLOCK and SWA_WINDOW == SWA_BLOCK
    in_specs = [pl.BlockSpec(memory_space=pltpu.SMEM),
                pl.BlockSpec((1, s, PS_W), lambda i: (i, 0, 0)),
                pl.BlockSpec((1, 2 * LANE, s), lambda i: (i, 0, 0)),
                pl.BlockSpec((1, n_ctx, PS_W), lambda i: (i, 0, 0)),
                pl.BlockSpec((1, 2 * LANE, n_ctx), lambda i: (0, 0, i))]
    out_specs = [pl.BlockSpec((1, s, SWA_WIDTH), lambda i: (i, 0, 0))]
    out_shape = [jax.ShapeDtypeStruct((b, s, SWA_WIDTH), BF16)]
    if ctx_out:
        out_specs.append(pl.BlockSpec((1, n_ctx, SWA_WIDTH), lambda i: (i, 0, 0)))
        out_shape.append(jax.ShapeDtypeStruct((b, n_ctx, SWA_WIDTH), BF16))
    res = pl.pallas_call(
        functools.partial(_swa_kernel, ctx_out=ctx_out),
        grid=(b,),
        in_specs=in_specs, out_specs=out_specs, out_shape=out_shape,
        scratch_shapes=[pltpu.VMEM((SWA_UNITS_PER_STEP, 2 * SWA_BLOCK, 3 * SWA_BLOCK + n_ctx), F32)],
        compiler_params=_params(1),
        name="swa",
    )(sink, psx, ktx, psc, ktc)
    return (res[0], res[1]) if ctx_out else (res[0], None)


def _diff_kernel(*refs, ctx_out, lambda_init):
    if ctx_out:
        px_ref, ktx_ref, pc_ref, ktc_ref, lam_ref, gn_ref, ox_ref, oc_ref, kt_ref, va_ref, sc_ref = refs
    else:
        px_ref, ktx_ref, pc_ref, ktc_ref, lam_ref, gn_ref, ox_ref, kt_ref, va_ref, sc_ref = refs
    s_len = px_ref.shape[1]
    w = DIFF_QK_W
    lv = lam_ref[...]
    lam = (jnp.exp(jnp.sum(lv[0:1] * lv[1:2], axis=-1, keepdims=True))
           - jnp.exp(jnp.sum(lv[2:3] * lv[3:4], axis=-1, keepdims=True)) + lambda_init)
    lane = lax.broadcasted_iota(jnp.int32, (1, w), 1)
    row2 = lax.broadcasted_iota(jnp.int32, (w, w), 0)
    col2 = lax.broadcasted_iota(jnp.int32, (w, w), 1)
    head_ones = ((row2 // DIFF_V) == (col2 // DIFF_V)).astype(F32).astype(BF16)
    gn = gn_ref[...] * (1.0 - lambda_init)
    unit_masks = [((lane >= DIFF_QK * u) & (lane < DIFF_QK * (u + 1))).astype(F32).astype(BF16)
                  for u in range(2 * DIFF_HEADS)]
    head_masks = [(lane >= DIFF_V * h) & (lane < DIFF_V * (h + 1)) for h in range(DIFF_HEADS)]

    kt_ref[:, :s_len] = ktx_ref[0]
    kt_ref[:, s_len:] = ktc_ref[0]
    for side in range(2):
        keep = ((lane >= side * (w // 2)) & (lane < (side + 1) * (w // 2))).astype(F32).astype(BF16)
        va_ref[side, :s_len, :] = px_ref[0, :, w:2 * w] * keep + (1.0 - keep)
        va_ref[side, s_len:, :] = pc_ref[0, :, w:2 * w] * keep + (1.0 - keep)

    def attend(q, k0, q_next=None):
        def scores(qq, u):
            return _dot(qq * unit_masks[u], kt_ref[:, k0:])

        def softmax_pv(sc, u):
            m = jnp.max(sc, axis=-1, keepdims=True)
            return _dot(jnp.exp2(sc - m).astype(BF16), va_ref[(u // 2) // (DIFF_HEADS // 2), k0:, :])

        units = []
        nxt = scores(q, 0) if q_next is None else sc_ref[...]
        for u in range(2 * DIFF_HEADS):
            cur = nxt
            if u + 1 < 2 * DIFF_HEADS:
                nxt = scores(q, u + 1)
            elif q_next is not None:
                sc_ref[...] = scores(q_next, 0)
            units.append(softmax_pv(cur, u))
        o = jnp.zeros((q.shape[0], w), F32)
        for h in range(DIFF_HEADS):
            r1, r2 = units[2 * h], units[2 * h + 1]
            oh = r1 / pltpu.roll(r1, w // 2, 1) - lam * (r2 / pltpu.roll(r2, w // 2, 1))
            o = jnp.where(head_masks[h], oh, o)
        hi, lo = _split_bf16(o * o)
        ms = (_dot(hi, head_ones) + _dot(lo, head_ones)) * (1.0 / DIFF_V)
        return (o * lax.rsqrt(ms + RMS_EPS) * gn).astype(BF16)

    n_blocks = s_len // DIFF_QBLOCK

    def q_block(n):
        return px_ref[0, pl.ds(pl.multiple_of(n * DIFF_QBLOCK, DIFF_QBLOCK), DIFF_QBLOCK), 0:w]

    def body(n, carry):
        out = attend(q_block(n), 0, q_block(jnp.minimum(n + 1, n_blocks - 1)))
        ox_ref[0, pl.ds(pl.multiple_of(n * DIFF_QBLOCK, DIFF_QBLOCK), DIFF_QBLOCK), :] = out
        return carry

    sc_ref[...] = _dot(q_block(0) * unit_masks[0], kt_ref[...])
    lax.fori_loop(0, n_blocks, body, 0)
    if ctx_out:
        oc_ref[0] = attend(pc_ref[0, :, 0:w], s_len)


def _diff_call(pdx, ktx, pdc, ktc, lam, gn, lambda_init, ctx_out):
    b, s, _ = pdx.shape
    n_ctx = pdc.shape[1]
    assert s % DIFF_QBLOCK == 0
    in_specs = [pl.BlockSpec((1, s, PD_W), lambda i: (i, 0, 0)),
                pl.BlockSpec((1, DIFF_QK_W, s), lambda i: (i, 0, 0)),
                pl.BlockSpec((1, n_ctx, PD_W), lambda i: (i, 0, 0)),
                pl.BlockSpec((1, DIFF_QK_W, n_ctx), lambda i: (0, 0, i)),
                _const_spec(lam.shape, 1), _const_spec((1, DIFF_WIDTH), 1)]
    out_specs = [pl.BlockSpec((1, s, DIFF_WIDTH), lambda i: (i, 0, 0))]
    out_shape = [jax.ShapeDtypeStruct((b, s, DIFF_WIDTH), BF16)]
    if ctx_out:
        out_specs.append(pl.BlockSpec((1, n_ctx, DIFF_WIDTH), lambda i: (i, 0, 0)))
        out_shape.append(jax.ShapeDtypeStruct((b, n_ctx, DIFF_WIDTH), BF16))
    res = pl.pallas_call(
        functools.partial(_diff_kernel, ctx_out=ctx_out, lambda_init=lambda_init),
        grid=(b,),
        in_specs=in_specs, out_specs=out_specs, out_shape=out_shape,
        scratch_shapes=[pltpu.VMEM((DIFF_QK_W, s + n_ctx), BF16),
                        pltpu.VMEM((2, s + n_ctx, DIFF_WIDTH), BF16),
                        pltpu.VMEM((DIFF_QBLOCK, s + n_ctx), F32)],
        compiler_params=_params(1),
        name="diff",
    )(pdx, ktx, pdc, ktc, lam, gn.reshape(1, DIFF_WIDTH))
    return (res[0], res[1]) if ctx_out else (res[0], None)


def _prep_w_in(w):
    w = w.astype(BF16)
    starts = np.concatenate([[0], np.cumsum(IN_SIZES)])
    gq, gk, gv, gf, gb, og, sq, sk, sv, dq, dk, dv = range(len(IN_SIZES))

    def cols(first, last):
        return w[..., starts[first]:starts[last + 1]]

    pad = jnp.zeros(w.shape[:-1] + (PG_W - (2 * GLA_QK_W + 2 * GLA_WIDTH + 2 * GLA_GATE_RANK),), BF16)
    v0 = w[..., starts[sv]:starts[sv] + SWA_HD]
    v1 = w[..., starts[sv] + SWA_HD:starts[sv + 1]]
    return jnp.concatenate([cols(gq, gv), cols(og, og), cols(gf, gb), pad, cols(sq, sk), v0, v0, v1, v1,
                            cols(dq, dv)], axis=-1)


def _prep_gate(w_gate, b_gate):
    wg = jnp.zeros((2, LANE, GLA_QK_W), F32)
    for d in range(2):
        wg = wg.at[d, GLA_GATE_RANK * d:GLA_GATE_RANK * (d + 1), :].set(w_gate[d])
    return wg, b_gate.reshape(2, 1, GLA_QK_W)


def _axial_angles(rows, head_dim):
    half = head_dim // 2
    row = np.repeat(np.arange(rows, dtype=np.float64), GRID_W)
    col = np.tile(np.arange(GRID_W, dtype=np.float64), rows)
    inv_freq = 1.0 / (ROPE_BASE ** (np.arange(0, half, 2, dtype=np.float64) / half))

    def axis_angles(pos):
        a = pos[:, None] * inv_freq[None, :]
        return np.concatenate([a, a], axis=-1)

    return np.concatenate([axis_angles(row), axis_angles(col)], axis=-1)


def _rope_table(seq):
    rows = seq // GRID_W
    blocks = []
    for head_dim, qscale in ((SWA_HD, SWA_QSCALE), (DIFF_QK, DIFF_QSCALE)):
        ang = _axial_angles(rows, head_dim)
        quarter = head_dim // 4
        sign = np.where((np.arange(head_dim) % (2 * quarter)) < quarter, -1.0, 1.0)
        reps = LANE // head_dim
        cos = np.tile(np.cos(ang), (1, reps))
        sin = np.tile(np.sin(ang) * sign[None, :], (1, reps))
        blocks += [cos * qscale, sin * qscale, cos, sin]
    return jnp.asarray(np.concatenate(blocks, axis=1), F32)


def kernel(x, c, ctx, c_ctx, w_mod, b_mod, g_ffn1, w_ffn1_in, w_ffn1_out, g_mix, w_in, w_out, w_gla_gate,
           b_gla_gate, g_gla_norm, swa_sink, diff_lambda, g_diff_norm, g_ffn2, w_ffn2_in, w_ffn2_out, g_final):
    b, s, d = x.shape
    n_ctx = ctx.shape[1]
    depth = w_mod.shape[0]

    cs = jnp.zeros((MOD_ROWS, d), F32).at[:b].set(c).at[b].set(c_ctx)
    mod = _mod_call(cs, w_mod, b_mod).reshape(depth, MOD_ROWS, N_MOD, d)
    tab = _rope_table(s)
    s2, lv = _gla_constants()

    w1i, w1o = w_ffn1_in.astype(BF16), w_ffn1_out.astype(BF16)
    w2i, w2o = w_ffn2_in.astype(BF16), w_ffn2_out.astype(BF16)
    wi = _prep_w_in(w_in)
    wo = w_out.astype(BF16)

    hx = x
    hc = ctx.reshape(1, b * n_ctx, d)
    for l in range(depth):
        ctx_out = l < depth - 1
        last = l == depth - 1
        mod_x = mod[l, :b]
        mod_c = mod[l, b:b + 1]
        lambda_init = 0.8 - 0.6 * math.exp(-0.3 * l)
        wg, bg = _prep_gate(w_gla_gate[l], b_gla_gate[l])
        sink = swa_sink[l] * LOG2E

        hx = _ffn_call(hx, mod_x, g_ffn1[l], w1i, w1o, l, 0)
        hc = _ffn_call(hc, mod_c, g_ffn1[l], w1i, w1o, l, 0)
        pgx, psx, pdx, ksx, ktx = _inproj_call(hx, mod_x, g_mix[l], wi, l, tab)
        pgc, psc, pdc, ksc, ktc = _inproj_call(hc, mod_c, g_mix[l], wi, l)
        pgc, psc, pdc = (a.reshape(b, n_ctx, a.shape[-1]) for a in (pgc, psc, pdc))

        gla_x, gla_c = _gla_call(pgx, pgc, wg, bg, g_gla_norm[l], s2, lv, ctx_out)
        swa_x, swa_c = _swa_call(psx, ksx, psc, ksc, sink, ctx_out)
        dif_x, dif_c = _diff_call(pdx, ktx, pdc, ktc, diff_lambda[l], g_diff_norm[l], lambda_init, ctx_out)

        hx = _ffn_call(hx, mod_x, g_ffn2[l], w2i, w2o, l, 6, mix=(gla_x, swa_x, dif_x, wo),
                       g_final=g_final if last else None)
        if ctx_out:
            flat = lambda a: a.reshape(1, b * n_ctx, a.shape[-1])
            hc = _ffn_call(hc, mod_c, g_ffn2[l], w2i, w2o, l, 6, mix=(flat(gla_c), flat(swa_c), flat(dif_c), wo))
    return hx
```

```python
import functools
import math

import numpy as np
import jax
import jax.numpy as jnp
from jax import lax
from jax.experimental import pallas as pl
from jax.experimental.pallas import tpu as pltpu

F32 = jnp.float32
BF16 = jnp.bfloat16

D_MODEL = 1024
DEPTH = 2
GRID_W = 64
N_MOD = 9
D_FF = 2816
RMS_EPS = 1e-6
ROPE_BASE = 10000.0

GLA_HEADS = 4
GLA_DK = 32
GLA_DV = 64
GLA_GATE_RANK = 16
GLA_TAU = 16.0
SWA_HEADS = 8
SWA_KV_HEADS = 2
SWA_HD = 64
SWA_WINDOW = 128
SWA_BLOCK = 128
DIFF_HEADS = 4
DIFF_QK = 32
DIFF_V = 64
DIFF_QBLOCK = 512

GLA_QK_W = GLA_HEADS * GLA_DK
GLA_WIDTH = GLA_HEADS * GLA_DV
SWA_WIDTH = SWA_HEADS * SWA_HD
SWA_KV_W = SWA_KV_HEADS * SWA_HD
DIFF_QK_W = DIFF_HEADS * 2 * DIFF_QK
DIFF_WIDTH = DIFF_HEADS * DIFF_V
MIX_WIDTH = GLA_WIDTH + SWA_WIDTH + DIFF_WIDTH
IN_SIZES = (GLA_QK_W, GLA_QK_W, GLA_WIDTH, GLA_GATE_RANK, GLA_GATE_RANK, GLA_WIDTH,
            SWA_WIDTH, SWA_KV_W, SWA_KV_W, DIFF_QK_W, DIFF_QK_W, DIFF_WIDTH)

LANE = 128
VMEM_LIMIT = 56 * 1024 * 1024
LOG2E = math.log2(math.e)

PG_W = 896
PS_COLS = 896
PS_W = 768
PD_COLS = 768
PD_W = 512
P_W = PG_W + PS_COLS + PD_COLS
GLA_CHUNK = 64
GLA_LEVELS = 6
GLA_GROUP = 4
GLA_GROUP_X = 16
GLA_SCAN_UNROLL = 4
SWA_UNITS_PER_STEP = 8
SWA_PIPE_DEPTH = 2
MOD_ROWS = 16
MOD_TN = 1152
FFN_TM = 512
FFN_STEP = 1024
FFN_ROWS = 256


def _dot(a, b):
    return jnp.dot(a, b, preferred_element_type=F32)


def _dot_nt(a, b):
    return lax.dot_general(a, b, (((1,), (1,)), ((), ())), preferred_element_type=F32)


def _dot_tn(a, b):
    return lax.dot_general(a, b, (((0,), (0,)), ((), ())), preferred_element_type=F32)


def _split_bf16(x):
    hi = x.astype(BF16)
    lo = (x - hi.astype(F32)).astype(BF16)
    return hi, lo


def _rms(x, g):
    return x * lax.rsqrt(jnp.mean(x * x, axis=-1, keepdims=True) + RMS_EPS) * g


def _silu(x):
    return x * (1.0 / (1.0 + jnp.exp(-x)))


def _params(n_grid):
    return pltpu.CompilerParams(dimension_semantics=("arbitrary",) * n_grid, vmem_limit_bytes=VMEM_LIMIT)


def _const_spec(shape, n_grid, single=False):
    zeros = (0,) * len(shape)
    index_map = {1: lambda a: zeros, 2: lambda a, b: zeros}[n_grid]
    if single:
        return pl.BlockSpec(shape, index_map, pipeline_mode=pl.Buffered(1))
    return pl.BlockSpec(shape, index_map)


def _layer_spec(shape, layer):
    return pl.BlockSpec((None,) + tuple(shape), lambda b, i: (layer, 0, 0), pipeline_mode=pl.Buffered(1))


def _mod_kernel(a_ref, w_ref, b_ref, o_ref):
    a = _silu(a_ref[...]).astype(BF16)
    o_ref[0] = _dot(a, w_ref[0].astype(BF16)) + b_ref[0]


def _mod_call(cs, w_mod, b_mod):
    n_layers, d, n = w_mod.shape
    return pl.pallas_call(
        _mod_kernel,
        grid=(n_layers, n // MOD_TN),
        in_specs=[pl.BlockSpec((MOD_ROWS, d), lambda l, j: (0, 0)),
                  pl.BlockSpec((1, d, MOD_TN), lambda l, j: (l, 0, j)),
                  pl.BlockSpec((1, 1, MOD_TN), lambda l, j: (l, 0, j))],
        out_specs=pl.BlockSpec((1, MOD_ROWS, MOD_TN), lambda l, j: (l, 0, j)),
        out_shape=jax.ShapeDtypeStruct((n_layers, MOD_ROWS, n), F32),
        compiler_params=_params(2),
        name="mod",
    )(cs, w_mod, b_mod.reshape(n_layers, 1, n))


def _ffn_kernel(*refs, mod_off, mix, final):
    refs = list(refs)
    h_ref, mod_ref, g_ref, win_ref, wout_ref = refs[:5]
    o_ref = refs[-1]
    mod = mod_ref[0]
    if mix:
        yg_ref, ys_ref, yd_ref, wmix_ref = refs[5:9]
    if final:
        gfin_ref = refs[-2]
    shift, scale, gate = mod[mod_off:mod_off + 1], mod[mod_off + 1:mod_off + 2], mod[mod_off + 2:mod_off + 3]
    parts = [slice(r, r + FFN_ROWS) for r in range(0, h_ref.shape[1], FFN_ROWS)]

    def prologue(rs):
        x = h_ref[0, rs, :]
        if mix:
            a = GLA_WIDTH
            b = GLA_WIDTH + SWA_WIDTH
            proj = (_dot(yg_ref[0, rs, :], wmix_ref[0:a, :]) + _dot(ys_ref[0, rs, :], wmix_ref[a:b, :])
                    + _dot(yd_ref[0, rs, :], wmix_ref[b:, :]))
            x = x + mod[5:6] * proj
        return x, (_rms(x, g_ref[...]) * (1.0 + scale) + shift).astype(BF16)

    def swiglu(y):
        gt = _dot(y, win_ref[:, :D_FF])
        up = _dot(y, win_ref[:, D_FF:])
        return _dot((_silu(gt) * up).astype(BF16), wout_ref[...])

    def epilogue(rs, x, acc):
        out = x + (0.5 * gate) * acc
        if final:
            out = _rms(out, gfin_ref[...])
        o_ref[0, rs, :] = out

    x, y = prologue(parts[0])
    for r, rs in enumerate(parts):
        acc = swiglu(y)
        x_cur = x
        if r + 1 < len(parts):
            x, y = prologue(parts[r + 1])
        epilogue(rs, x_cur, acc)


def _ffn_call(h, mod, g, w_in, w_out, layer, mod_off, mix=None, g_final=None):
    bx, t, d = h.shape
    tm = min(FFN_STEP, t)
    final = g_final is not None
    tile = lambda width: pl.BlockSpec((1, tm, width), lambda b, i: (b, i, 0))
    in_specs = [tile(d),
                pl.BlockSpec((1, N_MOD, d), lambda b, i: (b, 0, 0)),
                _const_spec((1, d), 2),
                _layer_spec((d, 2 * D_FF), layer),
                _layer_spec((D_FF, d), layer)]
    args = [h, mod, g.reshape(1, d), w_in, w_out]
    if mix is not None:
        in_specs += [tile(GLA_WIDTH), tile(SWA_WIDTH), tile(DIFF_WIDTH), _layer_spec((MIX_WIDTH, d), layer)]
        args += list(mix)
    if final:
        in_specs.append(_const_spec((1, d), 2))
        args.append(g_final.reshape(1, d))
    return pl.pallas_call(
        functools.partial(_ffn_kernel, mod_off=mod_off, mix=mix is not None, final=final),
        grid=(bx, t // tm),
        in_specs=in_specs,
        out_specs=pl.BlockSpec((1, tm, d), lambda b, i: (b, i, 0)),
        out_shape=jax.ShapeDtypeStruct(h.shape, F32),
        compiler_params=_params(2),
        name="ffn",
    )(*args)


SWA_QSCALE = SWA_HD ** -0.5 * LOG2E
DIFF_QSCALE = DIFF_QK ** -0.5 * LOG2E


def _rot_half(blk, quarter, first):
    return jnp.where(first, pltpu.roll(blk, LANE - quarter, 1), pltpu.roll(blk, quarter, 1))


def _inproj_kernel(*refs, rope):
    if rope:
        h_ref, mod_ref, g_ref, w_ref, tab_ref, pg_ref, ps_ref, pd_ref, kts_ref, kt_ref = refs
    else:
        h_ref, mod_ref, g_ref, w_ref, pg_ref, ps_ref, pd_ref, kts_ref, kt_ref = refs
    mod = mod_ref[0]
    rows = h_ref.shape[1] // 2
    halves = [slice(0, rows), slice(rows, 2 * rows)]

    def project(rs):
        y = (_rms(h_ref[0, rs, :], g_ref[...]) * (1.0 + mod[4:5]) + mod[3:4]).astype(BF16)
        return _dot(y, w_ref[...])

    def emit(rs, p):
        pg_ref[0, rs, :] = p[:, :PG_W]
        swa = [p[:, PG_W + i * LANE:PG_W + (i + 1) * LANE] for i in range(PS_COLS // LANE)]
        dif = [p[:, PG_W + PS_COLS + i * LANE:PG_W + PS_COLS + (i + 1) * LANE] for i in range(PD_COLS // LANE)]
        if rope:
            tab = [tab_ref[rs, i * LANE:(i + 1) * LANE] for i in range(8)]
            lane = lax.broadcasted_iota(jnp.int32, (1, LANE), 1)
            first_s = (lane % (SWA_HD // 2)) < (SWA_HD // 4)
            first_d = (lane % (DIFF_QK // 2)) < (DIFF_QK // 4)
            for i in range(5):
                c, s = (tab[0], tab[1]) if i < 4 else (tab[2], tab[3])
                swa[i] = swa[i] * c + _rot_half(swa[i], SWA_HD // 4, first_s) * s
            for i in range(4):
                c, s = (tab[4], tab[5]) if i < 2 else (tab[6], tab[7])
                dif[i] = dif[i] * c + _rot_half(dif[i], DIFF_QK // 4, first_d) * s
        else:
            for i in range(4):
                swa[i] = swa[i] * SWA_QSCALE
            for i in range(2):
                dif[i] = dif[i] * DIFF_QSCALE
        ps_ref[0, rs, :] = jnp.concatenate(swa[0:4] + swa[5:7], axis=1).astype(BF16)
        kt = swa[4].T
        k0, k1 = kt[:SWA_HD], kt[SWA_HD:]
        kts_ref[0, :, rs] = jnp.concatenate([k0, k0, k1, k1], axis=0).astype(BF16)
        pd_ref[0, rs, :] = jnp.concatenate(dif[0:2] + dif[4:6], axis=1).astype(BF16)
        kt_ref[0, :, rs] = jnp.concatenate(dif[2:4], axis=1).T.astype(BF16)

    pa = project(halves[0])
    pb = project(halves[1])
    emit(halves[0], pa)
    emit(halves[1], pb)


def _inproj_call(h, mod, g, w, layer, tab=None):
    bx, t, d = h.shape
    tm = min(FFN_TM, t)
    rope = tab is not None
    in_specs = [pl.BlockSpec((1, tm, d), lambda j, b: (b, j, 0)),
                pl.BlockSpec((1, N_MOD, d), lambda j, b: (b, 0, 0)),
                _const_spec((1, d), 2),
                _layer_spec((d, P_W), layer)]
    args = [h, mod, g.reshape(1, d), w]
    if rope:
        in_specs.append(pl.BlockSpec((tm, 8 * LANE), lambda j, b: (j, 0)))
        args.append(tab)
    out_map = lambda j, b: (b, j, 0)
    return pl.pallas_call(
        functools.partial(_inproj_kernel, rope=rope),
        grid=(t // tm, bx),
        in_specs=in_specs,
        out_specs=[pl.BlockSpec((1, tm, PG_W), out_map), pl.BlockSpec((1, tm, PS_W), out_map),
                   pl.BlockSpec((1, tm, PD_W), out_map),
                   pl.BlockSpec((1, 2 * LANE, tm), lambda j, b: (b, 0, j)),
                   pl.BlockSpec((1, DIFF_QK_W, tm), lambda j, b: (b, 0, j))],
        out_shape=[jax.ShapeDtypeStruct((bx, t, PG_W), F32), jax.ShapeDtypeStruct((bx, t, PS_W), BF16),
                   jax.ShapeDtypeStruct((bx, t, PD_W), BF16),
                   jax.ShapeDtypeStruct((bx, 2 * LANE, t), BF16),
                   jax.ShapeDtypeStruct((bx, DIFF_QK_W, t), BF16)],
        compiler_params=_params(2),
        name="inproj",
    )(*args)


GLA_SROWS = (GLA_LEVELS + 2) * GLA_CHUNK
GLA_LV_DIAG = GLA_LEVELS
GLA_LV_NONE = GLA_LEVELS + 1


def _gla_constants():
    c = GLA_CHUNK
    r = np.arange(c)[:, None]
    t = np.arange(c)[None, :]
    blocks = []
    for lv in range(GLA_LEVELS):
        half = 1 << lv
        mid = (r // (2 * half)) * (2 * half) + half
        second = r >= mid
        blocks.append(np.where(second, (t >= mid) & (t <= r), (t > r) & (t < mid)))
    blocks.append(t <= r)
    blocks.append(t > r)
    fwd = np.concatenate(blocks, axis=0).astype(np.float32)
    bwd = np.concatenate([b[::-1, ::-1] for b in blocks], axis=0).astype(np.float32)
    s2 = np.stack([np.concatenate([m, m], axis=1) for m in (fwd, bwd)])
    i = np.arange(c)[:, None]
    j = np.arange(c)[None, :]
    x = np.bitwise_xor(i, j)
    lvl = np.where(j > i, GLA_LV_NONE,
                   np.where(i == j, GLA_LV_DIAG, np.floor(np.log2(np.maximum(x, 1))).astype(np.int64)))
    lv_f = np.tile(lvl, (2, 2))
    lv_b = np.tile(lvl[::-1, ::-1], (2, 2))
    return jnp.asarray(s2, BF16), jnp.asarray(np.stack([lv_f, lv_b]), jnp.int32)


def _gla_kernel(*refs, ctx_out):
    if ctx_out:
        (pgx_ref, pgc_ref, wg_ref, bg_ref, gn_ref, s2_ref, lv_ref, ox_ref, oc_ref,
         oi_ref, qc_ref, upd_ref, dec_ref, st_ref, f_ref, att_ref) = refs
    else:
        (pgx_ref, pgc_ref, wg_ref, bg_ref, gn_ref, s2_ref, lv_ref, ox_ref,
         oi_ref, qc_ref, upd_ref, dec_ref, st_ref, f_ref, att_ref) = refs
        oc_ref = None
    c = GLA_CHUNK
    grp = GLA_GROUP
    hv = GLA_HEADS * GLA_DV
    n_ctx = pgc_ref.shape[1]
    s_len = pgx_ref.shape[1]
    row = lax.broadcasted_iota(jnp.int32, (hv, GLA_QK_W), 0)
    lane = lax.broadcasted_iota(jnp.int32, (hv, GLA_QK_W), 1)
    head_qk = (row // GLA_DV) == (lane // GLA_DK)
    row2 = lax.broadcasted_iota(jnp.int32, (hv, hv), 0)
    col2 = lax.broadcasted_iota(jnp.int32, (hv, hv), 1)
    head_v_b = ((row2 // GLA_DV) == (col2 // GLA_DV)).astype(F32).astype(BF16)
    gn = gn_ref[...]
    wh, wl = _split_bf16(jnp.concatenate([wg_ref[0], wg_ref[1]], axis=1))
    w3 = jnp.concatenate([wh, wh, wl], axis=0)
    bias = jnp.concatenate([bg_ref[0], bg_ref[1]], axis=1)

    lane1 = lax.broadcasted_iota(jnp.int32, (1, GLA_QK_W), 1)
    pair_mask = [((lane1 // (2 * GLA_DK)) == p).astype(F32).astype(BF16) for p in range(2)]
    parity_mask = [(((lane1 // GLA_DK) % 2) == hh).astype(F32).astype(BF16) for hh in range(2)]
    lane_v = lax.broadcasted_iota(jnp.int32, (1, hv), 1)
    hv_mask = [((lane_v // GLA_DV) == h).astype(F32).astype(BF16) for h in range(GLA_HEADS)]

    def pair_rows(a):
        a = a.astype(BF16)
        return jnp.concatenate([a * pair_mask[0], a * pair_mask[1]], axis=0)

    def parity_rows(a):
        a = a.astype(BF16)
        return jnp.concatenate([a * parity_mask[0], a * parity_mask[1]], axis=0)

    def local(src_ref, src_row, dst_row, with_out, grp):
        def src_rows(g):
            return pl.ds(pl.multiple_of(src_row + g * c, c), c)

        def q_of(g):
            return src_ref[0, src_rows(g), 0:128] * (GLA_DK ** -0.5)

        def k_of(g):
            return src_ref[0, src_rows(g), 128:256]

        def v_of(g):
            return src_ref[0, src_rows(g), 256:512].astype(BF16)

        zh, zl = _split_bf16(jnp.concatenate([src_ref[0, src_rows(g), 768:896] for g in range(grp)], axis=0))
        z = _dot(jnp.concatenate([zh, zl, zh], axis=1), w3) + bias
        gate = (jnp.minimum(z, 0.0) - jnp.log(1.0 + jnp.exp(-jnp.abs(z)))) * (1.0 / GLA_TAU)
        gh, gl = _split_bf16(gate)
        for d in range(2):
            cols = slice(d * GLA_QK_W, (d + 1) * GLA_QK_W)
            g2 = jnp.concatenate([jnp.concatenate([gh[g * c:(g + 1) * c, cols], gl[g * c:(g + 1) * c, cols]], axis=0)
                                  for g in range(grp)], axis=1)
            f_ref[d, :, 0:grp * GLA_QK_W] = jnp.exp(_dot(s2_ref[d], g2))

        def fac(d, g, block):
            return f_ref[d, block * c:(block + 1) * c, g * GLA_QK_W:(g + 1) * GLA_QK_W]

        def vbd_of(g, p):
            v = v_of(g)
            return jnp.concatenate([v * hv_mask[2 * p], v * hv_mask[2 * p + 1]], axis=0)

        if with_out:
            diag = [_dot_nt(pair_rows(q_of(g)), parity_rows(k_of(g))) for g in range(grp)]
        for d in range(2):
            for g in range(grp):
                rows = pl.ds(pl.multiple_of(dst_row + g * c, c), c)
                ci = (dst_row + g * c) // c
                upd = _dot_tn(v_of(g), (k_of(g) * fac(d, g, GLA_LEVELS + 1)).astype(BF16))
                upd_ref[d, ci] = jnp.where(head_qk, upd, 0.0)
                fcum = fac(d, g, GLA_LEVELS)
                last = c - 1 if d == 0 else 0
                dec_ref[d, ci] = jnp.broadcast_to(fcum[last:last + 1], (8, GLA_QK_W))
                if with_out:
                    qc_ref[d, rows, :] = (q_of(g) * fcum).astype(BF16)
            if with_out:
                lv = lv_ref[d]
                for g in range(grp):
                    att_ref[g] = jnp.where(lv == GLA_LV_DIAG, diag[g], 0.0)
                for level in range(GLA_LEVELS):
                    for g in range(grp):
                        fl = fac(d, g, level)
                        s = _dot_nt(pair_rows(q_of(g) * fl), parity_rows(k_of(g) * fl))
                        att_ref[g] = jnp.where(lv == level, s, att_ref[g])
                for g in range(grp):
                    rows = pl.ds(pl.multiple_of(dst_row + g * c, c), c)
                    oi_ref[d, rows, :] = (_dot(att_ref[g, :c, :].astype(BF16), vbd_of(g, 0))
                                          + _dot(att_ref[g, c:, :].astype(BF16), vbd_of(g, 1)))

    def local_pass(src_ref, dst0, with_out, grp):
        def body(i, carry):
            local(src_ref, i * (grp * c), dst0 + i * (grp * c), with_out, grp)
            return carry
        lax.fori_loop(0, src_ref.shape[1] // (grp * c), body, 0)

    def scan_pass(first, n, with_out):
        def body(i, carry):
            for d in range(2):
                st = st_ref[d]
                for j in range(GLA_SCAN_UNROLL):
                    step = i * GLA_SCAN_UNROLL + j
                    ci = first + (step if d == 0 else n - 1 - step)
                    if with_out:
                        rows = pl.ds(pl.multiple_of(ci * c, c), c)
                        oi_ref[d, rows, :] = oi_ref[d, rows, :] + _dot_nt(qc_ref[d, rows, :], st.astype(BF16))
                    st = dec_ref[d, ci][0:1] * st + upd_ref[d, ci]
                st_ref[d] = st
            return carry
        lax.fori_loop(0, n // GLA_SCAN_UNROLL, body, 0)

    def finish_pass(src_ref, src0, out_ref, n_rows):
        tile = grp * c

        def body(i, carry):
            r = pl.multiple_of(i * tile, tile)
            rs = pl.ds(pl.multiple_of(src0 + r, tile), tile)
            o = oi_ref[0, rs, :] + oi_ref[1, rs, :]
            hi, lo = _split_bf16(o * o)
            ms = (_dot(hi, head_v_b) + _dot(lo, head_v_b)) * (1.0 / GLA_DV)
            og = src_ref[0, pl.ds(r, tile), 512:768]
            out_ref[0, pl.ds(r, tile), :] = (o * lax.rsqrt(ms + RMS_EPS) * gn * _silu(og)).astype(BF16)
            return carry
        lax.fori_loop(0, n_rows // tile, body, 0)

    local_pass(pgc_ref, 0, ctx_out, GLA_GROUP)
    local_pass(pgx_ref, n_ctx, True, GLA_GROUP_X)
    st_ref[...] = jnp.zeros(st_ref.shape, F32)
    scan_pass(0, n_ctx // c, ctx_out)
    scan_pass(n_ctx // c, s_len // c, True)
    finish_pass(pgx_ref, n_ctx, ox_ref, s_len)
    if ctx_out:
        finish_pass(pgc_ref, 0, oc_ref, n_ctx)


def _gla_call(pgx, pgc, wg, bg, gn, s2, lv, ctx_out):
    b, s, _ = pgx.shape
    n_ctx = pgc.shape[1]
    hv = GLA_WIDTH
    n_rows = s + n_ctx
    assert s % (GLA_GROUP_X * GLA_CHUNK) == 0 and n_ctx % (GLA_GROUP * GLA_CHUNK) == 0
    assert (s // GLA_CHUNK) % GLA_SCAN_UNROLL == 0 and (n_ctx // GLA_CHUNK) % GLA_SCAN_UNROLL == 0
    in_specs = [pl.BlockSpec((1, s, PG_W), lambda i: (i, 0, 0)),
                pl.BlockSpec((1, n_ctx, PG_W), lambda i: (i, 0, 0)),
                _const_spec(wg.shape, 1), _const_spec(bg.shape, 1), _const_spec((1, hv), 1),
                _const_spec(s2.shape, 1), _const_spec(lv.shape, 1)]
    out_specs = [pl.BlockSpec((1, s, hv), lambda i: (i, 0, 0))]
    out_shape = [jax.ShapeDtypeStruct((b, s, hv), BF16)]
    if ctx_out:
        out_specs.append(pl.BlockSpec((1, n_ctx, hv), lambda i: (i, 0, 0)))
        out_shape.append(jax.ShapeDtypeStruct((b, n_ctx, hv), BF16))
    scratch = [pltpu.VMEM((2, n_rows, hv), F32),
               pltpu.VMEM((2, n_rows, GLA_QK_W), BF16),
               pltpu.VMEM((2, n_rows // GLA_CHUNK, hv, GLA_QK_W), F32),
               pltpu.VMEM((2, n_rows // GLA_CHUNK, 8, GLA_QK_W), F32),
               pltpu.VMEM((2, hv, GLA_QK_W), F32),
               pltpu.VMEM((2, GLA_SROWS, max(GLA_GROUP, GLA_GROUP_X) * GLA_QK_W), F32),
               pltpu.VMEM((max(GLA_GROUP, GLA_GROUP_X), 2 * GLA_CHUNK, 2 * GLA_CHUNK), F32)]
    res = pl.pallas_call(
        functools.partial(_gla_kernel, ctx_out=ctx_out),
        grid=(b,),
        in_specs=in_specs, out_specs=out_specs, out_shape=out_shape, scratch_shapes=scratch,
        compiler_params=_params(1),
        name="gla",
    )(pgx, pgc, wg, bg, gn.reshape(1, hv), s2, lv)
    return (res[0], res[1]) if ctx_out else (res[0], None)


def _swa_kernel(*refs, ctx_out):
    if ctx_out:
        sink_ref, px_ref, ktx_ref, pc_ref, ktc_ref, ox_ref, oc_ref, sc_ref = refs
    else:
        sink_ref, px_ref, ktx_ref, pc_ref, ktc_ref, ox_ref, sc_ref = refs
    s_len = px_ref.shape[1]
    n_ctx = pc_ref.shape[1]
    blk = SWA_BLOCK
    band = 3 * blk
    grp = SWA_HEADS // SWA_KV_HEADS
    lane = lax.broadcasted_iota(jnp.int32, (1, LANE), 1)
    low = lane < SWA_HD
    half_mask = [low.astype(F32).astype(BF16), (~low).astype(F32).astype(BF16)]
    rel = (lax.broadcasted_iota(jnp.int32, (blk, band), 0)
           - lax.broadcasted_iota(jnp.int32, (blk, band), 1))

    def col(i):
        return slice(i * LANE, (i + 1) * LANE)

    def krows(kv):
        return slice(kv * LANE, (kv + 1) * LANE)

    def sink_col(n_rows, kv, half):
        ha = grp * kv + half
        return jnp.concatenate([jnp.full((n_rows, 1), sink_ref[ha], F32),
                                jnp.full((n_rows, 1), sink_ref[ha + 2], F32)], axis=0)

    def softmax_pv(sc, v, snk):
        m = jnp.maximum(snk, jnp.max(sc(), axis=-1, keepdims=True))
        p = jnp.exp2(sc() - m).astype(BF16)
        r = _dot(p, jnp.concatenate([v, jnp.ones(v.shape, BF16)], axis=1))
        return r[:, :LANE] / (r[:, LANE:] + jnp.exp2(snk - m))

    def pipeline(tasks, depth, carried=(), prefetch=()):
        for t in range(len(carried), min(depth, len(tasks))):
            sc_ref[t] = tasks[t][0]()
        outs = []
        for t, (_, finish) in enumerate(tasks):
            ahead = t + depth
            if ahead < len(tasks):
                sc_ref[ahead] = tasks[ahead][0]()
            elif ahead - len(tasks) < len(prefetch):
                prefetch[ahead - len(tasks)]()
            outs.append(finish(lambda t=t: sc_ref[t]))
        return outs

    units = [(kv, half) for kv in range(SWA_KV_HEADS) for half in range(2)]
    blocks_per_step = 2
    n_steps = s_len // (blk * blocks_per_step)
    depth = SWA_PIPE_DEPTH

    def block_tasks(n):
        r0 = pl.multiple_of(n * blk, blk)
        start = pl.multiple_of(jnp.clip((n - 1) * blk, 0, s_len - band), blk)
        tasks = []
        for kv, half in units:
            def scores(kv=kv, half=half):
                bias = jnp.where(jnp.abs(rel + (r0 - start)) <= SWA_WINDOW, 0.0, -jnp.inf)
                bias = jnp.concatenate([bias, jnp.zeros((blk, n_ctx), F32)], axis=1)
                q = jnp.concatenate([px_ref[0, pl.ds(r0, blk), col(2 * kv)],
                                     px_ref[0, pl.ds(r0, blk), col(2 * kv + 1)]], axis=0) * half_mask[half]
                kt = jnp.concatenate([ktx_ref[0, krows(kv), pl.ds(start, band)],
                                      ktc_ref[0, krows(kv), :]], axis=1)
                return _dot(q, kt) + jnp.concatenate([bias, bias], axis=0)

            def finish(cur, kv=kv, half=half):
                v = jnp.concatenate([px_ref[0, pl.ds(start, band), col(4 + kv)],
                                     pc_ref[0, :, col(4 + kv)]], axis=0)
                return softmax_pv(cur, v, sink_col(blk, kv, half))

            tasks.append((scores, finish))
        return tasks, r0

    def prefetch_into(slot, scores):
        def run():
            sc_ref[slot] = scores()
        return run

    def body(i, carry):
        tasks, rows = [], []
        for sub in range(blocks_per_step):
            t, r0 = block_tasks(i * blocks_per_step + sub)
            tasks += t
            rows.append(r0)
        nxt, _ = block_tasks(jnp.minimum(i + 1, n_steps - 1) * blocks_per_step)
        outs = pipeline(tasks, depth, carried=range(depth),
                        prefetch=[prefetch_into(k, nxt[k][0]) for k in range(depth)])
        for sub in range(blocks_per_step):
            for kv in range(SWA_KV_HEADS):
                t = (sub * SWA_KV_HEADS + kv) * 2
                o2 = jnp.where(low, outs[t], outs[t + 1]).astype(BF16)
                ox_ref[0, pl.ds(rows[sub], blk), col(2 * kv)] = o2[:blk]
                ox_ref[0, pl.ds(rows[sub], blk), col(2 * kv + 1)] = o2[blk:]
        return carry

    first, _ = block_tasks(jnp.int32(0))
    for k in range(depth):
        sc_ref[k] = first[k][0]()
    lax.fori_loop(0, n_steps, body, 0)

    if ctx_out:
        tasks = []
        for kv, half in units:
            def scores(kv=kv, half=half):
                q = jnp.concatenate([pc_ref[0, :, col(2 * kv)], pc_ref[0, :, col(2 * kv + 1)]], axis=0)
                return _dot(q * half_mask[half], ktc_ref[0, krows(kv), :])

            def finish(cur, kv=kv, half=half):
                return softmax_pv(cur, pc_ref[0, :, col(4 + kv)], sink_col(n_ctx, kv, half))

            tasks.append((scores, finish))
        outs = []
        nxt = tasks[0][0]()
        for t, (_, finish) in enumerate(tasks):
            cur = nxt
            if t + 1 < len(tasks):
                nxt = tasks[t + 1][0]()
            outs.append(finish(lambda cur=cur: cur))
        for kv in range(SWA_KV_HEADS):
            o2 = jnp.where(low, outs[2 * kv], outs[2 * kv + 1]).astype(BF16)
            oc_ref[0, :, col(2 * kv)] = o2[:n_ctx]
            oc_ref[0, :, col(2 * kv + 1)] = o2[n_ctx:]


def _swa_call(psx, ktx, psc, ktc, sink, ctx_out):
    b, s, _ = psx.shape
    n_ctx = psc.shape[1]
    assert s % (2 * SWA_BLOCK) == 0 and s >= 3 * SWA_BLOCK and SWA_WINDOW == SWA_BLOCK
    in_specs = [pl.BlockSpec(memory_space=pltpu.SMEM),
                pl.BlockSpec((1, s, PS_W), lambda i: (i, 0, 0)),
                pl.BlockSpec((1, 2 * LANE, s), lambda i: (i, 0, 0)),
                pl.BlockSpec((1, n_ctx, PS_W), lambda i: (i, 0, 0)),
                pl.BlockSpec((1, 2 * LANE, n_ctx), lambda i: (0, 0, i))]
    out_specs = [pl.BlockSpec((1, s, SWA_WIDTH), lambda i: (i, 0, 0))]
    out_shape = [jax.ShapeDtypeStruct((b, s, SWA_WIDTH), BF16)]
    if ctx_out:
        out_specs.append(pl.BlockSpec((1, n_ctx, SWA_WIDTH), lambda i: (i, 0, 0)))
        out_shape.append(jax.ShapeDtypeStruct((b, n_ctx, SWA_WIDTH), BF16))
    res = pl.pallas_call(
        functools.partial(_swa_kernel, ctx_out=ctx_out),
        grid=(b,),
        in_specs=in_specs, out_specs=out_specs, out_shape=out_shape,
        scratch_shapes=[pltpu.VMEM((SWA_UNITS_PER_STEP, 2 * SWA_BLOCK, 3 * SWA_BLOCK + n_ctx), F32)],
        compiler_params=_params(1),
        name="swa",
    )(sink, psx, ktx, psc, ktc)
    return (res[0], res[1]) if ctx_out else (res[0], None)


def _diff_kernel(*refs, ctx_out, lambda_init):
    if ctx_out:
        px_ref, ktx_ref, pc_ref, ktc_ref, lam_ref, gn_ref, ox_ref, oc_ref, kt_ref, va_ref, sc_ref = refs
    else:
        px_ref, ktx_ref, pc_ref, ktc_ref, lam_ref, gn_ref, ox_ref, kt_ref, va_ref, sc_ref = refs
    s_len = px_ref.shape[1]
    w = DIFF_QK_W
    lv = lam_ref[...]
    lam = (jnp.exp(jnp.sum(lv[0:1] * lv[1:2], axis=-1, keepdims=True))
           - jnp.exp(jnp.sum(lv[2:3] * lv[3:4], axis=-1, keepdims=True)) + lambda_init)
    lane = lax.broadcasted_iota(jnp.int32, (1, w), 1)
    row2 = lax.broadcasted_iota(jnp.int32, (w, w), 0)
    col2 = lax.broadcasted_iota(jnp.int32, (w, w), 1)
    head_ones = ((row2 // DIFF_V) == (col2 // DIFF_V)).astype(F32).astype(BF16)
    gn = gn_ref[...] * (1.0 - lambda_init)
    unit_masks = [((lane >= DIFF_QK * u) & (lane < DIFF_QK * (u + 1))).astype(F32).astype(BF16)
                  for u in range(2 * DIFF_HEADS)]
    head_masks = [(lane >= DIFF_V * h) & (lane < DIFF_V * (h + 1)) for h in range(DIFF_HEADS)]

    kt_ref[:, :s_len] = ktx_ref[0]
    kt_ref[:, s_len:] = ktc_ref[0]
    for side in range(2):
        keep = ((lane >= side * (w // 2)) & (lane < (side + 1) * (w // 2))).astype(F32).astype(BF16)
        va_ref[side, :s_len, :] = px_ref[0, :, w:2 * w] * keep + (1.0 - keep)
        va_ref[side, s_len:, :] = pc_ref[0, :, w:2 * w] * keep + (1.0 - keep)

    def attend(q, k0, q_next=None):
        n_units = 2 * DIFF_HEADS

        def scores(qq, u):
            return _dot(qq * unit_masks[u], kt_ref[:, k0:])

        def softmax_pv(sc, u):
            m = jnp.max(sc(), axis=-1, keepdims=True)
            return _dot(jnp.exp2(sc() - m).astype(BF16), va_ref[(u // 2) // (DIFF_HEADS // 2), k0:, :])

        units = []
        if q_next is None:
            nxt = scores(q, 0)
            for u in range(n_units):
                cur = nxt
                if u + 1 < n_units:
                    nxt = scores(q, u + 1)
                units.append(softmax_pv(lambda cur=cur: cur, u))
        else:
            for u in range(n_units):
                sc_ref[(u + 1) % 2] = scores(q, u + 1) if u + 1 < n_units else scores(q_next, 0)
                units.append(softmax_pv(lambda u=u: sc_ref[u % 2], u))
        o = jnp.zeros((q.shape[0], w), F32)
        for h in range(DIFF_HEADS):
            r1, r2 = units[2 * h], units[2 * h + 1]
            oh = r1 / pltpu.roll(r1, w // 2, 1) - lam * (r2 / pltpu.roll(r2, w // 2, 1))
            o = jnp.where(head_masks[h], oh, o)
        hi, lo = _split_bf16(o * o)
        ms = (_dot(hi, head_ones) + _dot(lo, head_ones)) * (1.0 / DIFF_V)
        return (o * lax.rsqrt(ms + RMS_EPS) * gn).astype(BF16)

    n_blocks = s_len // DIFF_QBLOCK

    def q_block(n):
        return px_ref[0, pl.ds(pl.multiple_of(n * DIFF_QBLOCK, DIFF_QBLOCK), DIFF_QBLOCK), 0:w]

    def body(n, carry):
        out = attend(q_block(n), 0, q_block(jnp.minimum(n + 1, n_blocks - 1)))
        ox_ref[0, pl.ds(pl.multiple_of(n * DIFF_QBLOCK, DIFF_QBLOCK), DIFF_QBLOCK), :] = out
        return carry

    sc_ref[0] = _dot(q_block(0) * unit_masks[0], kt_ref[...])
    lax.fori_loop(0, n_blocks, body, 0)
    if ctx_out:
        oc_ref[0] = attend(pc_ref[0, :, 0:w], s_len)


def _diff_call(pdx, ktx, pdc, ktc, lam, gn, lambda_init, ctx_out):
    b, s, _ = pdx.shape
    n_ctx = pdc.shape[1]
    assert s % DIFF_QBLOCK == 0
    in_specs = [pl.BlockSpec((1, s, PD_W), lambda i: (i, 0, 0)),
                pl.BlockSpec((1, DIFF_QK_W, s), lambda i: (i, 0, 0)),
                pl.BlockSpec((1, n_ctx, PD_W), lambda i: (i, 0, 0)),
                pl.BlockSpec((1, DIFF_QK_W, n_ctx), lambda i: (0, 0, i)),
                _const_spec(lam.shape, 1), _const_spec((1, DIFF_WIDTH), 1)]
    out_specs = [pl.BlockSpec((1, s, DIFF_WIDTH), lambda i: (i, 0, 0))]
    out_shape = [jax.ShapeDtypeStruct((b, s, DIFF_WIDTH), BF16)]
    if ctx_out:
        out_specs.append(pl.BlockSpec((1, n_ctx, DIFF_WIDTH), lambda i: (i, 0, 0)))
        out_shape.append(jax.ShapeDtypeStruct((b, n_ctx, DIFF_WIDTH), BF16))
    res = pl.pallas_call(
        functools.partial(_diff_kernel, ctx_out=ctx_out, lambda_init=lambda_init),
        grid=(b,),
        in_specs=in_specs, out_specs=out_specs, out_shape=out_shape,
        scratch_shapes=[pltpu.VMEM((DIFF_QK_W, s + n_ctx), BF16),
                        pltpu.VMEM((2, s + n_ctx, DIFF_WIDTH), BF16),
                        pltpu.VMEM((2, DIFF_QBLOCK, s + n_ctx), F32)],
        compiler_params=_params(1),
        name="diff",
    )(pdx, ktx, pdc, ktc, lam, gn.reshape(1, DIFF_WIDTH))
    return (res[0], res[1]) if ctx_out else (res[0], None)


def _prep_w_in(w):
    w = w.astype(BF16)
    starts = np.concatenate([[0], np.cumsum(IN_SIZES)])
    gq, gk, gv, gf, gb, og, sq, sk, sv, dq, dk, dv = range(len(IN_SIZES))

    def cols(first, last):
        return w[..., starts[first]:starts[last + 1]]

    pad = jnp.zeros(w.shape[:-1] + (PG_W - (2 * GLA_QK_W + 2 * GLA_WIDTH + 2 * GLA_GATE_RANK),), BF16)
    v0 = w[..., starts[sv]:starts[sv] + SWA_HD]
    v1 = w[..., starts[sv] + SWA_HD:starts[sv + 1]]
    return jnp.concatenate([cols(gq, gv), cols(og, og), cols(gf, gb), pad, cols(sq, sk), v0, v0, v1, v1,
                            cols(dq, dv)], axis=-1)


def _prep_gate(w_gate, b_gate):
    wg = jnp.zeros((2, LANE, GLA_QK_W), F32)
    for d in range(2):
        wg = wg.at[d, GLA_GATE_RANK * d:GLA_GATE_RANK * (d + 1), :].set(w_gate[d])
    return wg, b_gate.reshape(2, 1, GLA_QK_W)


def _axial_angles(rows, head_dim):
    half = head_dim // 2
    row = np.repeat(np.arange(rows, dtype=np.float64), GRID_W)
    col = np.tile(np.arange(GRID_W, dtype=np.float64), rows)
    inv_freq = 1.0 / (ROPE_BASE ** (np.arange(0, half, 2, dtype=np.float64) / half))

    def axis_angles(pos):
        a = pos[:, None] * inv_freq[None, :]
        return np.concatenate([a, a], axis=-1)

    return np.concatenate([axis_angles(row), axis_angles(col)], axis=-1)


def _rope_table(seq):
    rows = seq // GRID_W
    blocks = []
    for head_dim, qscale in ((SWA_HD, SWA_QSCALE), (DIFF_QK, DIFF_QSCALE)):
        ang = _axial_angles(rows, head_dim)
        quarter = head_dim // 4
        sign = np.where((np.arange(head_dim) % (2 * quarter)) < quarter, -1.0, 1.0)
        reps = LANE // head_dim
        cos = np.tile(np.cos(ang), (1, reps))
        sin = np.tile(np.sin(ang) * sign[None, :], (1, reps))
        blocks += [cos * qscale, sin * qscale, cos, sin]
    return jnp.asarray(np.concatenate(blocks, axis=1), F32)


def kernel(x, c, ctx, c_ctx, w_mod, b_mod, g_ffn1, w_ffn1_in, w_ffn1_out, g_mix, w_in, w_out, w_gla_gate,
           b_gla_gate, g_gla_norm, swa_sink, diff_lambda, g_diff_norm, g_ffn2, w_ffn2_in, w_ffn2_out, g_final):
    b, s, d = x.shape
    n_ctx = ctx.shape[1]
    depth = w_mod.shape[0]

    cs = jnp.zeros((MOD_ROWS, d), F32).at[:b].set(c).at[b].set(c_ctx)
    mod = _mod_call(cs, w_mod, b_mod).reshape(depth, MOD_ROWS, N_MOD, d)
    tab = _rope_table(s)
    s2, lv = _gla_constants()

    w1i, w1o = w_ffn1_in.astype(BF16), w_ffn1_out.astype(BF16)
    w2i, w2o = w_ffn2_in.astype(BF16), w_ffn2_out.astype(BF16)
    wi = _prep_w_in(w_in)
    wo = w_out.astype(BF16)

    hx = x
    hc = ctx.reshape(1, b * n_ctx, d)
    for l in range(depth):
        ctx_out = l < depth - 1
        last = l == depth - 1
        mod_x = mod[l, :b]
        mod_c = mod[l, b:b + 1]
        lambda_init = 0.8 - 0.6 * math.exp(-0.3 * l)
        wg, bg = _prep_gate(w_gla_gate[l], b_gla_gate[l])
        sink = swa_sink[l] * LOG2E

        hx = _ffn_call(hx, mod_x, g_ffn1[l], w1i, w1o, l, 0)
        hc = _ffn_call(hc, mod_c, g_ffn1[l], w1i, w1o, l, 0)
        pgx, psx, pdx, ksx, ktx = _inproj_call(hx, mod_x, g_mix[l], wi, l, tab)
        pgc, psc, pdc, ksc, ktc = _inproj_call(hc, mod_c, g_mix[l], wi, l)
        pgc, psc, pdc = (a.reshape(b, n_ctx, a.shape[-1]) for a in (pgc, psc, pdc))

        gla_x, gla_c = _gla_call(pgx, pgc, wg, bg, g_gla_norm[l], s2, lv, ctx_out)
        swa_x, swa_c = _swa_call(psx, ksx, psc, ksc, sink, ctx_out)
        dif_x, dif_c = _diff_call(pdx, ktx, pdc, ktc, diff_lambda[l], g_diff_norm[l], lambda_init, ctx_out)

        hx = _ffn_call(hx, mod_x, g_ffn2[l], w2i, w2o, l, 6, mix=(gla_x, swa_x, dif_x, wo),
                       g_final=g_final if last else None)
        if ctx_out:
            flat = lambda a: a.reshape(1, b * n_ctx, a.shape[-1])
            hc = _ffn_call(hc, mod_c, g_ffn2[l], w2i, w2o, l, 6, mix=(flat(gla_c), flat(swa_c), flat(dif_c), wo))
    return hx
```

```python
import functools
import math

import numpy as np
import jax
import jax.numpy as jnp
from jax import lax
from jax.experimental import pallas as pl
from jax.experimental.pallas import tpu as pltpu

F32 = jnp.float32
BF16 = jnp.bfloat16

D_MODEL = 1024
DEPTH = 2
GRID_W = 64
N_MOD = 9
D_FF = 2816
RMS_EPS = 1e-6
ROPE_BASE = 10000.0

GLA_HEADS = 4
GLA_DK = 32
GLA_DV = 64
GLA_GATE_RANK = 16
GLA_TAU = 16.0
SWA_HEADS = 8
SWA_KV_HEADS = 2
SWA_HD = 64
SWA_WINDOW = 128
SWA_BLOCK = 128
DIFF_HEADS = 4
DIFF_QK = 32
DIFF_V = 64
DIFF_QBLOCK = 512

GLA_QK_W = GLA_HEADS * GLA_DK
GLA_WIDTH = GLA_HEADS * GLA_DV
SWA_WIDTH = SWA_HEADS * SWA_HD
SWA_KV_W = SWA_KV_HEADS * SWA_HD
DIFF_QK_W = DIFF_HEADS * 2 * DIFF_QK
DIFF_WIDTH = DIFF_HEADS * DIFF_V
MIX_WIDTH = GLA_WIDTH + SWA_WIDTH + DIFF_WIDTH
IN_SIZES = (GLA_QK_W, GLA_QK_W, GLA_WIDTH, GLA_GATE_RANK, GLA_GATE_RANK, GLA_WIDTH,
            SWA_WIDTH, SWA_KV_W, SWA_KV_W, DIFF_QK_W, DIFF_QK_W, DIFF_WIDTH)

LANE = 128
VMEM_LIMIT = 56 * 1024 * 1024
LOG2E = math.log2(math.e)

PG_W = 896
PS_COLS = 896
PS_W = 768
PD_COLS = 768
PD_W = 512
P_W = PG_W + PS_COLS + PD_COLS
GLA_CHUNK = 64
GLA_LEVELS = 6
GLA_GROUP = 4
GLA_GROUP_X = 16
GLA_SCAN_UNROLL = 4
SWA_UNITS_PER_STEP = 8
SWA_PIPE_DEPTH = 2
MOD_ROWS = 16
MOD_TN = 2304
FFN_TM = 512
FFN_STEP = 1024
FFN_ROWS = 256


def _dot(a, b):
    return jnp.dot(a, b, preferred_element_type=F32)


def _dot_nt(a, b):
    return lax.dot_general(a, b, (((1,), (1,)), ((), ())), preferred_element_type=F32)


def _dot_tn(a, b):
    return lax.dot_general(a, b, (((0,), (0,)), ((), ())), preferred_element_type=F32)


def _split_bf16(x):
    hi = x.astype(BF16)
    lo = (x - hi.astype(F32)).astype(BF16)
    return hi, lo


def _rms(x, g):
    return x * lax.rsqrt(jnp.mean(x * x, axis=-1, keepdims=True) + RMS_EPS) * g


def _silu(x):
    return x * (1.0 / (1.0 + jnp.exp(-x)))


def _params(n_grid):
    return pltpu.CompilerParams(dimension_semantics=("arbitrary",) * n_grid, vmem_limit_bytes=VMEM_LIMIT)


def _const_spec(shape, n_grid, single=False):
    zeros = (0,) * len(shape)
    index_map = {1: lambda a: zeros, 2: lambda a, b: zeros}[n_grid]
    if single:
        return pl.BlockSpec(shape, index_map, pipeline_mode=pl.Buffered(1))
    return pl.BlockSpec(shape, index_map)


def _layer_spec(shape, layer):
    return pl.BlockSpec((None,) + tuple(shape), lambda b, i: (layer, 0, 0), pipeline_mode=pl.Buffered(1))


def _mod_kernel(a_ref, w_ref, b_ref, o_ref):
    a = _silu(a_ref[...]).astype(BF16)
    o_ref[0] = _dot(a, w_ref[0].astype(BF16)) + b_ref[0]


def _mod_call(cs, w_mod, b_mod):
    n_layers, d, n = w_mod.shape
    return pl.pallas_call(
        _mod_kernel,
        grid=(n_layers, n // MOD_TN),
        in_specs=[pl.BlockSpec((MOD_ROWS, d), lambda l, j: (0, 0)),
                  pl.BlockSpec((1, d, MOD_TN), lambda l, j: (l, 0, j)),
                  pl.BlockSpec((1, 1, MOD_TN), lambda l, j: (l, 0, j))],
        out_specs=pl.BlockSpec((1, MOD_ROWS, MOD_TN), lambda l, j: (l, 0, j)),
        out_shape=jax.ShapeDtypeStruct((n_layers, MOD_ROWS, n), F32),
        compiler_params=_params(2),
        name="mod",
    )(cs, w_mod, b_mod.reshape(n_layers, 1, n))


def _ffn_kernel(*refs, mod_off, mix, final):
    refs = list(refs)
    h_ref, mod_ref, g_ref, win_ref, wout_ref = refs[:5]
    o_ref = refs[-1]
    mod = mod_ref[0]
    if mix:
        yg_ref, ys_ref, yd_ref, wmix_ref = refs[5:9]
    if final:
        gfin_ref = refs[-2]
    shift, scale, gate = mod[mod_off:mod_off + 1], mod[mod_off + 1:mod_off + 2], mod[mod_off + 2:mod_off + 3]
    parts = [slice(r, r + FFN_ROWS) for r in range(0, h_ref.shape[1], FFN_ROWS)]

    def prologue(rs):
        x = h_ref[0, rs, :]
        if mix:
            a = GLA_WIDTH
            b = GLA_WIDTH + SWA_WIDTH
            proj = (_dot(yg_ref[0, rs, :], wmix_ref[0:a, :]) + _dot(ys_ref[0, rs, :], wmix_ref[a:b, :])
                    + _dot(yd_ref[0, rs, :], wmix_ref[b:, :]))
            x = x + mod[5:6] * proj
        return x, (_rms(x, g_ref[...]) * (1.0 + scale) + shift).astype(BF16)

    def swiglu(y):
        gt = _dot(y, win_ref[:, :D_FF])
        up = _dot(y, win_ref[:, D_FF:])
        return _dot((_silu(gt) * up).astype(BF16), wout_ref[...])

    def epilogue(rs, x, acc):
        out = x + (0.5 * gate) * acc
        if final:
            out = _rms(out, gfin_ref[...])
        o_ref[0, rs, :] = out

    x, y = prologue(parts[0])
    for r, rs in enumerate(parts):
        acc = swiglu(y)
        x_cur = x
        if r + 1 < len(parts):
            x, y = prologue(parts[r + 1])
        epilogue(rs, x_cur, acc)


def _ffn_call(h, mod, g, w_in, w_out, layer, mod_off, mix=None, g_final=None):
    bx, t, d = h.shape
    tm = min(FFN_STEP, t)
    final = g_final is not None
    tile = lambda width: pl.BlockSpec((1, tm, width), lambda b, i: (b, i, 0))
    in_specs = [tile(d),
                pl.BlockSpec((1, N_MOD, d), lambda b, i: (b, 0, 0)),
                _const_spec((1, d), 2),
                _layer_spec((d, 2 * D_FF), layer),
                _layer_spec((D_FF, d), layer)]
    args = [h, mod, g.reshape(1, d), w_in, w_out]
    if mix is not None:
        in_specs += [tile(GLA_WIDTH), tile(SWA_WIDTH), tile(DIFF_WIDTH), _layer_spec((MIX_WIDTH, d), layer)]
        args += list(mix)
    if final:
        in_specs.append(_const_spec((1, d), 2))
        args.append(g_final.reshape(1, d))
    return pl.pallas_call(
        functools.partial(_ffn_kernel, mod_off=mod_off, mix=mix is not None, final=final),
        grid=(bx, t // tm),
        in_specs=in_specs,
        out_specs=pl.BlockSpec((1, tm, d), lambda b, i: (b, i, 0)),
        out_shape=jax.ShapeDtypeStruct(h.shape, F32),
        compiler_params=_params(2),
        name="ffn",
    )(*args)


SWA_QSCALE = SWA_HD ** -0.5 * LOG2E
DIFF_QSCALE = DIFF_QK ** -0.5 * LOG2E


def _rot_half(blk, quarter, first):
    return jnp.where(first, pltpu.roll(blk, LANE - quarter, 1), pltpu.roll(blk, quarter, 1))


def _inproj_kernel(*refs, rope):
    if rope:
        h_ref, mod_ref, g_ref, w_ref, tab_ref, pg_ref, ps_ref, pd_ref, kts_ref, kt_ref = refs
    else:
        h_ref, mod_ref, g_ref, w_ref, pg_ref, ps_ref, pd_ref, kts_ref, kt_ref = refs
    mod = mod_ref[0]
    rows = h_ref.shape[1] // 2
    halves = [slice(0, rows), slice(rows, 2 * rows)]

    def project(rs):
        y = (_rms(h_ref[0, rs, :], g_ref[...]) * (1.0 + mod[4:5]) + mod[3:4]).astype(BF16)
        return _dot(y, w_ref[...])

    def emit(rs, p):
        pg_ref[0, rs, :] = p[:, :PG_W]
        swa = [p[:, PG_W + i * LANE:PG_W + (i + 1) * LANE] for i in range(PS_COLS // LANE)]
        dif = [p[:, PG_W + PS_COLS + i * LANE:PG_W + PS_COLS + (i + 1) * LANE] for i in range(PD_COLS // LANE)]
        if rope:
            tab = [tab_ref[rs, i * LANE:(i + 1) * LANE] for i in range(8)]
            lane = lax.broadcasted_iota(jnp.int32, (1, LANE), 1)
            first_s = (lane % (SWA_HD // 2)) < (SWA_HD // 4)
            first_d = (lane % (DIFF_QK // 2)) < (DIFF_QK // 4)
            for i in range(5):
                c, s = (tab[0], tab[1]) if i < 4 else (tab[2], tab[3])
                swa[i] = swa[i] * c + _rot_half(swa[i], SWA_HD // 4, first_s) * s
            for i in range(4):
                c, s = (tab[4], tab[5]) if i < 2 else (tab[6], tab[7])
                dif[i] = dif[i] * c + _rot_half(dif[i], DIFF_QK // 4, first_d) * s
        else:
            for i in range(4):
                swa[i] = swa[i] * SWA_QSCALE
            for i in range(2):
                dif[i] = dif[i] * DIFF_QSCALE
        ps_ref[0, rs, :] = jnp.concatenate(swa[0:4] + swa[5:7], axis=1).astype(BF16)
        kt = swa[4].T
        k0, k1 = kt[:SWA_HD], kt[SWA_HD:]
        kts_ref[0, :, rs] = jnp.concatenate([k0, k0, k1, k1], axis=0).astype(BF16)
        pd_ref[0, rs, :] = jnp.concatenate(dif[0:2] + dif[4:6], axis=1).astype(BF16)
        kt_ref[0, :, rs] = jnp.concatenate(dif[2:4], axis=1).T.astype(BF16)

    pa = project(halves[0])
    pb = project(halves[1])
    emit(halves[0], pa)
    emit(halves[1], pb)


def _inproj_call(h, mod, g, w, layer, tab=None):
    bx, t, d = h.shape
    tm = min(FFN_TM, t)
    rope = tab is not None
    in_specs = [pl.BlockSpec((1, tm, d), lambda j, b: (b, j, 0)),
                pl.BlockSpec((1, N_MOD, d), lambda j, b: (b, 0, 0)),
                _const_spec((1, d), 2),
                _layer_spec((d, P_W), layer)]
    args = [h, mod, g.reshape(1, d), w]
    if rope:
        in_specs.append(pl.BlockSpec((tm, 8 * LANE), lambda j, b: (j, 0)))
        args.append(tab)
    out_map = lambda j, b: (b, j, 0)
    return pl.pallas_call(
        functools.partial(_inproj_kernel, rope=rope),
        grid=(t // tm, bx),
        in_specs=in_specs,
        out_specs=[pl.BlockSpec((1, tm, PG_W), out_map), pl.BlockSpec((1, tm, PS_W), out_map),
                   pl.BlockSpec((1, tm, PD_W), out_map),
                   pl.BlockSpec((1, 2 * LANE, tm), lambda j, b: (b, 0, j)),
                   pl.BlockSpec((1, DIFF_QK_W, tm), lambda j, b: (b, 0, j))],
        out_shape=[jax.ShapeDtypeStruct((bx, t, PG_W), F32), jax.ShapeDtypeStruct((bx, t, PS_W), BF16),
                   jax.ShapeDtypeStruct((bx, t, PD_W), BF16),
                   jax.ShapeDtypeStruct((bx, 2 * LANE, t), BF16),
                   jax.ShapeDtypeStruct((bx, DIFF_QK_W, t), BF16)],
        compiler_params=_params(2),
        name="inproj",
    )(*args)


GLA_SROWS = (GLA_LEVELS + 2) * GLA_CHUNK
GLA_LV_DIAG = GLA_LEVELS
GLA_LV_NONE = GLA_LEVELS + 1


def _gla_constants():
    c = GLA_CHUNK
    r = np.arange(c)[:, None]
    t = np.arange(c)[None, :]
    blocks = []
    for lv in range(GLA_LEVELS):
        half = 1 << lv
        mid = (r // (2 * half)) * (2 * half) + half
        second = r >= mid
        blocks.append(np.where(second, (t >= mid) & (t <= r), (t > r) & (t < mid)))
    blocks.append(t <= r)
    blocks.append(t > r)
    fwd = np.concatenate(blocks, axis=0).astype(np.float32)
    bwd = np.concatenate([b[::-1, ::-1] for b in blocks], axis=0).astype(np.float32)
    s2 = np.stack([np.concatenate([m, m], axis=1) for m in (fwd, bwd)])
    i = np.arange(c)[:, None]
    j = np.arange(c)[None, :]
    x = np.bitwise_xor(i, j)
    lvl = np.where(j > i, GLA_LV_NONE,
                   np.where(i == j, GLA_LV_DIAG, np.floor(np.log2(np.maximum(x, 1))).astype(np.int64)))
    lv_f = np.tile(lvl, (2, 2))
    lv_b = np.tile(lvl[::-1, ::-1], (2, 2))
    return jnp.asarray(s2, BF16), jnp.asarray(np.stack([lv_f, lv_b]), jnp.int32)


def _gla_kernel(*refs, ctx_out):
    if ctx_out:
        (pgx_ref, pgc_ref, wg_ref, bg_ref, gn_ref, s2_ref, lv_ref, ox_ref, oc_ref,
         oi_ref, qc_ref, upd_ref, dec_ref, st_ref, f_ref) = refs
    else:
        (pgx_ref, pgc_ref, wg_ref, bg_ref, gn_ref, s2_ref, lv_ref, ox_ref,
         oi_ref, qc_ref, upd_ref, dec_ref, st_ref, f_ref) = refs
        oc_ref = None
    c = GLA_CHUNK
    grp = GLA_GROUP
    hv = GLA_HEADS * GLA_DV
    n_ctx = pgc_ref.shape[1]
    s_len = pgx_ref.shape[1]
    row = lax.broadcasted_iota(jnp.int32, (hv, GLA_QK_W), 0)
    lane = lax.broadcasted_iota(jnp.int32, (hv, GLA_QK_W), 1)
    head_qk = (row // GLA_DV) == (lane // GLA_DK)
    row2 = lax.broadcasted_iota(jnp.int32, (hv, hv), 0)
    col2 = lax.broadcasted_iota(jnp.int32, (hv, hv), 1)
    head_v_b = ((row2 // GLA_DV) == (col2 // GLA_DV)).astype(F32).astype(BF16)
    gn = gn_ref[...]
    wh, wl = _split_bf16(jnp.concatenate([wg_ref[0], wg_ref[1]], axis=1))
    w3 = jnp.concatenate([wh, wh, wl], axis=0)
    bias = jnp.concatenate([bg_ref[0], bg_ref[1]], axis=1)

    lane1 = lax.broadcasted_iota(jnp.int32, (1, GLA_QK_W), 1)
    pair_mask = [((lane1 // (2 * GLA_DK)) == p).astype(F32).astype(BF16) for p in range(2)]
    parity_mask = [(((lane1 // GLA_DK) % 2) == hh).astype(F32).astype(BF16) for hh in range(2)]
    lane_v = lax.broadcasted_iota(jnp.int32, (1, hv), 1)
    hv_mask = [((lane_v // GLA_DV) == h).astype(F32).astype(BF16) for h in range(GLA_HEADS)]

    def pair_rows(a):
        a = a.astype(BF16)
        return jnp.concatenate([a * pair_mask[0], a * pair_mask[1]], axis=0)

    def parity_rows(a):
        a = a.astype(BF16)
        return jnp.concatenate([a * parity_mask[0], a * parity_mask[1]], axis=0)

    def local(src_ref, src_row, dst_row, with_out, grp):
        def src_rows(g):
            return pl.ds(pl.multiple_of(src_row + g * c, c), c)

        def q_of(g):
            return src_ref[0, src_rows(g), 0:128] * (GLA_DK ** -0.5)

        def k_of(g):
            return src_ref[0, src_rows(g), 128:256]

        def v_of(g):
            return src_ref[0, src_rows(g), 256:512].astype(BF16)

        zh, zl = _split_bf16(jnp.concatenate([src_ref[0, src_rows(g), 768:896] for g in range(grp)], axis=0))
        z = _dot(jnp.concatenate([zh, zl, zh], axis=1), w3) + bias
        gate = (jnp.minimum(z, 0.0) - jnp.log(1.0 + jnp.exp(-jnp.abs(z)))) * (1.0 / GLA_TAU)
        gh, gl = _split_bf16(gate)
        for d in range(2):
            cols = slice(d * GLA_QK_W, (d + 1) * GLA_QK_W)
            g2 = jnp.concatenate([jnp.concatenate([gh[g * c:(g + 1) * c, cols], gl[g * c:(g + 1) * c, cols]], axis=0)
                                  for g in range(grp)], axis=1)
            f_ref[d, :, 0:grp * GLA_QK_W] = jnp.exp(_dot(s2_ref[d], g2))

        def fac(d, g, block):
            return f_ref[d, block * c:(block + 1) * c, g * GLA_QK_W:(g + 1) * GLA_QK_W]

        def vbd_of(g, p):
            v = v_of(g)
            return jnp.concatenate([v * hv_mask[2 * p], v * hv_mask[2 * p + 1]], axis=0)

        if with_out:
            diag = [_dot_nt(pair_rows(q_of(g)), parity_rows(k_of(g))) for g in range(grp)]
        for d in range(2):
            for g in range(grp):
                rows = pl.ds(pl.multiple_of(dst_row + g * c, c), c)
                ci = (dst_row + g * c) // c
                upd = _dot_tn(v_of(g), (k_of(g) * fac(d, g, GLA_LEVELS + 1)).astype(BF16))
                upd_ref[d, ci] = jnp.where(head_qk, upd, 0.0)
                fcum = fac(d, g, GLA_LEVELS)
                last = c - 1 if d == 0 else 0
                dec_ref[d, ci] = jnp.broadcast_to(fcum[last:last + 1], (8, GLA_QK_W))
                if with_out:
                    qc_ref[d, rows, :] = (q_of(g) * fcum).astype(BF16)
            if with_out:
                lv = lv_ref[d]
                atts = [jnp.where(lv == GLA_LV_DIAG, dg, 0.0) for dg in diag]
                for level in range(GLA_LEVELS):
                    for g in range(grp):
                        fl = fac(d, g, level)
                        s = _dot_nt(pair_rows(q_of(g) * fl), parity_rows(k_of(g) * fl))
                        atts[g] = jnp.where(lv == level, s, atts[g])
                for g in range(grp):
                    rows = pl.ds(pl.multiple_of(dst_row + g * c, c), c)
                    att = atts[g].astype(BF16)
                    oi_ref[d, rows, :] = _dot(att[:c], vbd_of(g, 0)) + _dot(att[c:], vbd_of(g, 1))

    def local_pass(src_ref, dst0, with_out, grp):
        def body(i, carry):
            local(src_ref, i * (grp * c), dst0 + i * (grp * c), with_out, grp)
            return carry
        lax.fori_loop(0, src_ref.shape[1] // (grp * c), body, 0)

    def scan_pass(first, n, with_out):
        def body(i, carry):
            for d in range(2):
                st = st_ref[d]
                for j in range(GLA_SCAN_UNROLL):
                    step = i * GLA_SCAN_UNROLL + j
                    ci = first + (step if d == 0 else n - 1 - step)
                    if with_out:
                        rows = pl.ds(pl.multiple_of(ci * c, c), c)
                        oi_ref[d, rows, :] = oi_ref[d, rows, :] + _dot_nt(qc_ref[d, rows, :], st.astype(BF16))
                    st = dec_ref[d, ci][0:1] * st + upd_ref[d, ci]
                st_ref[d] = st
            return carry
        lax.fori_loop(0, n // GLA_SCAN_UNROLL, body, 0)

    def finish_pass(src_ref, src0, out_ref, n_rows):
        tile = grp * c

        def body(i, carry):
            r = pl.multiple_of(i * tile, tile)
            rs = pl.ds(pl.multiple_of(src0 + r, tile), tile)
            o = oi_ref[0, rs, :] + oi_ref[1, rs, :]
            hi, lo = _split_bf16(o * o)
            ms = (_dot(hi, head_v_b) + _dot(lo, head_v_b)) * (1.0 / GLA_DV)
            og = src_ref[0, pl.ds(r, tile), 512:768]
            out_ref[0, pl.ds(r, tile), :] = (o * lax.rsqrt(ms + RMS_EPS) * gn * _silu(og)).astype(BF16)
            return carry
        lax.fori_loop(0, n_rows // tile, body, 0)

    local_pass(pgc_ref, 0, ctx_out, GLA_GROUP)
    local_pass(pgx_ref, n_ctx, True, GLA_GROUP_X)
    st_ref[...] = jnp.zeros(st_ref.shape, F32)
    scan_pass(0, n_ctx // c, ctx_out)
    scan_pass(n_ctx // c, s_len // c, True)
    finish_pass(pgx_ref, n_ctx, ox_ref, s_len)
    if ctx_out:
        finish_pass(pgc_ref, 0, oc_ref, n_ctx)


def _gla_call(pgx, pgc, wg, bg, gn, s2, lv, ctx_out):
    b, s, _ = pgx.shape
    n_ctx = pgc.shape[1]
    hv = GLA_WIDTH
    n_rows = s + n_ctx
    assert s % (GLA_GROUP_X * GLA_CHUNK) == 0 and n_ctx % (GLA_GROUP * GLA_CHUNK) == 0
    assert (s // GLA_CHUNK) % GLA_SCAN_UNROLL == 0 and (n_ctx // GLA_CHUNK) % GLA_SCAN_UNROLL == 0
    in_specs = [pl.BlockSpec((1, s, PG_W), lambda i: (i, 0, 0)),
                pl.BlockSpec((1, n_ctx, PG_W), lambda i: (i, 0, 0)),
                _const_spec(wg.shape, 1), _const_spec(bg.shape, 1), _const_spec((1, hv), 1),
                _const_spec(s2.shape, 1), _const_spec(lv.shape, 1)]
    out_specs = [pl.BlockSpec((1, s, hv), lambda i: (i, 0, 0))]
    out_shape = [jax.ShapeDtypeStruct((b, s, hv), BF16)]
    if ctx_out:
        out_specs.append(pl.BlockSpec((1, n_ctx, hv), lambda i: (i, 0, 0)))
        out_shape.append(jax.ShapeDtypeStruct((b, n_ctx, hv), BF16))
    scratch = [pltpu.VMEM((2, n_rows, hv), F32),
               pltpu.VMEM((2, n_rows, GLA_QK_W), BF16),
               pltpu.VMEM((2, n_rows // GLA_CHUNK, hv, GLA_QK_W), F32),
               pltpu.VMEM((2, n_rows // GLA_CHUNK, 8, GLA_QK_W), F32),
               pltpu.VMEM((2, hv, GLA_QK_W), F32),
               pltpu.VMEM((2, GLA_SROWS, max(GLA_GROUP, GLA_GROUP_X) * GLA_QK_W), F32)]
    res = pl.pallas_call(
        functools.partial(_gla_kernel, ctx_out=ctx_out),
        grid=(b,),
        in_specs=in_specs, out_specs=out_specs, out_shape=out_shape, scratch_shapes=scratch,
        compiler_params=_params(1),
        name="gla",
    )(pgx, pgc, wg, bg, gn.reshape(1, hv), s2, lv)
    return (res[0], res[1]) if ctx_out else (res[0], None)


def _swa_kernel(*refs, ctx_out):
    if ctx_out:
        sink_ref, px_ref, ktx_ref, pc_ref, ktc_ref, ox_ref, oc_ref, sc_ref = refs
    else:
        sink_ref, px_ref, ktx_ref, pc_ref, ktc_ref, ox_ref, sc_ref = refs
    s_len = px_ref.shape[1]
    n_ctx = pc_ref.shape[1]
    blk = SWA_BLOCK
    band = 3 * blk
    grp = SWA_HEADS // SWA_KV_HEADS
    lane = lax.broadcasted_iota(jnp.int32, (1, LANE), 1)
    low = lane < SWA_HD
    half_mask = [low.astype(F32).astype(BF16), (~low).astype(F32).astype(BF16)]
    rel = (lax.broadcasted_iota(jnp.int32, (blk, band), 0)
           - lax.broadcasted_iota(jnp.int32, (blk, band), 1))

    def col(i):
        return slice(i * LANE, (i + 1) * LANE)

    def krows(kv):
        return slice(kv * LANE, (kv + 1) * LANE)

    def sink_col(n_rows, kv, half):
        ha = grp * kv + half
        return jnp.concatenate([jnp.full((n_rows, 1), sink_ref[ha], F32),
                                jnp.full((n_rows, 1), sink_ref[ha + 2], F32)], axis=0)

    def softmax_pv(sc, v, snk):
        m = jnp.maximum(snk, jnp.max(sc(), axis=-1, keepdims=True))
        p = jnp.exp2(sc() - m).astype(BF16)
        r = _dot(p, jnp.concatenate([v, jnp.ones(v.shape, BF16)], axis=1))
        return r[:, :LANE] / (r[:, LANE:] + jnp.exp2(snk - m))

    def pipeline(tasks, depth, carried=(), prefetch=()):
        for t in range(len(carried), min(depth, len(tasks))):
            sc_ref[t] = tasks[t][0]()
        outs = []
        for t, (_, finish) in enumerate(tasks):
            ahead = t + depth
            if ahead < len(tasks):
                sc_ref[ahead] = tasks[ahead][0]()
            elif ahead - len(tasks) < len(prefetch):
                prefetch[ahead - len(tasks)]()
            outs.append(finish(lambda t=t: sc_ref[t]))
        return outs

    units = [(kv, half) for kv in range(SWA_KV_HEADS) for half in range(2)]
    blocks_per_step = 2
    n_steps = s_len // (blk * blocks_per_step)
    depth = SWA_PIPE_DEPTH

    def block_tasks(n):
        r0 = pl.multiple_of(n * blk, blk)
        start = pl.multiple_of(jnp.clip((n - 1) * blk, 0, s_len - band), blk)
        tasks = []
        for kv, half in units:
            def scores(kv=kv, half=half):
                bias = jnp.where(jnp.abs(rel + (r0 - start)) <= SWA_WINDOW, 0.0, -jnp.inf)
                bias = jnp.concatenate([bias, jnp.zeros((blk, n_ctx), F32)], axis=1)
                q = jnp.concatenate([px_ref[0, pl.ds(r0, blk), col(2 * kv)],
                                     px_ref[0, pl.ds(r0, blk), col(2 * kv + 1)]], axis=0) * half_mask[half]
                kt = jnp.concatenate([ktx_ref[0, krows(kv), pl.ds(start, band)],
                                      ktc_ref[0, krows(kv), :]], axis=1)
                return _dot(q, kt) + jnp.concatenate([bias, bias], axis=0)

            def finish(cur, kv=kv, half=half):
                v = jnp.concatenate([px_ref[0, pl.ds(start, band), col(4 + kv)],
                                     pc_ref[0, :, col(4 + kv)]], axis=0)
                return softmax_pv(cur, v, sink_col(blk, kv, half))

            tasks.append((scores, finish))
        return tasks, r0

    def prefetch_into(slot, scores):
        def run():
            sc_ref[slot] = scores()
        return run

    def body(i, carry):
        tasks, rows = [], []
        for sub in range(blocks_per_step):
            t, r0 = block_tasks(i * blocks_per_step + sub)
            tasks += t
            rows.append(r0)
        nxt, _ = block_tasks(jnp.minimum(i + 1, n_steps - 1) * blocks_per_step)
        outs = pipeline(tasks, depth, carried=range(depth),
                        prefetch=[prefetch_into(k, nxt[k][0]) for k in range(depth)])
        for sub in range(blocks_per_step):
            for kv in range(SWA_KV_HEADS):
                t = (sub * SWA_KV_HEADS + kv) * 2
                o2 = jnp.where(low, outs[t], outs[t + 1]).astype(BF16)
                ox_ref[0, pl.ds(rows[sub], blk), col(2 * kv)] = o2[:blk]
                ox_ref[0, pl.ds(rows[sub], blk), col(2 * kv + 1)] = o2[blk:]
        return carry

    first, _ = block_tasks(jnp.int32(0))
    for k in range(depth):
        sc_ref[k] = first[k][0]()
    lax.fori_loop(0, n_steps, body, 0)

    if ctx_out:
        tasks = []
        for kv, half in units:
            def scores(kv=kv, half=half):
                q = jnp.concatenate([pc_ref[0, :, col(2 * kv)], pc_ref[0, :, col(2 * kv + 1)]], axis=0)
                return _dot(q * half_mask[half], ktc_ref[0, krows(kv), :])

            def finish(cur, kv=kv, half=half):
                return softmax_pv(cur, pc_ref[0, :, col(4 + kv)], sink_col(n_ctx, kv, half))

            tasks.append((scores, finish))
        outs = []
        nxt = tasks[0][0]()
        for t, (_, finish) in enumerate(tasks):
            cur = nxt
            if t + 1 < len(tasks):
                nxt = tasks[t + 1][0]()
            outs.append(finish(lambda cur=cur: cur))
        for kv in range(SWA_KV_HEADS):
            o2 = jnp.where(low, outs[2 * kv], outs[2 * kv + 1]).astype(BF16)
            oc_ref[0, :, col(2 * kv)] = o2[:n_ctx]
            oc_ref[0, :, col(2 * kv + 1)] = o2[n_ctx:]


def _swa_call(psx, ktx, psc, ktc, sink, ctx_out):
    b, s, _ = psx.shape
    n_ctx = psc.shape[1]
    assert s % (2 * SWA_BLOCK) == 0 and s >= 3 * SWA_BLOCK and SWA_WINDOW == SWA_BLOCK
    in_specs = [pl.BlockSpec(memory_space=pltpu.SMEM),
                pl.BlockSpec((1, s, PS_W), lambda i: (i, 0, 0)),
                pl.BlockSpec((1, 2 * LANE, s), lambda i: (i, 0, 0)),
                pl.BlockSpec((1, n_ctx, PS_W), lambda i: (i, 0, 0)),
                pl.BlockSpec((1, 2 * LANE, n_ctx), lambda i: (0, 0, i))]
    out_specs = [pl.BlockSpec((1, s, SWA_WIDTH), lambda i: (i, 0, 0))]
    out_shape = [jax.ShapeDtypeStruct((b, s, SWA_WIDTH), BF16)]
    if ctx_out:
        out_specs.append(pl.BlockSpec((1, n_ctx, SWA_WIDTH), lambda i: (i, 0, 0)))
        out_shape.append(jax.ShapeDtypeStruct((b, n_ctx, SWA_WIDTH), BF16))
    res = pl.pallas_call(
        functools.partial(_swa_kernel, ctx_out=ctx_out),
        grid=(b,),
        in_specs=in_specs, out_specs=out_specs, out_shape=out_shape,
        scratch_shapes=[pltpu.VMEM((SWA_UNITS_PER_STEP, 2 * SWA_BLOCK, 3 * SWA_BLOCK + n_ctx), F32)],
        compiler_params=_params(1),
        name="swa",
    )(sink, psx, ktx, psc, ktc)
    return (res[0], res[1]) if ctx_out else (res[0], None)


def _diff_kernel(*refs, ctx_out, lambda_init):
    if ctx_out:
        px_ref, ktx_ref, pc_ref, ktc_ref, lam_ref, gn_ref, ox_ref, oc_ref, kt_ref, va_ref, sc_ref = refs
    else:
        px_ref, ktx_ref, pc_ref, ktc_ref, lam_ref, gn_ref, ox_ref, kt_ref, va_ref, sc_ref = refs
    s_len = px_ref.shape[1]
    w = DIFF_QK_W
    lv = lam_ref[...]
    lam = (jnp.exp(jnp.sum(lv[0:1] * lv[1:2], axis=-1, keepdims=True))
           - jnp.exp(jnp.sum(lv[2:3] * lv[3:4], axis=-1, keepdims=True)) + lambda_init)
    lane = lax.broadcasted_iota(jnp.int32, (1, w), 1)
    row2 = lax.broadcasted_iota(jnp.int32, (w, w), 0)
    col2 = lax.broadcasted_iota(jnp.int32, (w, w), 1)
    head_ones = ((row2 // DIFF_V) == (col2 // DIFF_V)).astype(F32).astype(BF16)
    gn = gn_ref[...] * (1.0 - lambda_init)
    unit_masks = [((lane >= DIFF_QK * u) & (lane < DIFF_QK * (u + 1))).astype(F32).astype(BF16)
                  for u in range(2 * DIFF_HEADS)]
    head_masks = [(lane >= DIFF_V * h) & (lane < DIFF_V * (h + 1)) for h in range(DIFF_HEADS)]

    kt_ref[:, :s_len] = ktx_ref[0]
    kt_ref[:, s_len:] = ktc_ref[0]
    for side in range(2):
        keep = ((lane >= side * (w // 2)) & (lane < (side + 1) * (w // 2))).astype(F32).astype(BF16)
        va_ref[side, :s_len, :] = px_ref[0, :, w:2 * w] * keep + (1.0 - keep)
        va_ref[side, s_len:, :] = pc_ref[0, :, w:2 * w] * keep + (1.0 - keep)

    def attend(q, k0, q_next=None):
        def scores(qq, u):
            return _dot(qq * unit_masks[u], kt_ref[:, k0:])

        def softmax_pv(sc, u):
            m = jnp.max(sc, axis=-1, keepdims=True)
            return _dot(jnp.exp2(sc - m).astype(BF16), va_ref[(u // 2) // (DIFF_HEADS // 2), k0:, :])

        units = []
        nxt = scores(q, 0) if q_next is None else sc_ref[...]
        for u in range(2 * DIFF_HEADS):
            cur = nxt
            if u + 1 < 2 * DIFF_HEADS:
                nxt = scores(q, u + 1)
            elif q_next is not None:
                sc_ref[...] = scores(q_next, 0)
            units.append(softmax_pv(cur, u))
        o = jnp.zeros((q.shape[0], w), F32)
        for h in range(DIFF_HEADS):
            r1, r2 = units[2 * h], units[2 * h + 1]
            oh = r1 / pltpu.roll(r1, w // 2, 1) - lam * (r2 / pltpu.roll(r2, w // 2, 1))
            o = jnp.where(head_masks[h], oh, o)
        hi, lo = _split_bf16(o * o)
        ms = (_dot(hi, head_ones) + _dot(lo, head_ones)) * (1.0 / DIFF_V)
        return (o * lax.rsqrt(ms + RMS_EPS) * gn).astype(BF16)

    n_blocks = s_len // DIFF_QBLOCK

    def q_block(n):
        return px_ref[0, pl.ds(pl.multiple_of(n * DIFF_QBLOCK, DIFF_QBLOCK), DIFF_QBLOCK), 0:w]

    def body(n, carry):
        out = attend(q_block(n), 0, q_block(jnp.minimum(n + 1, n_blocks - 1)))
        ox_ref[0, pl.ds(pl.multiple_of(n * DIFF_QBLOCK, DIFF_QBLOCK), DIFF_QBLOCK), :] = out
        return carry

    sc_ref[...] = _dot(q_block(0) * unit_masks[0], kt_ref[...])
    lax.fori_loop(0, n_blocks, body, 0)
    if ctx_out:
        oc_ref[0] = attend(pc_ref[0, :, 0:w], s_len)


def _diff_call(pdx, ktx, pdc, ktc, lam, gn, lambda_init, ctx_out):
    b, s, _ = pdx.shape
    n_ctx = pdc.shape[1]
    assert s % DIFF_QBLOCK == 0
    in_specs = [pl.BlockSpec((1, s, PD_W), lambda i: (i, 0, 0)),
                pl.BlockSpec((1, DIFF_QK_W, s), lambda i: (i, 0, 0)),
                pl.BlockSpec((1, n_ctx, PD_W), lambda i: (i, 0, 0)),
                pl.BlockSpec((1, DIFF_QK_W, n_ctx), lambda i: (0, 0, i)),
                _const_spec(lam.shape, 1), _const_spec((1, DIFF_WIDTH), 1)]
    out_specs = [pl.BlockSpec((1, s, DIFF_WIDTH), lambda i: (i, 0, 0))]
    out_shape = [jax.ShapeDtypeStruct((b, s, DIFF_WIDTH), BF16)]
    if ctx_out:
        out_specs.append(pl.BlockSpec((1, n_ctx, DIFF_WIDTH), lambda i: (i, 0, 0)))
        out_shape.append(jax.ShapeDtypeStruct((b, n_ctx, DIFF_WIDTH), BF16))
    res = pl.pallas_call(
        functools.partial(_diff_kernel, ctx_out=ctx_out, lambda_init=lambda_init),
        grid=(b,),
        in_specs=in_specs, out_specs=out_specs, out_shape=out_shape,
        scratch_shapes=[pltpu.VMEM((DIFF_QK_W, s + n_ctx), BF16),
                        pltpu.VMEM((2, s + n_ctx, DIFF_WIDTH), BF16),
                        pltpu.VMEM((DIFF_QBLOCK, s + n_ctx), F32)],
        compiler_params=_params(1),
        name="diff",
    )(pdx, ktx, pdc, ktc, lam, gn.reshape(1, DIFF_WIDTH))
    return (res[0], res[1]) if ctx_out else (res[0], None)


def _prep_w_in(w):
    w = w.astype(BF16)
    starts = np.concatenate([[0], np.cumsum(IN_SIZES)])
    gq, gk, gv, gf, gb, og, sq, sk, sv, dq, dk, dv = range(len(IN_SIZES))

    def cols(first, last):
        return w[..., starts[first]:starts[last + 1]]

    pad = jnp.zeros(w.shape[:-1] + (PG_W - (2 * GLA_QK_W + 2 * GLA_WIDTH + 2 * GLA_GATE_RANK),), BF16)
    v0 = w[..., starts[sv]:starts[sv] + SWA_HD]
    v1 = w[..., starts[sv] + SWA_HD:starts[sv + 1]]
    return jnp.concatenate([cols(gq, gv), cols(og, og), cols(gf, gb), pad, cols(sq, sk), v0, v0, v1, v1,
                            cols(dq, dv)], axis=-1)


def _prep_gate(w_gate, b_gate):
    wg = jnp.stack([jnp.pad(w_gate[d], ((GLA_GATE_RANK * d, LANE - GLA_GATE_RANK * (d + 1)), (0, 0)))
                    for d in range(2)])
    return wg, b_gate.reshape(2, 1, GLA_QK_W)


def _axial_angles(rows, head_dim):
    half = head_dim // 2
    row = np.repeat(np.arange(rows, dtype=np.float64), GRID_W)
    col = np.tile(np.arange(GRID_W, dtype=np.float64), rows)
    inv_freq = 1.0 / (ROPE_BASE ** (np.arange(0, half, 2, dtype=np.float64) / half))

    def axis_angles(pos):
        a = pos[:, None] * inv_freq[None, :]
        return np.concatenate([a, a], axis=-1)

    return np.concatenate([axis_angles(row), axis_angles(col)], axis=-1)


def _rope_table(seq):
    rows = seq // GRID_W
    blocks = []
    for head_dim, qscale in ((SWA_HD, SWA_QSCALE), (DIFF_QK, DIFF_QSCALE)):
        ang = _axial_angles(rows, head_dim)
        quarter = head_dim // 4
        sign = np.where((np.arange(head_dim) % (2 * quarter)) < quarter, -1.0, 1.0)
        reps = LANE // head_dim
        cos = np.tile(np.cos(ang), (1, reps))
        sin = np.tile(np.sin(ang) * sign[None, :], (1, reps))
        blocks += [cos * qscale, sin * qscale, cos, sin]
    return jnp.asarray(np.concatenate(blocks, axis=1), F32)


def kernel(x, c, ctx, c_ctx, w_mod, b_mod, g_ffn1, w_ffn1_in, w_ffn1_out, g_mix, w_in, w_out, w_gla_gate,
           b_gla_gate, g_gla_norm, swa_sink, diff_lambda, g_diff_norm, g_ffn2, w_ffn2_in, w_ffn2_out, g_final):
    b, s, d = x.shape
    n_ctx = ctx.shape[1]
    depth = w_mod.shape[0]

    cs = jnp.concatenate([c, c_ctx[None, :], jnp.zeros((MOD_ROWS - b - 1, d), F32)], axis=0)
    mod = _mod_call(cs, w_mod, b_mod).reshape(depth, MOD_ROWS, N_MOD, d)
    tab = _rope_table(s)
    s2, lv = _gla_constants()

    w1i, w1o = w_ffn1_in.astype(BF16), w_ffn1_out.astype(BF16)
    w2i, w2o = w_ffn2_in.astype(BF16), w_ffn2_out.astype(BF16)
    wi = _prep_w_in(w_in)
    wo = w_out.astype(BF16)

    hx = x
    hc = ctx.reshape(1, b * n_ctx, d)
    for l in range(depth):
        ctx_out = l < depth - 1
        last = l == depth - 1
        mod_x = mod[l, :b]
        mod_c = mod[l, b:b + 1]
        lambda_init = 0.8 - 0.6 * math.exp(-0.3 * l)
        wg, bg = _prep_gate(w_gla_gate[l], b_gla_gate[l])
        sink = swa_sink[l] * LOG2E

        hx = _ffn_call(hx, mod_x, g_ffn1[l], w1i, w1o, l, 0)
        hc = _ffn_call(hc, mod_c, g_ffn1[l], w1i, w1o, l, 0)
        pgx, psx, pdx, ksx, ktx = _inproj_call(hx, mod_x, g_mix[l], wi, l, tab)
        pgc, psc, pdc, ksc, ktc = _inproj_call(hc, mod_c, g_mix[l], wi, l)
        pgc, psc, pdc = (a.reshape(b, n_ctx, a.shape[-1]) for a in (pgc, psc, pdc))

        gla_x, gla_c = _gla_call(pgx, pgc, wg, bg, g_gla_norm[l], s2, lv, ctx_out)
        swa_x, swa_c = _swa_call(psx, ksx, psc, ksc, sink, ctx_out)
        dif_x, dif_c = _diff_call(pdx, ktx, pdc, ktc, diff_lambda[l], g_diff_norm[l], lambda_init, ctx_out)

        hx = _ffn_call(hx, mod_x, g_ffn2[l], w2i, w2o, l, 6, mix=(gla_x, swa_x, dif_x, wo),
                       g_final=g_final if last else None)
        if ctx_out:
            flat = lambda a: a.reshape(1, b * n_ctx, a.shape[-1])
            hc = _ffn_call(hc, mod_c, g_ffn2[l], w2i, w2o, l, 6, mix=(flat(gla_c), flat(swa_c), flat(dif_c), wo))
    return hx
```

```python
import functools
import math

import numpy as np
import jax
import jax.numpy as jnp
from jax import lax
from jax.experimental import pallas as pl
from jax.experimental.pallas import tpu as pltpu

F32 = jnp.float32
BF16 = jnp.bfloat16

D_MODEL = 1024
DEPTH = 2
GRID_W = 64
N_MOD = 9
D_FF = 2816
RMS_EPS = 1e-6
ROPE_BASE = 10000.0

GLA_HEADS = 4
GLA_DK = 32
GLA_DV = 64
GLA_GATE_RANK = 16
GLA_TAU = 16.0
SWA_HEADS = 8
SWA_KV_HEADS = 2
SWA_HD = 64
SWA_WINDOW = 128
SWA_BLOCK = 128
DIFF_HEADS = 4
DIFF_QK = 32
DIFF_V = 64
DIFF_QBLOCK = 512

GLA_QK_W = GLA_HEADS * GLA_DK
GLA_WIDTH = GLA_HEADS * GLA_DV
SWA_WIDTH = SWA_HEADS * SWA_HD
SWA_KV_W = SWA_KV_HEADS * SWA_HD
DIFF_QK_W = DIFF_HEADS * 2 * DIFF_QK
DIFF_WIDTH = DIFF_HEADS * DIFF_V
MIX_WIDTH = GLA_WIDTH + SWA_WIDTH + DIFF_WIDTH
IN_SIZES = (GLA_QK_W, GLA_QK_W, GLA_WIDTH, GLA_GATE_RANK, GLA_GATE_RANK, GLA_WIDTH,
            SWA_WIDTH, SWA_KV_W, SWA_KV_W, DIFF_QK_W, DIFF_QK_W, DIFF_WIDTH)

LANE = 128
VMEM_LIMIT = 56 * 1024 * 1024
LOG2E = math.log2(math.e)

PG_W = 896
PS_COLS = 896
PS_W = 768
PD_COLS = 768
PD_W = 512
P_W = PG_W + PS_COLS + PD_COLS
GLA_CHUNK = 64
GLA_LEVELS = 6
GLA_GROUP = 4
GLA_GROUP_X = 16
GLA_SCAN_UNROLL = 4
SWA_UNITS_PER_STEP = 8
SWA_PIPE_DEPTH = 2
MOD_ROWS = 16
MOD_TN = 1152
FFN_TM = 512
FFN_STEP = 1024
FFN_ROWS = 256


def _dot(a, b):
    return jnp.dot(a, b, preferred_element_type=F32)


def _dot_nt(a, b):
    return lax.dot_general(a, b, (((1,), (1,)), ((), ())), preferred_element_type=F32)


def _dot_tn(a, b):
    return lax.dot_general(a, b, (((0,), (0,)), ((), ())), preferred_element_type=F32)


def _split_bf16(x):
    hi = x.astype(BF16)
    lo = (x - hi.astype(F32)).astype(BF16)
    return hi, lo


def _rms(x, g):
    return x * lax.rsqrt(jnp.mean(x * x, axis=-1, keepdims=True) + RMS_EPS) * g


def _silu(x):
    return x * (1.0 / (1.0 + jnp.exp(-x)))


def _params(n_grid):
    return pltpu.CompilerParams(dimension_semantics=("arbitrary",) * n_grid, vmem_limit_bytes=VMEM_LIMIT)


def _const_spec(shape, n_grid, single=False):
    zeros = (0,) * len(shape)
    index_map = {1: lambda a: zeros, 2: lambda a, b: zeros}[n_grid]
    if single:
        return pl.BlockSpec(shape, index_map, pipeline_mode=pl.Buffered(1))
    return pl.BlockSpec(shape, index_map)


def _layer_spec(shape, layer):
    return pl.BlockSpec((None,) + tuple(shape), lambda b, i: (layer, 0, 0), pipeline_mode=pl.Buffered(1))


def _mod_kernel(a_ref, w_ref, b_ref, o_ref):
    a = _silu(a_ref[...]).astype(BF16)
    o_ref[0] = _dot(a, w_ref[0].astype(BF16)) + b_ref[0]


def _mod_call(cs, w_mod, b_mod):
    n_layers, d, n = w_mod.shape
    return pl.pallas_call(
        _mod_kernel,
        grid=(n_layers, n // MOD_TN),
        in_specs=[pl.BlockSpec((MOD_ROWS, d), lambda l, j: (0, 0)),
                  pl.BlockSpec((1, d, MOD_TN), lambda l, j: (l, 0, j)),
                  pl.BlockSpec((1, 1, MOD_TN), lambda l, j: (l, 0, j))],
        out_specs=pl.BlockSpec((1, MOD_ROWS, MOD_TN), lambda l, j: (l, 0, j)),
        out_shape=jax.ShapeDtypeStruct((n_layers, MOD_ROWS, n), F32),
        compiler_params=_params(2),
        name="mod",
    )(cs, w_mod, b_mod.reshape(n_layers, 1, n))


def _ffn_kernel(*refs, mod_off, mix, final):
    refs = list(refs)
    h_ref, mod_ref, g_ref, win_ref, wout_ref = refs[:5]
    o_ref = refs[-1]
    mod = mod_ref[0]
    if mix:
        yg_ref, ys_ref, yd_ref, wmix_ref = refs[5:9]
    if final:
        gfin_ref = refs[-2]
    shift, scale, gate = mod[mod_off:mod_off + 1], mod[mod_off + 1:mod_off + 2], mod[mod_off + 2:mod_off + 3]
    parts = [slice(r, r + FFN_ROWS) for r in range(0, h_ref.shape[1], FFN_ROWS)]

    def prologue(rs):
        x = h_ref[0, rs, :]
        if mix:
            a = GLA_WIDTH
            b = GLA_WIDTH + SWA_WIDTH
            proj = (_dot(yg_ref[0, rs, :], wmix_ref[0:a, :]) + _dot(ys_ref[0, rs, :], wmix_ref[a:b, :])
                    + _dot(yd_ref[0, rs, :], wmix_ref[b:, :]))
            x = x + mod[5:6] * proj
        return x, (_rms(x, g_ref[...]) * (1.0 + scale) + shift).astype(BF16)

    def swiglu(y):
        gt = _dot(y, win_ref[:, :D_FF])
        up = _dot(y, win_ref[:, D_FF:])
        return _dot((_silu(gt) * up).astype(BF16), wout_ref[...])

    def epilogue(rs, x, acc):
        out = x + (0.5 * gate) * acc
        if final:
            out = _rms(out, gfin_ref[...])
        o_ref[0, rs, :] = out

    x, y = prologue(parts[0])
    for r, rs in enumerate(parts):
        acc = swiglu(y)
        x_cur = x
        if r + 1 < len(parts):
            x, y = prologue(parts[r + 1])
        epilogue(rs, x_cur, acc)


def _ffn_call(h, mod, g, w_in, w_out, layer, mod_off, mix=None, g_final=None):
    bx, t, d = h.shape
    tm = min(FFN_STEP, t)
    final = g_final is not None
    tile = lambda width: pl.BlockSpec((1, tm, width), lambda b, i: (b, i, 0))
    in_specs = [tile(d),
                pl.BlockSpec((1, N_MOD, d), lambda b, i: (b, 0, 0)),
                _const_spec((1, d), 2),
                _layer_spec((d, 2 * D_FF), layer),
                _layer_spec((D_FF, d), layer)]
    args = [h, mod, g.reshape(1, d), w_in, w_out]
    if mix is not None:
        in_specs += [tile(GLA_WIDTH), tile(SWA_WIDTH), tile(DIFF_WIDTH), _layer_spec((MIX_WIDTH, d), layer)]
        args += list(mix)
    if final:
        in_specs.append(_const_spec((1, d), 2))
        args.append(g_final.reshape(1, d))
    return pl.pallas_call(
        functools.partial(_ffn_kernel, mod_off=mod_off, mix=mix is not None, final=final),
        grid=(bx, t // tm),
        in_specs=in_specs,
        out_specs=pl.BlockSpec((1, tm, d), lambda b, i: (b, i, 0)),
        out_shape=jax.ShapeDtypeStruct(h.shape, F32),
        compiler_params=_params(2),
        name="ffn",
    )(*args)


SWA_QSCALE = SWA_HD ** -0.5 * LOG2E
DIFF_QSCALE = DIFF_QK ** -0.5 * LOG2E


def _rot_half(blk, quarter, first):
    return jnp.where(first, pltpu.roll(blk, LANE - quarter, 1), pltpu.roll(blk, quarter, 1))


def _inproj_kernel(*refs, rope):
    if rope:
        h_ref, mod_ref, g_ref, w_ref, tab_ref, pg_ref, ps_ref, pd_ref, kts_ref, kt_ref = refs
    else:
        h_ref, mod_ref, g_ref, w_ref, pg_ref, ps_ref, pd_ref, kts_ref, kt_ref = refs
    mod = mod_ref[0]
    rows = h_ref.shape[1] // 2
    halves = [slice(0, rows), slice(rows, 2 * rows)]

    def project(rs):
        y = (_rms(h_ref[0, rs, :], g_ref[...]) * (1.0 + mod[4:5]) + mod[3:4]).astype(BF16)
        return _dot(y, w_ref[...])

    def emit(rs, p):
        pg_ref[0, rs, :] = p[:, :PG_W]
        swa = [p[:, PG_W + i * LANE:PG_W + (i + 1) * LANE] for i in range(PS_COLS // LANE)]
        dif = [p[:, PG_W + PS_COLS + i * LANE:PG_W + PS_COLS + (i + 1) * LANE] for i in range(PD_COLS // LANE)]
        if rope:
            tab = [tab_ref[rs, i * LANE:(i + 1) * LANE] for i in range(8)]
            lane = lax.broadcasted_iota(jnp.int32, (1, LANE), 1)
            first_s = (lane % (SWA_HD // 2)) < (SWA_HD // 4)
            first_d = (lane % (DIFF_QK // 2)) < (DIFF_QK // 4)
            for i in range(5):
                c, s = (tab[0], tab[1]) if i < 4 else (tab[2], tab[3])
                swa[i] = swa[i] * c + _rot_half(swa[i], SWA_HD // 4, first_s) * s
            for i in range(4):
                c, s = (tab[4], tab[5]) if i < 2 else (tab[6], tab[7])
                dif[i] = dif[i] * c + _rot_half(dif[i], DIFF_QK // 4, first_d) * s
        else:
            for i in range(4):
                swa[i] = swa[i] * SWA_QSCALE
            for i in range(2):
                dif[i] = dif[i] * DIFF_QSCALE
        ps_ref[0, rs, :] = jnp.concatenate(swa[0:4] + swa[5:7], axis=1).astype(BF16)
        kt = swa[4].T
        k0, k1 = kt[:SWA_HD], kt[SWA_HD:]
        kts_ref[0, :, rs] = jnp.concatenate([k0, k0, k1, k1], axis=0).astype(BF16)
        pd_ref[0, rs, :] = jnp.concatenate(dif[0:2] + dif[4:6], axis=1).astype(BF16)
        kt_ref[0, :, rs] = jnp.concatenate(dif[2:4], axis=1).T.astype(BF16)

    pa = project(halves[0])
    pb = project(halves[1])
    emit(halves[0], pa)
    emit(halves[1], pb)


def _inproj_call(h, mod, g, w, layer, tab=None):
    bx, t, d = h.shape
    tm = min(FFN_TM, t)
    rope = tab is not None
    in_specs = [pl.BlockSpec((1, tm, d), lambda j, b: (b, j, 0)),
                pl.BlockSpec((1, N_MOD, d), lambda j, b: (b, 0, 0)),
                _const_spec((1, d), 2),
                _layer_spec((d, P_W), layer)]
    args = [h, mod, g.reshape(1, d), w]
    if rope:
        in_specs.append(pl.BlockSpec((tm, 8 * LANE), lambda j, b: (j, 0)))
        args.append(tab)
    out_map = lambda j, b: (b, j, 0)
    return pl.pallas_call(
        functools.partial(_inproj_kernel, rope=rope),
        grid=(t // tm, bx),
        in_specs=in_specs,
        out_specs=[pl.BlockSpec((1, tm, PG_W), out_map), pl.BlockSpec((1, tm, PS_W), out_map),
                   pl.BlockSpec((1, tm, PD_W), out_map),
                   pl.BlockSpec((1, 2 * LANE, tm), lambda j, b: (b, 0, j)),
                   pl.BlockSpec((1, DIFF_QK_W, tm), lambda j, b: (b, 0, j))],
        out_shape=[jax.ShapeDtypeStruct((bx, t, PG_W), F32), jax.ShapeDtypeStruct((bx, t, PS_W), BF16),
                   jax.ShapeDtypeStruct((bx, t, PD_W), BF16),
                   jax.ShapeDtypeStruct((bx, 2 * LANE, t), BF16),
                   jax.ShapeDtypeStruct((bx, DIFF_QK_W, t), BF16)],
        compiler_params=_params(2),
        name="inproj",
    )(*args)


GLA_SROWS = (GLA_LEVELS + 2) * GLA_CHUNK
GLA_LV_DIAG = GLA_LEVELS
GLA_LV_NONE = GLA_LEVELS + 1


def _gla_constants():
    c = GLA_CHUNK
    r = np.arange(c)[:, None]
    t = np.arange(c)[None, :]
    blocks = []
    for lv in range(GLA_LEVELS):
        half = 1 << lv
        mid = (r // (2 * half)) * (2 * half) + half
        second = r >= mid
        blocks.append(np.where(second, (t >= mid) & (t <= r), (t > r) & (t < mid)))
    blocks.append(t <= r)
    blocks.append(t > r)
    fwd = np.concatenate(blocks, axis=0).astype(np.float32)
    bwd = np.concatenate([b[::-1, ::-1] for b in blocks], axis=0).astype(np.float32)
    s2 = np.stack([np.concatenate([m, m], axis=1) for m in (fwd, bwd)])
    i = np.arange(c)[:, None]
    j = np.arange(c)[None, :]
    x = np.bitwise_xor(i, j)
    lvl = np.where(j > i, GLA_LV_NONE,
                   np.where(i == j, GLA_LV_DIAG, np.floor(np.log2(np.maximum(x, 1))).astype(np.int64)))
    lv_f = np.tile(lvl, (2, 2))
    lv_b = np.tile(lvl[::-1, ::-1], (2, 2))
    return jnp.asarray(s2, BF16), jnp.asarray(np.stack([lv_f, lv_b]), jnp.int32)


def _gla_kernel(*refs, ctx_out):
    if ctx_out:
        (pgx_ref, pgc_ref, wg_ref, bg_ref, gn_ref, s2_ref, lv_ref, ox_ref, oc_ref,
         oi_ref, qc_ref, upd_ref, dec_ref, st_ref, f_ref) = refs
    else:
        (pgx_ref, pgc_ref, wg_ref, bg_ref, gn_ref, s2_ref, lv_ref, ox_ref,
         oi_ref, qc_ref, upd_ref, dec_ref, st_ref, f_ref) = refs
        oc_ref = None
    c = GLA_CHUNK
    grp = GLA_GROUP
    hv = GLA_HEADS * GLA_DV
    n_ctx = pgc_ref.shape[1]
    s_len = pgx_ref.shape[1]
    row = lax.broadcasted_iota(jnp.int32, (hv, GLA_QK_W), 0)
    lane = lax.broadcasted_iota(jnp.int32, (hv, GLA_QK_W), 1)
    head_qk = (row // GLA_DV) == (lane // GLA_DK)
    row2 = lax.broadcasted_iota(jnp.int32, (hv, hv), 0)
    col2 = lax.broadcasted_iota(jnp.int32, (hv, hv), 1)
    head_v_b = ((row2 // GLA_DV) == (col2 // GLA_DV)).astype(F32).astype(BF16)
    gn = gn_ref[...]
    wh, wl = _split_bf16(jnp.concatenate([wg_ref[0], wg_ref[1]], axis=1))
    w3 = jnp.concatenate([wh, wh, wl], axis=0)
    bias = jnp.concatenate([bg_ref[0], bg_ref[1]], axis=1)

    lane1 = lax.broadcasted_iota(jnp.int32, (1, GLA_QK_W), 1)
    pair_mask = [((lane1 // (2 * GLA_DK)) == p).astype(F32).astype(BF16) for p in range(2)]
    parity_mask = [(((lane1 // GLA_DK) % 2) == hh).astype(F32).astype(BF16) for hh in range(2)]
    lane_v = lax.broadcasted_iota(jnp.int32, (1, hv), 1)
    hv_mask = [((lane_v // GLA_DV) == h).astype(F32).astype(BF16) for h in range(GLA_HEADS)]

    def pair_rows(a):
        a = a.astype(BF16)
        return jnp.concatenate([a * pair_mask[0], a * pair_mask[1]], axis=0)

    def parity_rows(a):
        a = a.astype(BF16)
        return jnp.concatenate([a * parity_mask[0], a * parity_mask[1]], axis=0)

    def local(src_ref, src_row, dst_row, with_out, grp):
        def src_rows(g):
            return pl.ds(pl.multiple_of(src_row + g * c, c), c)

        def q_of(g):
            return src_ref[0, src_rows(g), 0:128] * (GLA_DK ** -0.5)

        def k_of(g):
            return src_ref[0, src_rows(g), 128:256]

        def v_of(g):
            return src_ref[0, src_rows(g), 256:512].astype(BF16)

        zh, zl = _split_bf16(jnp.concatenate([src_ref[0, src_rows(g), 768:896] for g in range(grp)], axis=0))
        z = _dot(jnp.concatenate([zh, zl, zh], axis=1), w3) + bias
        gate = (jnp.minimum(z, 0.0) - jnp.log(1.0 + jnp.exp(-jnp.abs(z)))) * (1.0 / GLA_TAU)
        gh, gl = _split_bf16(gate)
        for d in range(2):
            cols = slice(d * GLA_QK_W, (d + 1) * GLA_QK_W)
            g2 = jnp.concatenate([jnp.concatenate([gh[g * c:(g + 1) * c, cols], gl[g * c:(g + 1) * c, cols]], axis=0)
                                  for g in range(grp)], axis=1)
            f_ref[d, :, 0:grp * GLA_QK_W] = jnp.exp(_dot(s2_ref[d], g2))

        def fac(d, g, block):
            return f_ref[d, block * c:(block + 1) * c, g * GLA_QK_W:(g + 1) * GLA_QK_W]

        def vbd_of(g, p):
            v = v_of(g)
            return jnp.concatenate([v * hv_mask[2 * p], v * hv_mask[2 * p + 1]], axis=0)

        if with_out:
            diag = [_dot_nt(pair_rows(q_of(g)), parity_rows(k_of(g))) for g in range(grp)]
        for d in range(2):
            for g in range(grp):
                rows = pl.ds(pl.multiple_of(dst_row + g * c, c), c)
                ci = (dst_row + g * c) // c
                upd = _dot_tn(v_of(g), (k_of(g) * fac(d, g, GLA_LEVELS + 1)).astype(BF16))
                upd_ref[d, ci] = jnp.where(head_qk, upd, 0.0)
                fcum = fac(d, g, GLA_LEVELS)
                last = c - 1 if d == 0 else 0
                dec_ref[d, ci] = jnp.broadcast_to(fcum[last:last + 1], (8, GLA_QK_W))
                if with_out:
                    qc_ref[d, rows, :] = (q_of(g) * fcum).astype(BF16)
            if with_out:
                lv = lv_ref[d]
                atts = [jnp.where(lv == GLA_LV_DIAG, dg, 0.0) for dg in diag]
                for level in range(GLA_LEVELS):
                    for g in range(grp):
                        fl = fac(d, g, level)
                        s = _dot_nt(pair_rows(q_of(g) * fl), parity_rows(k_of(g) * fl))
                        atts[g] = jnp.where(lv == level, s, atts[g])
                for g in range(grp):
                    rows = pl.ds(pl.multiple_of(dst_row + g * c, c), c)
                    att = atts[g].astype(BF16)
                    oi_ref[d, rows, :] = _dot(att[:c], vbd_of(g, 0)) + _dot(att[c:], vbd_of(g, 1))

    def local_pass(src_ref, dst0, with_out, grp):
        def body(i, carry):
            local(src_ref, i * (grp * c), dst0 + i * (grp * c), with_out, grp)
            return carry
        lax.fori_loop(0, src_ref.shape[1] // (grp * c), body, 0)

    def scan_pass(first, n, with_out):
        def body(i, carry):
            for d in range(2):
                st = st_ref[d]
                for j in range(GLA_SCAN_UNROLL):
                    step = i * GLA_SCAN_UNROLL + j
                    ci = first + (step if d == 0 else n - 1 - step)
                    if with_out:
                        rows = pl.ds(pl.multiple_of(ci * c, c), c)
                        oi_ref[d, rows, :] = oi_ref[d, rows, :] + _dot_nt(qc_ref[d, rows, :], st.astype(BF16))
                    st = dec_ref[d, ci][0:1] * st + upd_ref[d, ci]
                st_ref[d] = st
            return carry
        lax.fori_loop(0, n // GLA_SCAN_UNROLL, body, 0)

    def finish_pass(src_ref, src0, out_ref, n_rows):
        tile = grp * c

        def body(i, carry):
            r = pl.multiple_of(i * tile, tile)
            rs = pl.ds(pl.multiple_of(src0 + r, tile), tile)
            o = oi_ref[0, rs, :] + oi_ref[1, rs, :]
            hi, lo = _split_bf16(o * o)
            ms = (_dot(hi, head_v_b) + _dot(lo, head_v_b)) * (1.0 / GLA_DV)
            og = src_ref[0, pl.ds(r, tile), 512:768]
            out_ref[0, pl.ds(r, tile), :] = (o * lax.rsqrt(ms + RMS_EPS) * gn * _silu(og)).astype(BF16)
            return carry
        lax.fori_loop(0, n_rows // tile, body, 0)

    local_pass(pgc_ref, 0, ctx_out, GLA_GROUP)
    local_pass(pgx_ref, n_ctx, True, GLA_GROUP_X)
    st_ref[...] = jnp.zeros(st_ref.shape, F32)
    scan_pass(0, n_ctx // c, ctx_out)
    scan_pass(n_ctx // c, s_len // c, True)
    finish_pass(pgx_ref, n_ctx, ox_ref, s_len)
    if ctx_out:
        finish_pass(pgc_ref, 0, oc_ref, n_ctx)


def _gla_call(pgx, pgc, wg, bg, gn, s2, lv, ctx_out):
    b, s, _ = pgx.shape
    n_ctx = pgc.shape[1]
    hv = GLA_WIDTH
    n_rows = s + n_ctx
    assert s % (GLA_GROUP_X * GLA_CHUNK) == 0 and n_ctx % (GLA_GROUP * GLA_CHUNK) == 0
    assert (s // GLA_CHUNK) % GLA_SCAN_UNROLL == 0 and (n_ctx // GLA_CHUNK) % GLA_SCAN_UNROLL == 0
    in_specs = [pl.BlockSpec((1, s, PG_W), lambda i: (i, 0, 0)),
                pl.BlockSpec((1, n_ctx, PG_W), lambda i: (i, 0, 0)),
                _const_spec(wg.shape, 1), _const_spec(bg.shape, 1), _const_spec((1, hv), 1),
                _const_spec(s2.shape, 1), _const_spec(lv.shape, 1)]
    out_specs = [pl.BlockSpec((1, s, hv), lambda i: (i, 0, 0))]
    out_shape = [jax.ShapeDtypeStruct((b, s, hv), BF16)]
    if ctx_out:
        out_specs.append(pl.BlockSpec((1, n_ctx, hv), lambda i: (i, 0, 0)))
        out_shape.append(jax.ShapeDtypeStruct((b, n_ctx, hv), BF16))
    scratch = [pltpu.VMEM((2, n_rows, hv), F32),
               pltpu.VMEM((2, n_rows, GLA_QK_W), BF16),
               pltpu.VMEM((2, n_rows // GLA_CHUNK, hv, GLA_QK_W), F32),
               pltpu.VMEM((2, n_rows // GLA_CHUNK, 8, GLA_QK_W), F32),
               pltpu.VMEM((2, hv, GLA_QK_W), F32),
               pltpu.VMEM((2, GLA_SROWS, max(GLA_GROUP, GLA_GROUP_X) * GLA_QK_W), F32)]
    res = pl.pallas_call(
        functools.partial(_gla_kernel, ctx_out=ctx_out),
        grid=(b,),
        in_specs=in_specs, out_specs=out_specs, out_shape=out_shape, scratch_shapes=scratch,
        compiler_params=_params(1),
        name="gla",
    )(pgx, pgc, wg, bg, gn.reshape(1, hv), s2, lv)
    return (res[0], res[1]) if ctx_out else (res[0], None)


def _swa_kernel(*refs, ctx_out):
    if ctx_out:
        sink_ref, px_ref, ktx_ref, pc_ref, ktc_ref, ox_ref, oc_ref, sc_ref, bias_ref = refs
    else:
        sink_ref, px_ref, ktx_ref, pc_ref, ktc_ref, ox_ref, sc_ref, bias_ref = refs
    s_len = px_ref.shape[1]
    n_ctx = pc_ref.shape[1]
    blk = SWA_BLOCK
    band = 3 * blk
    grp = SWA_HEADS // SWA_KV_HEADS
    lane = lax.broadcasted_iota(jnp.int32, (1, LANE), 1)
    low = lane < SWA_HD
    half_mask = [low.astype(F32).astype(BF16), (~low).astype(F32).astype(BF16)]
    rel = (lax.broadcasted_iota(jnp.int32, (blk, band), 0)
           - lax.broadcasted_iota(jnp.int32, (blk, band), 1))
    for j in range(band // blk):
        bias_ref[j] = jnp.where(jnp.abs(rel + j * blk) <= SWA_WINDOW, 0.0, -jnp.inf)

    def col(i):
        return slice(i * LANE, (i + 1) * LANE)

    def krows(kv):
        return slice(kv * LANE, (kv + 1) * LANE)

    def sink_col(n_rows, kv, half):
        ha = grp * kv + half
        return jnp.concatenate([jnp.full((n_rows, 1), sink_ref[ha], F32),
                                jnp.full((n_rows, 1), sink_ref[ha + 2], F32)], axis=0)

    def softmax_pv(sc, v, snk):
        m = jnp.maximum(snk, jnp.max(sc(), axis=-1, keepdims=True))
        p = jnp.exp2(sc() - m).astype(BF16)
        r = _dot(p, jnp.concatenate([v, jnp.ones(v.shape, BF16)], axis=1))
        return r[:, :LANE] / (r[:, LANE:] + jnp.exp2(snk - m))

    def pipeline(tasks, depth, carried=(), prefetch=()):
        for t in range(len(carried), min(depth, len(tasks))):
            sc_ref[t] = tasks[t][0]()
        outs = []
        for t, (_, finish) in enumerate(tasks):
            ahead = t + depth
            if ahead < len(tasks):
                sc_ref[ahead] = tasks[ahead][0]()
            elif ahead - len(tasks) < len(prefetch):
                prefetch[ahead - len(tasks)]()
            outs.append(finish(lambda t=t: sc_ref[t]))
        return outs

    units = [(kv, half) for kv in range(SWA_KV_HEADS) for half in range(2)]
    blocks_per_step = 2
    n_steps = s_len // (blk * blocks_per_step)
    depth = SWA_PIPE_DEPTH

    def block_tasks(n):
        r0 = pl.multiple_of(n * blk, blk)
        start = pl.multiple_of(jnp.clip((n - 1) * blk, 0, s_len - band), blk)
        tasks = []
        for kv, half in units:
            def scores(kv=kv, half=half):
                bias = bias_ref[(r0 - start) // blk]
                q = jnp.concatenate([px_ref[0, pl.ds(r0, blk), col(2 * kv)],
                                     px_ref[0, pl.ds(r0, blk), col(2 * kv + 1)]], axis=0) * half_mask[half]
                kt = jnp.concatenate([ktx_ref[0, krows(kv), pl.ds(start, band)],
                                      ktc_ref[0, krows(kv), :]], axis=1)
                sc = _dot(q, kt)
                return jnp.concatenate([sc[:, :band] + jnp.concatenate([bias, bias], axis=0), sc[:, band:]],
                                       axis=1)

            def finish(cur, kv=kv, half=half):
                v = jnp.concatenate([px_ref[0, pl.ds(start, band), col(4 + kv)],
                                     pc_ref[0, :, col(4 + kv)]], axis=0)
                return softmax_pv(cur, v, sink_col(blk, kv, half))

            tasks.append((scores, finish))
        return tasks, r0

    def prefetch_into(slot, scores):
        def run():
            sc_ref[slot] = scores()
        return run

    def body(i, carry):
        tasks, rows = [], []
        for sub in range(blocks_per_step):
            t, r0 = block_tasks(i * blocks_per_step + sub)
            tasks += t
            rows.append(r0)
        nxt, _ = block_tasks(jnp.minimum(i + 1, n_steps - 1) * blocks_per_step)
        outs = pipeline(tasks, depth, carried=range(depth),
                        prefetch=[prefetch_into(k, nxt[k][0]) for k in range(depth)])
        for sub in range(blocks_per_step):
            for kv in range(SWA_KV_HEADS):
                t = (sub * SWA_KV_HEADS + kv) * 2
                o2 = jnp.where(low, outs[t], outs[t + 1]).astype(BF16)
                ox_ref[0, pl.ds(rows[sub], blk), col(2 * kv)] = o2[:blk]
                ox_ref[0, pl.ds(rows[sub], blk), col(2 * kv + 1)] = o2[blk:]
        return carry

    first, _ = block_tasks(jnp.int32(0))
    for k in range(depth):
        sc_ref[k] = first[k][0]()
    lax.fori_loop(0, n_steps, body, 0)

    if ctx_out:
        tasks = []
        for kv, half in units:
            def scores(kv=kv, half=half):
                q = jnp.concatenate([pc_ref[0, :, col(2 * kv)], pc_ref[0, :, col(2 * kv + 1)]], axis=0)
                return _dot(q * half_mask[half], ktc_ref[0, krows(kv), :])

            def finish(cur, kv=kv, half=half):
                return softmax_pv(cur, pc_ref[0, :, col(4 + kv)], sink_col(n_ctx, kv, half))

            tasks.append((scores, finish))
        outs = []
        nxt = tasks[0][0]()
        for t, (_, finish) in enumerate(tasks):
            cur = nxt
            if t + 1 < len(tasks):
                nxt = tasks[t + 1][0]()
            outs.append(finish(lambda cur=cur: cur))
        for kv in range(SWA_KV_HEADS):
            o2 = jnp.where(low, outs[2 * kv], outs[2 * kv + 1]).astype(BF16)
            oc_ref[0, :, col(2 * kv)] = o2[:n_ctx]
            oc_ref[0, :, col(2 * kv + 1)] = o2[n_ctx:]


def _swa_call(psx, ktx, psc, ktc, sink, ctx_out):
    b, s, _ = psx.shape
    n_ctx = psc.shape[1]
    assert s % (2 * SWA_BLOCK) == 0 and s >= 3 * SWA_BLOCK and SWA_WINDOW == SWA_BLOCK
    in_specs = [pl.BlockSpec(memory_space=pltpu.SMEM),
                pl.BlockSpec((1, s, PS_W), lambda i: (i, 0, 0)),
                pl.BlockSpec((1, 2 * LANE, s), lambda i: (i, 0, 0)),
                pl.BlockSpec((1, n_ctx, PS_W), lambda i: (i, 0, 0)),
                pl.BlockSpec((1, 2 * LANE, n_ctx), lambda i: (0, 0, i))]
    out_specs = [pl.BlockSpec((1, s, SWA_WIDTH), lambda i: (i, 0, 0))]
    out_shape = [jax.ShapeDtypeStruct((b, s, SWA_WIDTH), BF16)]
    if ctx_out:
        out_specs.append(pl.BlockSpec((1, n_ctx, SWA_WIDTH), lambda i: (i, 0, 0)))
        out_shape.append(jax.ShapeDtypeStruct((b, n_ctx, SWA_WIDTH), BF16))
    res = pl.pallas_call(
        functools.partial(_swa_kernel, ctx_out=ctx_out),
        grid=(b,),
        in_specs=in_specs, out_specs=out_specs, out_shape=out_shape,
        scratch_shapes=[pltpu.VMEM((SWA_UNITS_PER_STEP, 2 * SWA_BLOCK, 3 * SWA_BLOCK + n_ctx), F32),
                        pltpu.VMEM((3, SWA_BLOCK, 3 * SWA_BLOCK), F32)],
        compiler_params=_params(1),
        name="swa",
    )(sink, psx, ktx, psc, ktc)
    return (res[0], res[1]) if ctx_out else (res[0], None)


def _diff_kernel(*refs, ctx_out, lambda_init):
    if ctx_out:
        px_ref, ktx_ref, pc_ref, ktc_ref, lam_ref, gn_ref, ox_ref, oc_ref, kt_ref, va_ref, sc_ref = refs
    else:
        px_ref, ktx_ref, pc_ref, ktc_ref, lam_ref, gn_ref, ox_ref, kt_ref, va_ref, sc_ref = refs
    s_len = px_ref.shape[1]
    w = DIFF_QK_W
    lv = lam_ref[...]
    lam = (jnp.exp(jnp.sum(lv[0:1] * lv[1:2], axis=-1, keepdims=True))
           - jnp.exp(jnp.sum(lv[2:3] * lv[3:4], axis=-1, keepdims=True)) + lambda_init)
    lane = lax.broadcasted_iota(jnp.int32, (1, w), 1)
    row2 = lax.broadcasted_iota(jnp.int32, (w, w), 0)
    col2 = lax.broadcasted_iota(jnp.int32, (w, w), 1)
    head_ones = ((row2 // DIFF_V) == (col2 // DIFF_V)).astype(F32).astype(BF16)
    gn = gn_ref[...] * (1.0 - lambda_init)
    unit_masks = [((lane >= DIFF_QK * u) & (lane < DIFF_QK * (u + 1))).astype(F32).astype(BF16)
                  for u in range(2 * DIFF_HEADS)]
    head_masks = [(lane >= DIFF_V * h) & (lane < DIFF_V * (h + 1)) for h in range(DIFF_HEADS)]

    kt_ref[:, :s_len] = ktx_ref[0]
    kt_ref[:, s_len:] = ktc_ref[0]
    for side in range(2):
        keep = ((lane >= side * (w // 2)) & (lane < (side + 1) * (w // 2))).astype(F32).astype(BF16)
        va_ref[side, :s_len, :] = px_ref[0, :, w:2 * w] * keep + (1.0 - keep)
        va_ref[side, s_len:, :] = pc_ref[0, :, w:2 * w] * keep + (1.0 - keep)

    def attend(q, k0, q_next=None):
        def scores(qq, u):
            return _dot(qq * unit_masks[u], kt_ref[:, k0:])

        def softmax_pv(sc, u):
            m = jnp.max(sc, axis=-1, keepdims=True)
            return _dot(jnp.exp2(sc - m).astype(BF16), va_ref[(u // 2) // (DIFF_HEADS // 2), k0:, :])

        units = []
        nxt = scores(q, 0) if q_next is None else sc_ref[...]
        for u in range(2 * DIFF_HEADS):
            cur = nxt
            if u + 1 < 2 * DIFF_HEADS:
                nxt = scores(q, u + 1)
            elif q_next is not None:
                sc_ref[...] = scores(q_next, 0)
            units.append(softmax_pv(cur, u))
        o = jnp.zeros((q.shape[0], w), F32)
        for h in range(DIFF_HEADS):
            r1, r2 = units[2 * h], units[2 * h + 1]
            oh = r1 / pltpu.roll(r1, w // 2, 1) - lam * (r2 / pltpu.roll(r2, w // 2, 1))
            o = jnp.where(head_masks[h], oh, o)
        hi, lo = _split_bf16(o * o)
        ms = (_dot(hi, head_ones) + _dot(lo, head_ones)) * (1.0 / DIFF_V)
        return (o * lax.rsqrt(ms + RMS_EPS) * gn).astype(BF16)

    n_blocks = s_len // DIFF_QBLOCK

    def q_block(n):
        return px_ref[0, pl.ds(pl.multiple_of(n * DIFF_QBLOCK, DIFF_QBLOCK), DIFF_QBLOCK), 0:w]

    def body(n, carry):
        out = attend(q_block(n), 0, q_block(jnp.minimum(n + 1, n_blocks - 1)))
        ox_ref[0, pl.ds(pl.multiple_of(n * DIFF_QBLOCK, DIFF_QBLOCK), DIFF_QBLOCK), :] = out
        return carry

    sc_ref[...] = _dot(q_block(0) * unit_masks[0], kt_ref[...])
    lax.fori_loop(0, n_blocks, body, 0)
    if ctx_out:
        oc_ref[0] = attend(pc_ref[0, :, 0:w], s_len)


def _diff_call(pdx, ktx, pdc, ktc, lam, gn, lambda_init, ctx_out):
    b, s, _ = pdx.shape
    n_ctx = pdc.shape[1]
    assert s % DIFF_QBLOCK == 0
    in_specs = [pl.BlockSpec((1, s, PD_W), lambda i: (i, 0, 0)),
                pl.BlockSpec((1, DIFF_QK_W, s), lambda i: (i, 0, 0)),
                pl.BlockSpec((1, n_ctx, PD_W), lambda i: (i, 0, 0)),
                pl.BlockSpec((1, DIFF_QK_W, n_ctx), lambda i: (0, 0, i)),
                _const_spec(lam.shape, 1), _const_spec((1, DIFF_WIDTH), 1)]
    out_specs = [pl.BlockSpec((1, s, DIFF_WIDTH), lambda i: (i, 0, 0))]
    out_shape = [jax.ShapeDtypeStruct((b, s, DIFF_WIDTH), BF16)]
    if ctx_out:
        out_specs.append(pl.BlockSpec((1, n_ctx, DIFF_WIDTH), lambda i: (i, 0, 0)))
        out_shape.append(jax.ShapeDtypeStruct((b, n_ctx, DIFF_WIDTH), BF16))
    res = pl.pallas_call(
        functools.partial(_diff_kernel, ctx_out=ctx_out, lambda_init=lambda_init),
        grid=(b,),
        in_specs=in_specs, out_specs=out_specs, out_shape=out_shape,
        scratch_shapes=[pltpu.VMEM((DIFF_QK_W, s + n_ctx), BF16),
                        pltpu.VMEM((2, s + n_ctx, DIFF_WIDTH), BF16),
                        pltpu.VMEM((DIFF_QBLOCK, s + n_ctx), F32)],
        compiler_params=_params(1),
        name="diff",
    )(pdx, ktx, pdc, ktc, lam, gn.reshape(1, DIFF_WIDTH))
    return (res[0], res[1]) if ctx_out else (res[0], None)


def _prep_w_in(w):
    w = w.astype(BF16)
    starts = np.concatenate([[0], np.cumsum(IN_SIZES)])
    gq, gk, gv, gf, gb, og, sq, sk, sv, dq, dk, dv = range(len(IN_SIZES))

    def cols(first, last):
        return w[..., starts[first]:starts[last + 1]]

    pad = jnp.zeros(w.shape[:-1] + (PG_W - (2 * GLA_QK_W + 2 * GLA_WIDTH + 2 * GLA_GATE_RANK),), BF16)
    v0 = w[..., starts[sv]:starts[sv] + SWA_HD]
    v1 = w[..., starts[sv] + SWA_HD:starts[sv + 1]]
    return jnp.concatenate([cols(gq, gv), cols(og, og), cols(gf, gb), pad, cols(sq, sk), v0, v0, v1, v1,
                            cols(dq, dv)], axis=-1)


def _prep_gate(w_gate, b_gate):
    wg = jnp.zeros((2, LANE, GLA_QK_W), F32)
    for d in range(2):
        wg = wg.at[d, GLA_GATE_RANK * d:GLA_GATE_RANK * (d + 1), :].set(w_gate[d])
    return wg, b_gate.reshape(2, 1, GLA_QK_W)


def _axial_angles(rows, head_dim):
    half = head_dim // 2
    row = np.repeat(np.arange(rows, dtype=np.float64), GRID_W)
    col = np.tile(np.arange(GRID_W, dtype=np.float64), rows)
    inv_freq = 1.0 / (ROPE_BASE ** (np.arange(0, half, 2, dtype=np.float64) / half))

    def axis_angles(pos):
        a = pos[:, None] * inv_freq[None, :]
        return np.concatenate([a, a], axis=-1)

    return np.concatenate([axis_angles(row), axis_angles(col)], axis=-1)


def _rope_table(seq):
    rows = seq // GRID_W
    blocks = []
    for head_dim, qscale in ((SWA_HD, SWA_QSCALE), (DIFF_QK, DIFF_QSCALE)):
        ang = _axial_angles(rows, head_dim)
        quarter = head_dim // 4
        sign = np.where((np.arange(head_dim) % (2 * quarter)) < quarter, -1.0, 1.0)
        reps = LANE // head_dim
        cos = np.tile(np.cos(ang), (1, reps))
        sin = np.tile(np.sin(ang) * sign[None, :], (1, reps))
        blocks += [cos * qscale, sin * qscale, cos, sin]
    return jnp.asarray(np.concatenate(blocks, axis=1), F32)


def kernel(x, c, ctx, c_ctx, w_mod, b_mod, g_ffn1, w_ffn1_in, w_ffn1_out, g_mix, w_in, w_out, w_gla_gate,
           b_gla_gate, g_gla_norm, swa_sink, diff_lambda, g_diff_norm, g_ffn2, w_ffn2_in, w_ffn2_out, g_final):
    b, s, d = x.shape
    n_ctx = ctx.shape[1]
    depth = w_mod.shape[0]

    cs = jnp.zeros((MOD_ROWS, d), F32).at[:b].set(c).at[b].set(c_ctx)
    mod = _mod_call(cs, w_mod, b_mod).reshape(depth, MOD_ROWS, N_MOD, d)
    tab = _rope_table(s)
    s2, lv = _gla_constants()

    w1i, w1o = w_ffn1_in.astype(BF16), w_ffn1_out.astype(BF16)
    w2i, w2o = w_ffn2_in.astype(BF16), w_ffn2_out.astype(BF16)
    wi = _prep_w_in(w_in)
    wo = w_out.astype(BF16)

    hx = x
    hc = ctx.reshape(1, b * n_ctx, d)
    for l in range(depth):
        ctx_out = l < depth - 1
        last = l == depth - 1
        mod_x = mod[l, :b]
        mod_c = mod[l, b:b + 1]
        lambda_init = 0.8 - 0.6 * math.exp(-0.3 * l)
        wg, bg = _prep_gate(w_gla_gate[l], b_gla_gate[l])
        sink = swa_sink[l] * LOG2E

        hx = _ffn_call(hx, mod_x, g_ffn1[l], w1i, w1o, l, 0)
        hc = _ffn_call(hc, mod_c, g_ffn1[l], w1i, w1o, l, 0)
        pgx, psx, pdx, ksx, ktx = _inproj_call(hx, mod_x, g_mix[l], wi, l, tab)
        pgc, psc, pdc, ksc, ktc = _inproj_call(hc, mod_c, g_mix[l], wi, l)
        pgc, psc, pdc = (a.reshape(b, n_ctx, a.shape[-1]) for a in (pgc, psc, pdc))

        gla_x, gla_c = _gla_call(pgx, pgc, wg, bg, g_gla_norm[l], s2, lv, ctx_out)
        swa_x, swa_c = _swa_call(psx, ksx, psc, ksc, sink, ctx_out)
        dif_x, dif_c = _diff_call(pdx, ktx, pdc, ktc, diff_lambda[l], g_diff_norm[l], lambda_init, ctx_out)

        hx = _ffn_call(hx, mod_x, g_ffn2[l], w2i, w2o, l, 6, mix=(gla_x, swa_x, dif_x, wo),
                       g_final=g_final if last else None)
        if ctx_out:
            flat = lambda a: a.reshape(1, b * n_ctx, a.shape[-1])
            hc = _ffn_call(hc, mod_c, g_ffn2[l], w2i, w2o, l, 6, mix=(flat(gla_c), flat(swa_c), flat(dif_c), wo))
    return hx
```
